```python
import math
import jax, jax.numpy as jnp
from jax import lax
import numpy as np

D_MODEL = 1024
BATCH = 4
SEQ = 8192
DEPTH = 1

CHUNK = 64
MIX_WIDTH = D_MODEL
LRU_WIDTH = MIX_WIDTH // 2
ATT_WIDTH = MIX_WIDTH - LRU_WIDTH
LRU_BLOCKS = 8
LRU_BLOCK_W = LRU_WIDTH // LRU_BLOCKS
CONV_W = 4
LRU_C = 8.0
ATT_HEADS = 8
ATT_HEAD_DIM = ATT_WIDTH // ATT_HEADS
Q_BLOCK = 128
D_FF = -(-8 * D_MODEL // (3 * 256)) * 256
IN_COLS = 2 * LRU_WIDTH + 3 * ATT_WIDTH
EPS = 1e-6

kernel_name = "hymba_rglru_stickbreaking_block"


def rms_norm(x, g):
    xf = x.astype(jnp.float32)
    xf = xf * lax.rsqrt(jnp.mean(xf * xf, axis=-1, keepdims=True) + EPS)
    return xf.astype(x.dtype) * g


def causal_depthwise_conv(x, w, b):
    s = x.shape[1]
    xp = jnp.pad(x, ((0, 0), (CONV_W - 1, 0), (0, 0)))
    y = b
    for i in range(CONV_W):
        y = y + xp[:, i:i + s, :] * w[i]
    return y


def block_diag_linear(x, w, b):
    bsz, s, _ = x.shape
    xb = x.reshape(bsz, s, LRU_BLOCKS, LRU_BLOCK_W)
    y = jnp.einsum('bsnc,ncd->bsnd', xb, w) + b
    return y.reshape(bsz, s, LRU_WIDTH)


def rg_lru(x, w_rg, b_rg, w_ig, b_ig, lam):
    r = jax.nn.sigmoid(block_diag_linear(x, w_rg, b_rg).astype(jnp.float32))
    i = jax.nn.sigmoid(block_diag_linear(x, w_ig, b_ig).astype(jnp.float32))
    log_a = -LRU_C * r * jax.nn.softplus(-lam.astype(jnp.float32))
    a = jnp.exp(log_a)
    mult = jnp.sqrt(-jnp.expm1(2.0 * log_a))
    bterm = mult * (i * x.astype(jnp.float32))

    def combine(e1, e2):
        a1, b1 = e1
        a2, b2 = e2
        return a1 * a2, a2 * b1 + b2

    _, h = lax.associative_scan(combine, (a, bterm), axis=1)
    return h.astype(x.dtype)


def stick_breaking_attention(q, k, v):
    bsz, s, h, dh = q.shape
    nb = s // Q_BLOCK
    scale = 1.0 / math.sqrt(dh)
    qb = q.reshape(bsz, nb, Q_BLOCK, h, dh).transpose(1, 0, 2, 3, 4)
    key_pos = jnp.arange(s)

    def one_block(args):
        q_blk, blk = args
        z = jnp.einsum('bqhd,bkhd->bhqk', q_blk, k).astype(jnp.float32) * scale
        t = blk * Q_BLOCK + jnp.arange(Q_BLOCK)
        mask = key_pos[None, :] < t[:, None]
        log_beta = jax.nn.log_sigmoid(z)
        log_1mb = jnp.where(mask, jax.nn.log_sigmoid(-z), 0.0)
        suffix = lax.cumsum(log_1mb, axis=3, reverse=True) - log_1mb
        att = jnp.where(mask, jnp.exp(log_beta + suffix), 0.0)
        return jnp.einsum('bhqk,bkhd->bqhd', att.astype(v.dtype), v)

    out = lax.map(one_block, (qb, jnp.arange(nb)))
    return out.transpose(1, 0, 2, 3, 4).reshape(bsz, s, h, dh)


def setup_inputs(seed: int = 0) -> dict:
    key = jax.random.key(seed)
    ks = jax.random.split(key, 20)
    f32 = jnp.float32
    nrm = lambda k, shp, sc: jax.random.normal(k, shp, f32) * sc
    gain = lambda k, shp: 1.0 + 0.05 * jax.random.normal(k, shp, f32)
    x = jax.random.normal(ks[0], (BATCH, SEQ, D_MODEL), f32)
    u = jax.random.uniform(ks[9], (DEPTH, LRU_WIDTH), f32, 0.9, 0.999)
    a0 = u ** (1.0 / LRU_C)
    lru_lambda = jnp.log(a0) - jnp.log1p(-a0)
    return {
        "x": x,
        "norm_mix": gain(ks[1], (DEPTH, D_MODEL)),
        "w_in": nrm(ks[2], (DEPTH, D_MODEL, IN_COLS), D_MODEL ** -0.5),
        "conv_w": nrm(ks[3], (DEPTH, CONV_W, LRU_WIDTH), CONV_W ** -0.5),
        "conv_b": nrm(ks[4], (DEPTH, LRU_WIDTH), 0.01),
        "w_rg": nrm(ks[5], (DEPTH, LRU_BLOCKS, LRU_BLOCK_W, LRU_BLOCK_W), LRU_BLOCK_W ** -0.5),
        "b_rg": nrm(ks[6], (DEPTH, LRU_BLOCKS, LRU_BLOCK_W), 0.01),
        "w_ig": nrm(ks[7], (DEPTH, LRU_BLOCKS, LRU_BLOCK_W, LRU_BLOCK_W), LRU_BLOCK_W ** -0.5),
        "b_ig": nrm(ks[8], (DEPTH, LRU_BLOCKS, LRU_BLOCK_W), 0.01),
        "lru_lambda": lru_lambda,
        "norm_lru_out": gain(ks[10], (DEPTH, LRU_WIDTH)),
        "norm_att_out": gain(ks[11], (DEPTH, ATT_WIDTH)),
        "w_out": nrm(ks[12], (DEPTH, MIX_WIDTH, D_MODEL), MIX_WIDTH ** -0.5),
        "norm_ffn": gain(ks[13], (DEPTH, D_MODEL)),
        "w_ffn_in": nrm(ks[14], (DEPTH, D_MODEL, 2 * D_FF), D_MODEL ** -0.5),
        "w_ffn_out": nrm(ks[15], (DEPTH, D_FF, D_MODEL), D_FF ** -0.5),
        "norm_final": gain(ks[16], (D_MODEL,)),
    }


def reference(x, norm_mix, w_in, conv_w, conv_b, w_rg, b_rg, w_ig, b_ig, lru_lambda,
              norm_lru_out, norm_att_out, w_out, norm_ffn, w_ffn_in, w_ffn_out, norm_final):
    bsz, s, _ = x.shape
    for l in range(DEPTH):
        h = rms_norm(x, norm_mix[l])
        u = jnp.einsum('bsd,de->bse', h, w_in[l])
        o = 0
        lru_x = u[..., o:o + LRU_WIDTH]; o += LRU_WIDTH
        lru_g = u[..., o:o + LRU_WIDTH]; o += LRU_WIDTH
        q = u[..., o:o + ATT_WIDTH].reshape(bsz, s, ATT_HEADS, ATT_HEAD_DIM); o += ATT_WIDTH
        k = u[..., o:o + ATT_WIDTH].reshape(bsz, s, ATT_HEADS, ATT_HEAD_DIM); o += ATT_WIDTH
        v = u[..., o:o + ATT_WIDTH].reshape(bsz, s, ATT_HEADS, ATT_HEAD_DIM)

        xc = causal_depthwise_conv(lru_x, conv_w[l], conv_b[l])
        hl = rg_lru(xc, w_rg[l], b_rg[l], w_ig[l], b_ig[l], lru_lambda[l])
        y_lru = jax.nn.gelu(lru_g) * hl

        y_att = stick_breaking_attention(q, k, v).reshape(bsz, s, ATT_WIDTH)

        y = jnp.concatenate([rms_norm(y_lru, norm_lru_out[l]),
                             rms_norm(y_att, norm_att_out[l])], axis=-1)
        x = x + jnp.einsum('bse,ed->bsd', y, w_out[l])

        h2 = rms_norm(x, norm_ffn[l])
        gu = jnp.einsum('bsd,df->bsf', h2, w_ffn_in[l])
        x = x + jnp.einsum('bsf,fd->bsd', jax.nn.silu(gu[..., :D_FF]) * gu[..., D_FF:], w_ffn_out[l])
    return rms_norm(x, norm_final)
```

```python
import functools
import math

import jax
import jax.numpy as jnp
from jax import lax
from jax.experimental import pallas as pl
from jax.experimental.pallas import tpu as pltpu

F32 = jnp.float32
BF16 = jnp.bfloat16

EPS = 1e-6
LRU_C = 8.0
CONV_W = 4
HEAD_DIM = 64

LANES = 128
SUBLANES = 8
MXU_DIM = 256
VMEM_LIMIT_BYTES = 56 * 1024 * 1024

ATT_DEAD_LOG = -110.0


def _rms_scale(x):
    return lax.rsqrt(jnp.mean(x * x, axis=-1, keepdims=True) + EPS)


def _sigmoid(x):
    return 1.0 / (1.0 + jnp.exp(-x))


def _softplus(x):
    return jnp.maximum(x, 0.0) + jnp.log(1.0 + jnp.exp(-jnp.abs(x)))


def _gelu_tanh(x):
    c = math.sqrt(2.0 / math.pi)
    return 0.5 * x * (1.0 + jnp.tanh(c * (x + 0.044715 * (x * x * x))))


def _inproj_kernel(x_ref, g_ref, w_ref, lru_ref, qkv_ref, *, lru_cols):
    x = x_ref[...]
    h = (x * _rms_scale(x)) * g_ref[...]
    u = jnp.dot(h.astype(BF16), w_ref[...], preferred_element_type=F32)
    lru_ref[...] = u[:, :lru_cols]
    qkv_ref[...] = u[:, lru_cols:].astype(BF16)


def _inproj(x2, gain, w_in_bf16, lru_cols, tm):
    n, d = x2.shape
    cols = w_in_bf16.shape[1]
    return pl.pallas_call(
        functools.partial(_inproj_kernel, lru_cols=lru_cols),
        grid=(n // tm,),
        in_specs=[
            pl.BlockSpec((tm, d), lambda i: (i, 0)),
            pl.BlockSpec((1, d), lambda i: (0, 0)),
            pl.BlockSpec((d, cols), lambda i: (0, 0), pipeline_mode=pl.Buffered(1)),
        ],
        out_specs=[
            pl.BlockSpec((tm, lru_cols), lambda i: (i, 0)),
            pl.BlockSpec((tm, cols - lru_cols), lambda i: (i, 0)),
        ],
        out_shape=[
            jax.ShapeDtypeStruct((n, lru_cols), F32),
            jax.ShapeDtypeStruct((n, cols - lru_cols), BF16),
        ],
        compiler_params=pltpu.CompilerParams(
            dimension_semantics=("arbitrary",), vmem_limit_bytes=VMEM_LIMIT_BYTES),
        name="inproj",
    )(x2, gain, w_in_bf16)


def _lru_kernel(xg_ref, cw_ref, cb_ref, wrg_ref, brg_ref, wig_ref, big_ref, lam_ref, gn_ref,
                out_ref, xs_ref, a_ref, b_ref, hc_ref, *, ts, width):
    si = pl.program_id(1)
    halo = SUBLANES

    @pl.when(si == 0)
    def _():
        xs_ref[0:halo, :] = jnp.zeros((halo, width), F32)
        hc_ref[...] = jnp.zeros((SUBLANES, width), F32)

    xs_ref[halo:halo + ts, :] = xg_ref[0, :, 0:width]

    xc = jnp.broadcast_to(cb_ref[...], (ts, width))
    for i in range(CONV_W):
        off = halo - (CONV_W - 1) + i
        xc = xc + xs_ref[off:off + ts, :] * cw_ref[i:i + 1, :]
    xs_ref[0:halo, :] = xs_ref[ts:ts + halo, :]

    xcb = xc.astype(BF16)
    rg, ig = [], []
    for c in range(width // MXU_DIM):
        sl = slice(c * MXU_DIM, (c + 1) * MXU_DIM)
        rg.append(jnp.dot(xcb[:, sl], wrg_ref[c], preferred_element_type=F32))
        ig.append(jnp.dot(xcb[:, sl], wig_ref[c], preferred_element_type=F32))
    r = _sigmoid(jnp.concatenate(rg, axis=1) + brg_ref[...])
    ig = _sigmoid(jnp.concatenate(ig, axis=1) + big_ref[...])

    log_a = (-LRU_C * r) * _softplus(-lam_ref[...])
    a = jnp.exp(log_a)
    th = jnp.tanh(log_a)
    mult = jnp.sqrt((-2.0 * th) / (1.0 - th))
    a_ref[...] = a
    b_ref[...] = mult * (ig * xc)

    row = lax.broadcasted_iota(jnp.int32, (SUBLANES, width), 0)

    def group(g, hprev):
        base = pl.multiple_of(g * SUBLANES, SUBLANES)
        a8 = a_ref[pl.ds(base, SUBLANES), :]
        b8 = b_ref[pl.ds(base, SUBLANES), :]
        for d in (1, 2, 4):
            keep = row >= d
            b8 = jnp.where(keep, b8 + a8 * pltpu.roll(b8, d, axis=0), b8)
            a8 = jnp.where(keep, a8 * pltpu.roll(a8, d, axis=0), a8)
        h8 = b8 + a8 * hprev
        b_ref[pl.ds(base, SUBLANES), :] = h8
        return jnp.broadcast_to(h8[SUBLANES - 1:SUBLANES, :], (SUBLANES, width))

    hc_ref[...] = lax.fori_loop(0, ts // SUBLANES, group, hc_ref[...], unroll=4)

    y = _gelu_tanh(xg_ref[0, :, width:2 * width]) * b_ref[...]
    out_ref[0] = ((y * _rms_scale(y)) * gn_ref[...]).astype(out_ref.dtype)


def _lru(u_lru, conv_w, conv_b, wrg_bd, b_rg, wig_bd, b_ig, lam, gain, ts):
    bsz, s, two_w = u_lru.shape
    width = two_w // 2
    vec = pl.BlockSpec((1, width), lambda b, i: (0, 0))
    wspec = pl.BlockSpec(wrg_bd.shape, lambda b, i: (0, 0, 0))
    return pl.pallas_call(
        functools.partial(_lru_kernel, ts=ts, width=width),
        grid=(bsz, s // ts),
        in_specs=[
            pl.BlockSpec((1, ts, two_w), lambda b, i: (b, i, 0)),
            pl.BlockSpec((CONV_W, width), lambda b, i: (0, 0)),
            vec, wspec, vec, wspec, vec, vec, vec,
        ],
        out_specs=pl.BlockSpec((1, ts, width), lambda b, i: (b, i, 0)),
        out_shape=jax.ShapeDtypeStruct((bsz, s, width), BF16),
        scratch_shapes=[
            pltpu.VMEM((ts + SUBLANES, width), F32),
            pltpu.VMEM((ts, width), F32),
            pltpu.VMEM((ts, width), F32),
            pltpu.VMEM((SUBLANES, width), F32),
        ],
        compiler_params=pltpu.CompilerParams(
            dimension_semantics=("arbitrary", "arbitrary"), vmem_limit_bytes=VMEM_LIMIT_BYTES),
        name="rglru",
    )(u_lru, conv_w, conv_b, wrg_bd, b_rg, wig_bd, b_ig, lam, gain)


def _block_diag(w, per):
    nb, bw, _ = w.shape
    w = w.reshape(nb // per, per, bw, bw)
    eye = jnp.eye(per, dtype=w.dtype)
    return jnp.einsum('gpij,pq->gpiqj', w, eye).reshape(nb // per, per * bw, per * bw)


def _attn_kernel(q_ref, k_ref, v_ref, o_ref, r_ref, acc_ref, *, tq, tk, scale):
    qi = pl.program_id(2)
    nsub = tq // tk

    q = (q_ref[0].astype(F32) * scale).astype(BF16)
    lane_q = lax.broadcasted_iota(jnp.int32, (tq, LANES), 1)
    q_heads = (jnp.where(lane_q < HEAD_DIM, q, jnp.zeros_like(q)),
               jnp.where(lane_q >= HEAD_DIM, q, jnp.zeros_like(q)))

    rr = lax.broadcasted_iota(jnp.int32, (2 * tk, 2 * tk), 0)
    cc = lax.broadcasted_iota(jnp.int32, (2 * tk, 2 * tk), 1)
    key_j = jnp.where(rr >= tk, rr - tk, rr)
    tri = jnp.where((cc >= tk) | (key_j > cc), 1.0, 0.0).astype(BF16)

    r_ref[...] = jnp.zeros(r_ref.shape, F32)
    acc_ref[...] = jnp.zeros(acc_ref.shape, F32)

    def sweep(k_blk, v_blk, row0, masked):
        rows = tq - row0
        if masked:
            rl = lax.broadcasted_iota(jnp.int32, (rows, tk), 0)
            kl = lax.broadcasted_iota(jnp.int32, (rows, tk), 1)
            keep = kl < rl
        for h in range(2):
            z = lax.dot_general(q_heads[h][row0:, :], k_blk, (((1,), (1,)), ((), ())),
                                preferred_element_type=F32)
            log_beta = jnp.minimum(z, 0.0) - jnp.log(1.0 + jnp.exp(-jnp.abs(z)))
            log_1mb = log_beta - z
            if masked:
                log_1mb = jnp.where(keep, log_1mb, 0.0)
            hi = log_1mb.astype(BF16)
            lo = (log_1mb - hi.astype(F32)).astype(BF16)
            cs = jnp.dot(jnp.concatenate([hi, lo], axis=1), tri, preferred_element_type=F32)
            r_old = r_ref[h, row0:, :]
            att = jnp.exp(log_beta + cs[:, :tk] + r_old)
            if masked:
                att = jnp.where(keep, att, 0.0)
            acc_ref[h, row0:, :] += jnp.dot(att.astype(BF16), v_blk, preferred_element_type=F32)
            r_ref[h, row0:, :] = r_old + cs[:, tk:]

    q0 = qi * tq
    for c in reversed(range(nsub)):
        ks = pl.multiple_of(q0 + c * tk, tk)
        sweep(k_ref[0, pl.ds(ks, tk), :], v_ref[0, pl.ds(ks, tk), :], c * tk, True)

    def cond(carry):
        j, alive = carry
        return jnp.logical_and(j >= 0, alive)

    def body(carry):
        j, _ = carry
        ks = pl.multiple_of(j * tk, tk)
        sweep(k_ref[0, pl.ds(ks, tk), :], v_ref[0, pl.ds(ks, tk), :], 0, False)
        return j - 1, jnp.max(r_ref[...]) >= ATT_DEAD_LOG

    lax.while_loop(cond, body, (qi * nsub - 1, True))

    o_ref[0] = jnp.where(lane_q < HEAD_DIM, acc_ref[0], acc_ref[1])


def _attention(qkv, tq, tk):
    bsz, s, three_w = qkv.shape
    width = three_w // 3
    ngrp = width // LANES
    return pl.pallas_call(
        functools.partial(_attn_kernel, tq=tq, tk=tk, scale=1.0 / math.sqrt(HEAD_DIM)),
        grid=(bsz, ngrp, s // tq),
        in_specs=[
            pl.BlockSpec((1, tq, LANES), lambda b, g, i: (b, i, g)),
            pl.BlockSpec((1, s, LANES), lambda b, g, i: (b, 0, ngrp + g)),
            pl.BlockSpec((1, s, LANES), lambda b, g, i: (b, 0, 2 * ngrp + g)),
        ],
        out_specs=pl.BlockSpec((1, tq, LANES), lambda b, g, i: (b, i, g)),
        out_shape=jax.ShapeDtypeStruct((bsz, s, width), F32),
        scratch_shapes=[
            pltpu.VMEM((2, tq, tk), F32),
            pltpu.VMEM((2, tq, LANES), F32),
        ],
        compiler_params=pltpu.CompilerParams(
            dimension_semantics=("arbitrary", "arbitrary", "arbitrary"),
            vmem_limit_bytes=VMEM_LIMIT_BYTES),
        name="sb_attention",
    )(qkv, qkv, qkv)


def _mix_ffn_kernel(x_ref, yl_ref, ya_ref, ga_ref, wo_ref, gf_ref, wi_ref, wd_ref, gl_ref,
                    out_ref, *, d_ff, ff_chunk, final_norm):
    ya = ya_ref[...]
    ya = ((ya * _rms_scale(ya)) * ga_ref[...]).astype(BF16)
    wl = yl_ref.shape[1]
    x1 = x_ref[...] + jnp.dot(yl_ref[...], wo_ref[0:wl, :], preferred_element_type=F32)
    x1 = x1 + jnp.dot(ya, wo_ref[wl:, :], preferred_element_type=F32)

    h2 = ((x1 * _rms_scale(x1)) * gf_ref[...]).astype(BF16)
    ffn = jnp.zeros_like(x1)
    for c in range(d_ff // ff_chunk):
        lo = c * ff_chunk
        gate = jnp.dot(h2, wi_ref[:, lo:lo + ff_chunk], preferred_element_type=F32)
        up = jnp.dot(h2, wi_ref[:, d_ff + lo:d_ff + lo + ff_chunk], preferred_element_type=F32)
        act = (gate * _sigmoid(gate)) * up
        ffn = ffn + jnp.dot(act.astype(BF16), wd_ref[lo:lo + ff_chunk, :],
                            preferred_element_type=F32)
    out = x1 + ffn
    if final_norm:
        out = (out * _rms_scale(out)) * gl_ref[...]
    out_ref[...] = out


def _mix_ffn(x2, y_lru, y_att, g_att, w_out, g_ffn, w_ffn_in, w_ffn_out, g_last, tm, ff_chunk,
             final_norm):
    n, d = x2.shape
    d_ff = w_ffn_out.shape[0]
    row = lambda w: pl.BlockSpec((tm, w), lambda i: (i, 0))
    vec = lambda w: pl.BlockSpec((1, w), lambda i: (0, 0))
    whole = lambda a: pl.BlockSpec(a.shape, lambda i: (0, 0), pipeline_mode=pl.Buffered(1))
    return pl.pallas_call(
        functools.partial(_mix_ffn_kernel, d_ff=d_ff, ff_chunk=ff_chunk, final_norm=final_norm),
        grid=(n // tm,),
        in_specs=[row(d), row(y_lru.shape[1]), row(y_att.shape[1]), vec(y_att.shape[1]),
                  whole(w_out), vec(d), whole(w_ffn_in), whole(w_ffn_out), vec(d)],
        out_specs=row(d),
        out_shape=jax.ShapeDtypeStruct((n, d), F32),
        compiler_params=pltpu.CompilerParams(
            dimension_semantics=("arbitrary",), vmem_limit_bytes=VMEM_LIMIT_BYTES),
        name="mix_ffn",
    )(x2, y_lru, y_att, g_att, w_out, g_ffn, w_ffn_in, w_ffn_out, g_last)


def _largest_chunk(total, limit):
    best = LANES
    for c in range(LANES, limit + 1, LANES):
        if total % c == 0:
            best = c
    return best


def kernel(x, norm_mix, w_in, conv_w, conv_b, w_rg, b_rg, w_ig, b_ig, lru_lambda, norm_lru_out,
           norm_att_out, w_out, norm_ffn, w_ffn_in, w_ffn_out, norm_final):
    bsz, s, d = x.shape
    depth = w_in.shape[0]
    lru_w = conv_w.shape[2]
    att_w = norm_att_out.shape[1]
    nblk, blk_w = w_rg.shape[1], w_rg.shape[2]
    assert w_in.shape[2] == 2 * lru_w + 3 * att_w and att_w % LANES == 0
    assert MXU_DIM % blk_w == 0 and lru_w % MXU_DIM == 0

    n = bsz * s
    tm = 512
    ts = 1024
    tq, tk = 512, 128
    per = MXU_DIM // blk_w
    row1 = lambda v: v.reshape(1, -1)

    x2 = x.reshape(n, d)
    for l in range(depth):
        u_lru, qkv = _inproj(x2, row1(norm_mix[l]), w_in[l].astype(BF16), 2 * lru_w, tm)
        y_lru = _lru(u_lru.reshape(bsz, s, 2 * lru_w), conv_w[l], row1(conv_b[l]),
                     _block_diag(w_rg[l], per).astype(BF16), row1(b_rg[l]),
                     _block_diag(w_ig[l], per).astype(BF16), row1(b_ig[l]),
                     row1(lru_lambda[l]), row1(norm_lru_out[l]), ts)
        y_att = _attention(qkv.reshape(bsz, s, 3 * att_w), tq, tk)
        d_ff = w_ffn_out.shape[1]
        x2 = _mix_ffn(x2, y_lru.reshape(n, lru_w), y_att.reshape(n, att_w), row1(norm_att_out[l]),
                      w_out[l].astype(BF16), row1(norm_ffn[l]), w_ffn_in[l].astype(BF16),
                      w_ffn_out[l].astype(BF16), row1(norm_final),
                      tm, _largest_chunk(d_ff, 512), l == depth - 1)
    return x2.reshape(bsz, s, d)
```

```python
import functools
import math

import jax
import jax.numpy as jnp
from jax import lax
from jax.experimental import pallas as pl
from jax.experimental.pallas import tpu as pltpu

F32 = jnp.float32
BF16 = jnp.bfloat16

EPS = 1e-6
LRU_C = 8.0
CONV_W = 4
HEAD_DIM = 64

LANES = 128
SUBLANES = 8
MXU_DIM = 256
VMEM_LIMIT_BYTES = 56 * 1024 * 1024

LOG2_E = 1.4426950408889634
ATT_DEAD_LOG2 = -160.0
MASKED_SCORE = 1e30


def _rms_scale(x):
    return lax.rsqrt(jnp.mean(x * x, axis=-1, keepdims=True) + EPS)


def _sigmoid(x):
    return 1.0 / (1.0 + jnp.exp(-x))


def _softplus(x):
    return jnp.maximum(x, 0.0) + jnp.log(1.0 + jnp.exp(-jnp.abs(x)))


def _gelu_tanh(x):
    c = math.sqrt(2.0 / math.pi)
    return 0.5 * x * (1.0 + jnp.tanh(c * (x + 0.044715 * (x * x * x))))


def _inproj_kernel(x_ref, g_ref, w_ref, lru_ref, qkv_ref, *, lru_cols):
    x = x_ref[...]
    h = (x * _rms_scale(x)) * g_ref[...]
    u = jnp.dot(h.astype(BF16), w_ref[...], preferred_element_type=F32)
    lru_ref[...] = u[:, :lru_cols]
    qkv_ref[...] = u[:, lru_cols:].astype(BF16)


def _inproj(x2, gain, w_in_bf16, lru_cols, tm):
    n, d = x2.shape
    cols = w_in_bf16.shape[1]
    return pl.pallas_call(
        functools.partial(_inproj_kernel, lru_cols=lru_cols),
        grid=(n // tm,),
        in_specs=[
            pl.BlockSpec((tm, d), lambda i: (i, 0)),
            pl.BlockSpec((1, d), lambda i: (0, 0)),
            pl.BlockSpec((d, cols), lambda i: (0, 0), pipeline_mode=pl.Buffered(1)),
        ],
        out_specs=[
            pl.BlockSpec((tm, lru_cols), lambda i: (i, 0)),
            pl.BlockSpec((tm, cols - lru_cols), lambda i: (i, 0)),
        ],
        out_shape=[
            jax.ShapeDtypeStruct((n, lru_cols), F32),
            jax.ShapeDtypeStruct((n, cols - lru_cols), BF16),
        ],
        compiler_params=pltpu.CompilerParams(
            dimension_semantics=("arbitrary",), vmem_limit_bytes=VMEM_LIMIT_BYTES),
        name="inproj",
    )(x2, gain, w_in_bf16)


def _lru_kernel(xg_ref, cw_ref, cb_ref, wrg_ref, brg_ref, wig_ref, big_ref, lam_ref, gn_ref,
                out_ref, xs_ref, a_ref, b_ref, hc_ref, *, ts, width):
    si = pl.program_id(1)
    halo = SUBLANES

    @pl.when(si == 0)
    def _():
        xs_ref[0:halo, :] = jnp.zeros((halo, width), F32)
        hc_ref[...] = jnp.zeros((SUBLANES, width), F32)

    xs_ref[halo:halo + ts, :] = xg_ref[0, :, 0:width]

    xc = jnp.broadcast_to(cb_ref[...], (ts, width))
    for i in range(CONV_W):
        off = halo - (CONV_W - 1) + i
        xc = xc + xs_ref[off:off + ts, :] * cw_ref[i:i + 1, :]
    xs_ref[0:halo, :] = xs_ref[ts:ts + halo, :]

    xcb = xc.astype(BF16)
    rg, ig = [], []
    for c in range(width // MXU_DIM):
        sl = slice(c * MXU_DIM, (c + 1) * MXU_DIM)
        rg.append(jnp.dot(xcb[:, sl], wrg_ref[c], preferred_element_type=F32))
        ig.append(jnp.dot(xcb[:, sl], wig_ref[c], preferred_element_type=F32))
    r = _sigmoid(jnp.concatenate(rg, axis=1) + brg_ref[...])
    ig = _sigmoid(jnp.concatenate(ig, axis=1) + big_ref[...])

    log_a = (-LRU_C * r) * _softplus(-lam_ref[...])
    a = jnp.exp(log_a)
    th = jnp.tanh(log_a)
    mult = jnp.sqrt((-2.0 * th) / (1.0 - th))
    a_ref[...] = a
    b_ref[...] = mult * (ig * xc)

    row = lax.broadcasted_iota(jnp.int32, (SUBLANES, width), 0)

    def group(g, hprev):
        base = pl.multiple_of(g * SUBLANES, SUBLANES)
        a8 = a_ref[pl.ds(base, SUBLANES), :]
        b8 = b_ref[pl.ds(base, SUBLANES), :]
        for d in (1, 2, 4):
            keep = row >= d
            b8 = jnp.where(keep, b8 + a8 * pltpu.roll(b8, d, axis=0), b8)
            a8 = jnp.where(keep, a8 * pltpu.roll(a8, d, axis=0), a8)
        h8 = b8 + a8 * hprev
        b_ref[pl.ds(base, SUBLANES), :] = h8
        return jnp.broadcast_to(h8[SUBLANES - 1:SUBLANES, :], (SUBLANES, width))

    hc_ref[...] = lax.fori_loop(0, ts // SUBLANES, group, hc_ref[...], unroll=4)

    y = _gelu_tanh(xg_ref[0, :, width:2 * width]) * b_ref[...]
    out_ref[0] = ((y * _rms_scale(y)) * gn_ref[...]).astype(out_ref.dtype)


def _lru(u_lru, conv_w, conv_b, wrg_bd, b_rg, wig_bd, b_ig, lam, gain, ts):
    bsz, s, two_w = u_lru.shape
    width = two_w // 2
    vec = pl.BlockSpec((1, width), lambda b, i: (0, 0))
    wspec = pl.BlockSpec(wrg_bd.shape, lambda b, i: (0, 0, 0))
    return pl.pallas_call(
        functools.partial(_lru_kernel, ts=ts, width=width),
        grid=(bsz, s // ts),
        in_specs=[
            pl.BlockSpec((1, ts, two_w), lambda b, i: (b, i, 0)),
            pl.BlockSpec((CONV_W, width), lambda b, i: (0, 0)),
            vec, wspec, vec, wspec, vec, vec, vec,
        ],
        out_specs=pl.BlockSpec((1, ts, width), lambda b, i: (b, i, 0)),
        out_shape=jax.ShapeDtypeStruct((bsz, s, width), BF16),
        scratch_shapes=[
            pltpu.VMEM((ts + SUBLANES, width), F32),
            pltpu.VMEM((ts, width), F32),
            pltpu.VMEM((ts, width), F32),
            pltpu.VMEM((SUBLANES, width), F32),
        ],
        compiler_params=pltpu.CompilerParams(
            dimension_semantics=("arbitrary", "arbitrary"), vmem_limit_bytes=VMEM_LIMIT_BYTES),
        name="rglru",
    )(u_lru, conv_w, conv_b, wrg_bd, b_rg, wig_bd, b_ig, lam, gain)


def _block_diag(w, per):
    nb, bw, _ = w.shape
    w = w.reshape(nb // per, per, bw, bw)
    eye = jnp.eye(per, dtype=w.dtype)
    return jnp.einsum('gpij,pq->gpiqj', w, eye).reshape(nb // per, per * bw, per * bw)


def _attn_kernel(q_ref, k_ref, v_ref, o_ref, r_ref, acc_ref, *, tq, tk, nleft, scale2):
    qi = pl.program_id(2)
    nkb = tq // tk + nleft
    nk = nkb * tk
    q0 = qi * tq
    ws = pl.multiple_of(jnp.maximum(q0 - nleft * tk, 0), tk)

    q = (q_ref[0].astype(F32) * scale2).astype(BF16)
    lane_q = lax.broadcasted_iota(jnp.int32, (tq, LANES), 1)
    q_heads = (jnp.where(lane_q < HEAD_DIM, q, jnp.zeros_like(q)),
               jnp.where(lane_q >= HEAD_DIM, q, jnp.zeros_like(q)))

    rr = lax.broadcasted_iota(jnp.int32, (2 * tk, 2 * tk), 0)
    cc = lax.broadcasted_iota(jnp.int32, (2 * tk, 2 * tk), 1)
    key_j = jnp.where(rr >= tk, rr - tk, rr)
    tri = jnp.where((cc >= tk) | (key_j > cc), 1.0, 0.0).astype(BF16)

    def softplus_parts(z):
        lb = jnp.minimum(z, 0.0) - jnp.log(1.0 + jnp.exp2(-jnp.abs(z))) * LOG2_E
        return lb, lb - z

    def suffix_and_total(l):
        hi = l.astype(BF16)
        lo = (l - hi.astype(F32)).astype(BF16)
        cs = jnp.dot(jnp.concatenate([hi, lo], axis=1), tri, preferred_element_type=F32)
        return cs[:, :tk], cs[:, tk:]

    k_win = k_ref[0, pl.ds(ws, nk), :]
    v_win = v_ref[0, pl.ds(ws, nk), :]
    kl_minus_rl = (lax.broadcasted_iota(jnp.int32, (tq, nk), 1)
                   - lax.broadcasted_iota(jnp.int32, (tq, nk), 0))
    keep = kl_minus_rl < (q0 - ws)
    for h in range(2):
        z = lax.dot_general(q_heads[h], k_win, (((1,), (1,)), ((), ())),
                            preferred_element_type=F32)
        lb, l = softplus_parts(jnp.where(keep, z, -MASKED_SCORE))
        parts = [suffix_and_total(l[:, kb * tk:(kb + 1) * tk]) for kb in range(nkb)]
        r = jnp.zeros((tq, tk), F32)
        atts = [None] * nkb
        for kb in reversed(range(nkb)):
            suffix, total = parts[kb]
            atts[kb] = jnp.exp2(lb[:, kb * tk:(kb + 1) * tk] + suffix + r).astype(BF16)
            r = r + total
        acc_ref[h] = jnp.dot(jnp.concatenate(atts, axis=1), v_win, preferred_element_type=F32)
        r_ref[h] = r

    def cond(carry):
        j, alive = carry
        return jnp.logical_and(j >= 0, alive)

    def body(carry):
        j, _ = carry
        ks = pl.multiple_of(j * tk, tk)
        k_blk = k_ref[0, pl.ds(ks, tk), :]
        v_blk = v_ref[0, pl.ds(ks, tk), :]
        for h in range(2):
            z = lax.dot_general(q_heads[h], k_blk, (((1,), (1,)), ((), ())),
                                preferred_element_type=F32)
            lb, l = softplus_parts(z)
            suffix, total = suffix_and_total(l)
            r_old = r_ref[h]
            att = jnp.exp2(lb + suffix + r_old).astype(BF16)
            acc_ref[h] += jnp.dot(att, v_blk, preferred_element_type=F32)
            r_ref[h] = r_old + total
        return j - 1, jnp.max(r_ref[...]) >= ATT_DEAD_LOG2

    alive0 = jnp.max(r_ref[...]) >= ATT_DEAD_LOG2
    lax.while_loop(cond, body, (ws // tk - 1, alive0))

    o_ref[0] = jnp.where(lane_q < HEAD_DIM, acc_ref[0], acc_ref[1])


def _attention(qkv, tq, tk, nleft):
    bsz, s, three_w = qkv.shape
    width = three_w // 3
    ngrp = width // LANES
    return pl.pallas_call(
        functools.partial(_attn_kernel, tq=tq, tk=tk, nleft=nleft,
                          scale2=LOG2_E / math.sqrt(HEAD_DIM)),
        grid=(bsz, ngrp, s // tq),
        in_specs=[
            pl.BlockSpec((1, tq, LANES), lambda b, g, i: (b, i, g)),
            pl.BlockSpec((1, s, LANES), lambda b, g, i: (b, 0, ngrp + g)),
            pl.BlockSpec((1, s, LANES), lambda b, g, i: (b, 0, 2 * ngrp + g)),
        ],
        out_specs=pl.BlockSpec((1, tq, LANES), lambda b, g, i: (b, i, g)),
        out_shape=jax.ShapeDtypeStruct((bsz, s, width), F32),
        scratch_shapes=[
            pltpu.VMEM((2, tq, tk), F32),
            pltpu.VMEM((2, tq, LANES), F32),
        ],
        compiler_params=pltpu.CompilerParams(
            dimension_semantics=("arbitrary", "arbitrary", "arbitrary"),
            vmem_limit_bytes=VMEM_LIMIT_BYTES),
        name="sb_attention",
    )(qkv, qkv, qkv)


def _mix_ffn_kernel(x_ref, yl_ref, ya_ref, ga_ref, wo_ref, gf_ref, wi_ref, wd_ref, gl_ref,
                    out_ref, *, d_ff, ff_chunk, final_norm):
    ya = ya_ref[...]
    ya = ((ya * _rms_scale(ya)) * ga_ref[...]).astype(BF16)
    wl = yl_ref.shape[1]
    x1 = x_ref[...] + jnp.dot(yl_ref[...], wo_ref[0:wl, :], preferred_element_type=F32)
    x1 = x1 + jnp.dot(ya, wo_ref[wl:, :], preferred_element_type=F32)

    h2 = ((x1 * _rms_scale(x1)) * gf_ref[...]).astype(BF16)
    ffn = jnp.zeros_like(x1)
    for c in range(d_ff // ff_chunk):
        lo = c * ff_chunk
        gate = jnp.dot(h2, wi_ref[:, lo:lo + ff_chunk], preferred_element_type=F32)
        up = jnp.dot(h2, wi_ref[:, d_ff + lo:d_ff + lo + ff_chunk], preferred_element_type=F32)
        act = (gate * _sigmoid(gate)) * up
        ffn = ffn + jnp.dot(act.astype(BF16), wd_ref[lo:lo + ff_chunk, :],
                            preferred_element_type=F32)
    out = x1 + ffn
    if final_norm:
        out = (out * _rms_scale(out)) * gl_ref[...]
    out_ref[...] = out


def _mix_ffn(x2, y_lru, y_att, g_att, w_out, g_ffn, w_ffn_in, w_ffn_out, g_last, tm, ff_chunk,
             final_norm):
    n, d = x2.shape
    d_ff = w_ffn_out.shape[0]
    row = lambda w: pl.BlockSpec((tm, w), lambda i: (i, 0))
    vec = lambda w: pl.BlockSpec((1, w), lambda i: (0, 0))
    whole = lambda a: pl.BlockSpec(a.shape, lambda i: (0, 0), pipeline_mode=pl.Buffered(1))
    return pl.pallas_call(
        functools.partial(_mix_ffn_kernel, d_ff=d_ff, ff_chunk=ff_chunk, final_norm=final_norm),
        grid=(n // tm,),
        in_specs=[row(d), row(y_lru.shape[1]), row(y_att.shape[1]), vec(y_att.shape[1]),
                  whole(w_out), vec(d), whole(w_ffn_in), whole(w_ffn_out), vec(d)],
        out_specs=row(d),
        out_shape=jax.ShapeDtypeStruct((n, d), F32),
        compiler_params=pltpu.CompilerParams(
            dimension_semantics=("arbitrary",), vmem_limit_bytes=VMEM_LIMIT_BYTES),
        name="mix_ffn",
    )(x2, y_lru, y_att, g_att, w_out, g_ffn, w_ffn_in, w_ffn_out, g_last)


def _largest_chunk(total, limit):
    best = LANES
    for c in range(LANES, limit + 1, LANES):
        if total % c == 0:
            best = c
    return best


def kernel(x, norm_mix, w_in, conv_w, conv_b, w_rg, b_rg, w_ig, b_ig, lru_lambda, norm_lru_out,
           norm_att_out, w_out, norm_ffn, w_ffn_in, w_ffn_out, norm_final):
    bsz, s, d = x.shape
    depth = w_in.shape[0]
    lru_w = conv_w.shape[2]
    att_w = norm_att_out.shape[1]
    nblk, blk_w = w_rg.shape[1], w_rg.shape[2]
    assert w_in.shape[2] == 2 * lru_w + 3 * att_w and att_w % LANES == 0
    assert MXU_DIM % blk_w == 0 and lru_w % MXU_DIM == 0

    n = bsz * s
    tm = 512
    ts = 1024
    tq, tk, nleft = 256, 128, 2
    per = MXU_DIM // blk_w
    row1 = lambda v: v.reshape(1, -1)

    x2 = x.reshape(n, d)
    for l in range(depth):
        u_lru, qkv = _inproj(x2, row1(norm_mix[l]), w_in[l].astype(BF16), 2 * lru_w, tm)
        y_lru = _lru(u_lru.reshape(bsz, s, 2 * lru_w), conv_w[l], row1(conv_b[l]),
                     _block_diag(w_rg[l], per).astype(BF16), row1(b_rg[l]),
                     _block_diag(w_ig[l], per).astype(BF16), row1(b_ig[l]),
                     row1(lru_lambda[l]), row1(norm_lru_out[l]), ts)
        y_att = _attention(qkv.reshape(bsz, s, 3 * att_w), tq, tk, nleft)
        d_ff = w_ffn_out.shape[1]
        x2 = _mix_ffn(x2, y_lru.reshape(n, lru_w), y_att.reshape(n, att_w), row1(norm_att_out[l]),
                      w_out[l].astype(BF16), row1(norm_ffn[l]), w_ffn_in[l].astype(BF16),
                      w_ffn_out[l].astype(BF16), row1(norm_final),
                      tm, _largest_chunk(d_ff, 512), l == depth - 1)
    return x2.reshape(bsz, s, d)
```

```python
import functools
import math

import jax
import jax.numpy as jnp
from jax import lax
from jax.experimental import pallas as pl
from jax.experimental.pallas import tpu as pltpu

F32 = jnp.float32
BF16 = jnp.bfloat16

EPS = 1e-6
LRU_C = 8.0
CONV_W = 4
HEAD_DIM = 64

LANES = 128
SUBLANES = 8
MXU_DIM = 256
VMEM_LIMIT_BYTES = 56 * 1024 * 1024

LOG2_E = 1.4426950408889634
ATT_DEAD_LOG2 = -160.0
MASKED_SCORE = 1e30


def _rms_scale(x):
    return lax.rsqrt(jnp.mean(x * x, axis=-1, keepdims=True) + EPS)


def _sigmoid(x):
    return 1.0 / (1.0 + jnp.exp(-x))


def _softplus(x):
    return jnp.maximum(x, 0.0) + jnp.log(1.0 + jnp.exp(-jnp.abs(x)))


def _gelu_tanh(x):
    c = math.sqrt(2.0 / math.pi)
    return 0.5 * x * (1.0 + jnp.tanh(c * (x + 0.044715 * (x * x * x))))


def _inproj_kernel(x_ref, g_ref, w_ref, lru_ref, qkv_ref, *, lru_cols):
    x = x_ref[...]
    h = (x * _rms_scale(x)) * g_ref[...]
    u = jnp.dot(h.astype(BF16), w_ref[...], preferred_element_type=F32)
    lru_ref[...] = u[:, :lru_cols]
    qkv_ref[...] = u[:, lru_cols:].astype(BF16)


def _inproj(x2, gain, w_in_bf16, lru_cols, tm):
    n, d = x2.shape
    cols = w_in_bf16.shape[1]
    return pl.pallas_call(
        functools.partial(_inproj_kernel, lru_cols=lru_cols),
        grid=(n // tm,),
        in_specs=[
            pl.BlockSpec((tm, d), lambda i: (i, 0)),
            pl.BlockSpec((1, d), lambda i: (0, 0)),
            pl.BlockSpec((d, cols), lambda i: (0, 0), pipeline_mode=pl.Buffered(1)),
        ],
        out_specs=[
            pl.BlockSpec((tm, lru_cols), lambda i: (i, 0)),
            pl.BlockSpec((tm, cols - lru_cols), lambda i: (i, 0)),
        ],
        out_shape=[
            jax.ShapeDtypeStruct((n, lru_cols), F32),
            jax.ShapeDtypeStruct((n, cols - lru_cols), BF16),
        ],
        compiler_params=pltpu.CompilerParams(
            dimension_semantics=("arbitrary",), vmem_limit_bytes=VMEM_LIMIT_BYTES),
        name="inproj",
    )(x2, gain, w_in_bf16)


def _lru_kernel(xg_ref, cw_ref, cb_ref, wrg_ref, brg_ref, wig_ref, big_ref, lam_ref, gn_ref,
                out_ref, xs_ref, a_ref, b_ref, hc_ref, *, ts, width):
    si = pl.program_id(1)
    halo = SUBLANES

    @pl.when(si == 0)
    def _():
        xs_ref[0:halo, :] = jnp.zeros((halo, width), F32)
        hc_ref[...] = jnp.zeros((SUBLANES, width), F32)

    xs_ref[halo:halo + ts, :] = xg_ref[0, :, 0:width]

    xc = jnp.broadcast_to(cb_ref[...], (ts, width))
    for i in range(CONV_W):
        off = halo - (CONV_W - 1) + i
        xc = xc + xs_ref[off:off + ts, :] * cw_ref[i:i + 1, :]
    xs_ref[0:halo, :] = xs_ref[ts:ts + halo, :]

    xcb = xc.astype(BF16)
    rg, ig = [], []
    for c in range(width // MXU_DIM):
        sl = slice(c * MXU_DIM, (c + 1) * MXU_DIM)
        rg.append(jnp.dot(xcb[:, sl], wrg_ref[c], preferred_element_type=F32))
        ig.append(jnp.dot(xcb[:, sl], wig_ref[c], preferred_element_type=F32))
    r = _sigmoid(jnp.concatenate(rg, axis=1) + brg_ref[...])
    ig = _sigmoid(jnp.concatenate(ig, axis=1) + big_ref[...])

    log_a = (-LRU_C * r) * _softplus(-lam_ref[...])
    a = jnp.exp(log_a)
    th = jnp.tanh(log_a)
    mult = jnp.sqrt((-2.0 * th) / (1.0 - th))
    a_ref[...] = a
    b_ref[...] = mult * (ig * xc)

    row = lax.broadcasted_iota(jnp.int32, (SUBLANES, width), 0)

    def group(g, hprev):
        base = pl.multiple_of(g * SUBLANES, SUBLANES)
        a8 = a_ref[pl.ds(base, SUBLANES), :]
        b8 = b_ref[pl.ds(base, SUBLANES), :]
        for d in (1, 2, 4):
            keep = row >= d
            b8 = jnp.where(keep, b8 + a8 * pltpu.roll(b8, d, axis=0), b8)
            a8 = jnp.where(keep, a8 * pltpu.roll(a8, d, axis=0), a8)
        h8 = b8 + a8 * hprev
        b_ref[pl.ds(base, SUBLANES), :] = h8
        return jnp.broadcast_to(h8[SUBLANES - 1:SUBLANES, :], (SUBLANES, width))

    hc_ref[...] = lax.fori_loop(0, ts // SUBLANES, group, hc_ref[...], unroll=4)

    y = _gelu_tanh(xg_ref[0, :, width:2 * width]) * b_ref[...]
    out_ref[0] = ((y * _rms_scale(y)) * gn_ref[...]).astype(out_ref.dtype)


def _lru(u_lru, conv_w, conv_b, wrg_bd, b_rg, wig_bd, b_ig, lam, gain, ts):
    bsz, s, two_w = u_lru.shape
    width = two_w // 2
    vec = pl.BlockSpec((1, width), lambda b, i: (0, 0))
    wspec = pl.BlockSpec(wrg_bd.shape, lambda b, i: (0, 0, 0))
    return pl.pallas_call(
        functools.partial(_lru_kernel, ts=ts, width=width),
        grid=(bsz, s // ts),
        in_specs=[
            pl.BlockSpec((1, ts, two_w), lambda b, i: (b, i, 0)),
            pl.BlockSpec((CONV_W, width), lambda b, i: (0, 0)),
            vec, wspec, vec, wspec, vec, vec, vec,
        ],
        out_specs=pl.BlockSpec((1, ts, width), lambda b, i: (b, i, 0)),
        out_shape=jax.ShapeDtypeStruct((bsz, s, width), BF16),
        scratch_shapes=[
            pltpu.VMEM((ts + SUBLANES, width), F32),
            pltpu.VMEM((ts, width), F32),
            pltpu.VMEM((ts, width), F32),
            pltpu.VMEM((SUBLANES, width), F32),
        ],
        compiler_params=pltpu.CompilerParams(
            dimension_semantics=("arbitrary", "arbitrary"), vmem_limit_bytes=VMEM_LIMIT_BYTES),
        name="rglru",
    )(u_lru, conv_w, conv_b, wrg_bd, b_rg, wig_bd, b_ig, lam, gain)


def _block_diag(w, per):
    nb, bw, _ = w.shape
    w = w.reshape(nb // per, per, bw, bw)
    eye = jnp.eye(per, dtype=w.dtype)
    return jnp.einsum('gpij,pq->gpiqj', w, eye).reshape(nb // per, per * bw, per * bw)


def _attn_kernel(q_ref, k_ref, v_ref, o_ref, r_ref, acc_ref, *, tq, tk, nleft, scale2):
    qi = pl.program_id(2)
    nrb = tq // tk
    g0 = qi * nrb

    q = (q_ref[0].astype(F32) * scale2).astype(BF16)
    lane_q = lax.broadcasted_iota(jnp.int32, (tq, LANES), 1)
    q_heads = (jnp.where(lane_q < HEAD_DIM, q, jnp.zeros_like(q)),
               jnp.where(lane_q >= HEAD_DIM, q, jnp.zeros_like(q)))

    rr = lax.broadcasted_iota(jnp.int32, (2 * tk, 2 * tk), 0)
    cc = lax.broadcasted_iota(jnp.int32, (2 * tk, 2 * tk), 1)
    key_j = jnp.where(rr >= tk, rr - tk, rr)
    tri = jnp.where((cc >= tk) | (key_j > cc), 1.0, 0.0).astype(BF16)
    below_diag = (lax.broadcasted_iota(jnp.int32, (tk, tk), 1)
                  < lax.broadcasted_iota(jnp.int32, (tk, tk), 0))

    def softplus_parts(z):
        lb = jnp.minimum(z, 0.0) - jnp.log(1.0 + jnp.exp2(-jnp.abs(z))) * LOG2_E
        return lb, lb - z

    def suffix_and_total(l):
        hi = l.astype(BF16)
        lo = (l - hi.astype(F32)).astype(BF16)
        cs = jnp.dot(jnp.concatenate([hi, lo], axis=1), tri, preferred_element_type=F32)
        return cs[:, :tk], cs[:, tk:]

    def scores(h, rows, k_blk):
        return lax.dot_general(q_heads[h][rows, :], k_blk, (((1,), (1,)), ((), ())),
                               preferred_element_type=F32)

    def windows(jobs):
        chains = []
        for rb, first_kb, nkb in jobs:
            rows = slice(rb * tk, (rb + 1) * tk)
            ks = first_kb * tk
            if not isinstance(ks, int):
                ks = pl.multiple_of(ks, tk)
            k_win = k_ref[0, pl.ds(ks, nkb * tk), :]
            v_win = v_ref[0, pl.ds(ks, nkb * tk), :]
            for h in range(2):
                chains.append(dict(h=h, rows=rows, nkb=nkb, v=v_win, z=scores(h, rows, k_win)))
        for c in chains:
            c["lb"], c["parts"] = [], []
            for kb in range(c["nkb"]):
                zb = c["z"][:, kb * tk:(kb + 1) * tk]
                if kb == c["nkb"] - 1:
                    zb = jnp.where(below_diag, zb, -MASKED_SCORE)
                lb, l = softplus_parts(zb)
                c["lb"].append(lb)
                c["parts"].append(suffix_and_total(l))
        for c in chains:
            r = jnp.zeros((tk, tk), F32)
            atts = [None] * c["nkb"]
            for kb in reversed(range(c["nkb"])):
                suffix, total = c["parts"][kb]
                atts[kb] = jnp.exp2(c["lb"][kb] + suffix + r).astype(BF16)
                r = r + total
            c["att"] = atts[0] if c["nkb"] == 1 else jnp.concatenate(atts, axis=1)
            r_ref[c["h"], c["rows"], :] = r
        for c in chains:
            acc_ref[c["h"], c["rows"], :] = jnp.dot(c["att"], c["v"], preferred_element_type=F32)

    @pl.when(qi == 0)
    def _():
        windows([(rb, max(rb - nleft, 0), rb - max(rb - nleft, 0) + 1) for rb in range(nrb)])

    @pl.when(qi > 0)
    def _():
        windows([(rb, g0 + (rb - nleft), nleft + 1) for rb in range(nrb)])

    def cond(carry):
        j, alive = carry
        return jnp.logical_and(j <= g0 + (nrb - 1 - nleft - 1), alive)

    def body(carry):
        j, _ = carry
        for rb in range(nrb):
            jb = g0 + (rb - nleft - 1) - j

            @pl.when(jb >= 0)
            def _():
                rows = slice(rb * tk, (rb + 1) * tk)
                ks = pl.multiple_of(jb * tk, tk)
                k_blk = k_ref[0, pl.ds(ks, tk), :]
                v_blk = v_ref[0, pl.ds(ks, tk), :]
                for h in range(2):
                    lb, l = softplus_parts(scores(h, rows, k_blk))
                    suffix, total = suffix_and_total(l)
                    r_old = r_ref[h, rows, :]
                    att = jnp.exp2(lb + suffix + r_old).astype(BF16)
                    acc_ref[h, rows, :] += jnp.dot(att, v_blk, preferred_element_type=F32)
                    r_ref[h, rows, :] = r_old + total
        return j + 1, jnp.max(r_ref[...]) >= ATT_DEAD_LOG2

    lax.while_loop(cond, body, (0, jnp.max(r_ref[...]) >= ATT_DEAD_LOG2))

    o_ref[0] = jnp.where(lane_q < HEAD_DIM, acc_ref[0], acc_ref[1])


def _attention(qkv, tq, tk, nleft):
    bsz, s, three_w = qkv.shape
    width = three_w // 3
    ngrp = width // LANES
    return pl.pallas_call(
        functools.partial(_attn_kernel, tq=tq, tk=tk, nleft=nleft,
                          scale2=LOG2_E / math.sqrt(HEAD_DIM)),
        grid=(bsz, ngrp, s // tq),
        in_specs=[
            pl.BlockSpec((1, tq, LANES), lambda b, g, i: (b, i, g)),
            pl.BlockSpec((1, s, LANES), lambda b, g, i: (b, 0, ngrp + g)),
            pl.BlockSpec((1, s, LANES), lambda b, g, i: (b, 0, 2 * ngrp + g)),
        ],
        out_specs=pl.BlockSpec((1, tq, LANES), lambda b, g, i: (b, i, g)),
        out_shape=jax.ShapeDtypeStruct((bsz, s, width), F32),
        scratch_shapes=[
            pltpu.VMEM((2, tq, tk), F32),
            pltpu.VMEM((2, tq, LANES), F32),
        ],
        compiler_params=pltpu.CompilerParams(
            dimension_semantics=("arbitrary", "arbitrary", "arbitrary"),
            vmem_limit_bytes=VMEM_LIMIT_BYTES),
        name="sb_attention",
    )(qkv, qkv, qkv)


def _mix_ffn_kernel(x_ref, yl_ref, ya_ref, ga_ref, wo_ref, gf_ref, wi_ref, wd_ref, gl_ref,
                    out_ref, *, d_ff, ff_chunk, final_norm):
    ya = ya_ref[...]
    ya = ((ya * _rms_scale(ya)) * ga_ref[...]).astype(BF16)
    wl = yl_ref.shape[1]
    x1 = x_ref[...] + jnp.dot(yl_ref[...], wo_ref[0:wl, :], preferred_element_type=F32)
    x1 = x1 + jnp.dot(ya, wo_ref[wl:, :], preferred_element_type=F32)

    h2 = ((x1 * _rms_scale(x1)) * gf_ref[...]).astype(BF16)
    ffn = jnp.zeros_like(x1)
    for c in range(d_ff // ff_chunk):
        lo = c * ff_chunk
        gate = jnp.dot(h2, wi_ref[:, lo:lo + ff_chunk], preferred_element_type=F32)
        up = jnp.dot(h2, wi_ref[:, d_ff + lo:d_ff + lo + ff_chunk], preferred_element_type=F32)
        act = (gate * _sigmoid(gate)) * up
        ffn = ffn + jnp.dot(act.astype(BF16), wd_ref[lo:lo + ff_chunk, :],
                            preferred_element_type=F32)
    out = x1 + ffn
    if final_norm:
        out = (out * _rms_scale(out)) * gl_ref[...]
    out_ref[...] = out


def _mix_ffn(x2, y_lru, y_att, g_att, w_out, g_ffn, w_ffn_in, w_ffn_out, g_last, tm, ff_chunk,
             final_norm):
    n, d = x2.shape
    d_ff = w_ffn_out.shape[0]
    row = lambda w: pl.BlockSpec((tm, w), lambda i: (i, 0))
    vec = lambda w: pl.BlockSpec((1, w), lambda i: (0, 0))
    whole = lambda a: pl.BlockSpec(a.shape, lambda i: (0, 0), pipeline_mode=pl.Buffered(1))
    return pl.pallas_call(
        functools.partial(_mix_ffn_kernel, d_ff=d_ff, ff_chunk=ff_chunk, final_norm=final_norm),
        grid=(n // tm,),
        in_specs=[row(d), row(y_lru.shape[1]), row(y_att.shape[1]), vec(y_att.shape[1]),
                  whole(w_out), vec(d), whole(w_ffn_in), whole(w_ffn_out), vec(d)],
        out_specs=row(d),
        out_shape=jax.ShapeDtypeStruct((n, d), F32),
        compiler_params=pltpu.CompilerParams(
            dimension_semantics=("arbitrary",), vmem_limit_bytes=VMEM_LIMIT_BYTES),
        name="mix_ffn",
    )(x2, y_lru, y_att, g_att, w_out, g_ffn, w_ffn_in, w_ffn_out, g_last)


def _largest_chunk(total, limit):
    best = LANES
    for c in range(LANES, limit + 1, LANES):
        if total % c == 0:
            best = c
    return best


def kernel(x, norm_mix, w_in, conv_w, conv_b, w_rg, b_rg, w_ig, b_ig, lru_lambda, norm_lru_out,
           norm_att_out, w_out, norm_ffn, w_ffn_in, w_ffn_out, norm_final):
    bsz, s, d = x.shape
    depth = w_in.shape[0]
    lru_w = conv_w.shape[2]
    att_w = norm_att_out.shape[1]
    nblk, blk_w = w_rg.shape[1], w_rg.shape[2]
    assert w_in.shape[2] == 2 * lru_w + 3 * att_w and att_w % LANES == 0
    assert MXU_DIM % blk_w == 0 and lru_w % MXU_DIM == 0

    n = bsz * s
    tm = 512
    ts = 1024
    tq, tk, nleft = 512, 128, 2
    per = MXU_DIM // blk_w
    row1 = lambda v: v.reshape(1, -1)

    x2 = x.reshape(n, d)
    for l in range(depth):
        u_lru, qkv = _inproj(x2, row1(norm_mix[l]), w_in[l].astype(BF16), 2 * lru_w, tm)
        y_lru = _lru(u_lru.reshape(bsz, s, 2 * lru_w), conv_w[l], row1(conv_b[l]),
                     _block_diag(w_rg[l], per).astype(BF16), row1(b_rg[l]),
                     _block_diag(w_ig[l], per).astype(BF16), row1(b_ig[l]),
                     row1(lru_lambda[l]), row1(norm_lru_out[l]), ts)
        y_att = _attention(qkv.reshape(bsz, s, 3 * att_w), tq, tk, nleft)
        d_ff = w_ffn_out.shape[1]
        x2 = _mix_ffn(x2, y_lru.reshape(n, lru_w), y_att.reshape(n, att_w), row1(norm_att_out[l]),
                      w_out[l].astype(BF16), row1(norm_ffn[l]), w_ffn_in[l].astype(BF16),
                      w_ffn_out[l].astype(BF16), row1(norm_final),
                      tm, _largest_chunk(d_ff, 512), l == depth - 1)
    return x2.reshape(bsz, s, d)
```

```python
import functools
import math

import jax
import jax.numpy as jnp
from jax import lax
from jax.experimental import pallas as pl
from jax.experimental.pallas import tpu as pltpu

F32 = jnp.float32
BF16 = jnp.bfloat16

EPS = 1e-6
LRU_C = 8.0
CONV_W = 4
HEAD_DIM = 64

LANES = 128
SUBLANES = 8
MXU_DIM = 256
VMEM_LIMIT_BYTES = 56 * 1024 * 1024

LOG2_E = 1.4426950408889634
ATT_DEAD_LOG2 = -160.0
MASKED_SCORE = 1e30


def _rms_scale(x):
    return lax.rsqrt(jnp.mean(x * x, axis=-1, keepdims=True) + EPS)


def _sigmoid(x):
    return 1.0 / (1.0 + jnp.exp(-x))


def _softplus(x):
    return jnp.maximum(x, 0.0) + jnp.log(1.0 + jnp.exp(-jnp.abs(x)))


def _gelu_tanh(x):
    c = math.sqrt(2.0 / math.pi)
    return 0.5 * x * (1.0 + jnp.tanh(c * (x + 0.044715 * (x * x * x))))


def _inproj_kernel(x_ref, g_ref, w_ref, lru_ref, qkv_ref, *, lru_cols):
    x = x_ref[...]
    h = (x * _rms_scale(x)) * g_ref[...]
    u = jnp.dot(h.astype(BF16), w_ref[...], preferred_element_type=F32)
    lru_ref[...] = u[:, :lru_cols]
    qkv_ref[...] = u[:, lru_cols:].astype(BF16)


def _inproj(x2, gain, w_in_bf16, lru_cols, tm):
    n, d = x2.shape
    cols = w_in_bf16.shape[1]
    return pl.pallas_call(
        functools.partial(_inproj_kernel, lru_cols=lru_cols),
        grid=(n // tm,),
        in_specs=[
            pl.BlockSpec((tm, d), lambda i: (i, 0)),
            pl.BlockSpec((1, d), lambda i: (0, 0)),
            pl.BlockSpec((d, cols), lambda i: (0, 0), pipeline_mode=pl.Buffered(1)),
        ],
        out_specs=[
            pl.BlockSpec((tm, lru_cols), lambda i: (i, 0)),
            pl.BlockSpec((tm, cols - lru_cols), lambda i: (i, 0)),
        ],
        out_shape=[
            jax.ShapeDtypeStruct((n, lru_cols), F32),
            jax.ShapeDtypeStruct((n, cols - lru_cols), BF16),
        ],
        compiler_params=pltpu.CompilerParams(
            dimension_semantics=("arbitrary",), vmem_limit_bytes=VMEM_LIMIT_BYTES),
        name="inproj",
    )(x2, gain, w_in_bf16)


def _lru_kernel(xg_ref, cw_ref, cb_ref, wrg_ref, brg_ref, wig_ref, big_ref, lam_ref, gn_ref,
                out_ref, xs_ref, a_ref, b_ref, hc_ref, *, ts, width):
    si = pl.program_id(1)
    halo = SUBLANES

    @pl.when(si == 0)
    def _():
        xs_ref[0:halo, :] = jnp.zeros((halo, width), F32)
        hc_ref[...] = jnp.zeros((SUBLANES, width), F32)

    xs_ref[halo:halo + ts, :] = xg_ref[0, :, 0:width]

    xc = jnp.broadcast_to(cb_ref[...], (ts, width))
    for i in range(CONV_W):
        off = halo - (CONV_W - 1) + i
        xc = xc + xs_ref[off:off + ts, :] * cw_ref[i:i + 1, :]
    xs_ref[0:halo, :] = xs_ref[ts:ts + halo, :]

    xcb = xc.astype(BF16)
    rg, ig = [], []
    for c in range(width // MXU_DIM):
        sl = slice(c * MXU_DIM, (c + 1) * MXU_DIM)
        rg.append(jnp.dot(xcb[:, sl], wrg_ref[c], preferred_element_type=F32))
        ig.append(jnp.dot(xcb[:, sl], wig_ref[c], preferred_element_type=F32))
    r = _sigmoid(jnp.concatenate(rg, axis=1) + brg_ref[...])
    ig = _sigmoid(jnp.concatenate(ig, axis=1) + big_ref[...])

    log_a = (-LRU_C * r) * _softplus(-lam_ref[...])
    a = jnp.exp(log_a)
    th = jnp.tanh(log_a)
    mult = jnp.sqrt((-2.0 * th) / (1.0 - th))
    a_ref[...] = a
    b_ref[...] = mult * (ig * xc)

    row = lax.broadcasted_iota(jnp.int32, (SUBLANES, width), 0)

    def group(g, hprev):
        base = pl.multiple_of(g * SUBLANES, SUBLANES)
        a8 = a_ref[pl.ds(base, SUBLANES), :]
        b8 = b_ref[pl.ds(base, SUBLANES), :]
        for d in (1, 2, 4):
            keep = row >= d
            b8 = jnp.where(keep, b8 + a8 * pltpu.roll(b8, d, axis=0), b8)
            a8 = jnp.where(keep, a8 * pltpu.roll(a8, d, axis=0), a8)
        h8 = b8 + a8 * hprev
        b_ref[pl.ds(base, SUBLANES), :] = h8
        return jnp.broadcast_to(h8[SUBLANES - 1:SUBLANES, :], (SUBLANES, width))

    hc_ref[...] = lax.fori_loop(0, ts // SUBLANES, group, hc_ref[...], unroll=4)

    y = _gelu_tanh(xg_ref[0, :, width:2 * width]) * b_ref[...]
    out_ref[0] = ((y * _rms_scale(y)) * gn_ref[...]).astype(out_ref.dtype)


def _lru(u_lru, conv_w, conv_b, wrg_bd, b_rg, wig_bd, b_ig, lam, gain, ts):
    bsz, s, two_w = u_lru.shape
    width = two_w // 2
    vec = pl.BlockSpec((1, width), lambda b, i: (0, 0))
    wspec = pl.BlockSpec(wrg_bd.shape, lambda b, i: (0, 0, 0))
    return pl.pallas_call(
        functools.partial(_lru_kernel, ts=ts, width=width),
        grid=(bsz, s // ts),
        in_specs=[
            pl.BlockSpec((1, ts, two_w), lambda b, i: (b, i, 0)),
            pl.BlockSpec((CONV_W, width), lambda b, i: (0, 0)),
            vec, wspec, vec, wspec, vec, vec, vec,
        ],
        out_specs=pl.BlockSpec((1, ts, width), lambda b, i: (b, i, 0)),
        out_shape=jax.ShapeDtypeStruct((bsz, s, width), BF16),
        scratch_shapes=[
            pltpu.VMEM((ts + SUBLANES, width), F32),
            pltpu.VMEM((ts, width), F32),
            pltpu.VMEM((ts, width), F32),
            pltpu.VMEM((SUBLANES, width), F32),
        ],
        compiler_params=pltpu.CompilerParams(
            dimension_semantics=("arbitrary", "arbitrary"), vmem_limit_bytes=VMEM_LIMIT_BYTES),
        name="rglru",
    )(u_lru, conv_w, conv_b, wrg_bd, b_rg, wig_bd, b_ig, lam, gain)


def _block_diag(w, per):
    nb, bw, _ = w.shape
    w = w.reshape(nb // per, per, bw, bw)
    eye = jnp.eye(per, dtype=w.dtype)
    return jnp.einsum('gpij,pq->gpiqj', w, eye).reshape(nb // per, per * bw, per * bw)


def _ffn_stages(x_ref, yl_ref, ya, ga_ref, wo_ref, gf_ref, wi_ref, wd_ref, gl_ref, out_ref, *,
                d_ff, ff_chunk, final_norm):
    ya = ((ya * _rms_scale(ya)) * ga_ref[...]).astype(BF16)
    wl = yl_ref.shape[1]
    x1 = x_ref[...] + jnp.dot(yl_ref[...], wo_ref[0:wl, :], preferred_element_type=F32)
    x1 = x1 + jnp.dot(ya, wo_ref[wl:, :], preferred_element_type=F32)
    h2 = ((x1 * _rms_scale(x1)) * gf_ref[...]).astype(BF16)
    yield
    ffn = jnp.zeros_like(x1)
    for c in range(d_ff // ff_chunk):
        lo = c * ff_chunk
        gate = jnp.dot(h2, wi_ref[:, lo:lo + ff_chunk], preferred_element_type=F32)
        up = jnp.dot(h2, wi_ref[:, d_ff + lo:d_ff + lo + ff_chunk], preferred_element_type=F32)
        act = (gate * _sigmoid(gate)) * up
        ffn = ffn + jnp.dot(act.astype(BF16), wd_ref[lo:lo + ff_chunk, :],
                            preferred_element_type=F32)
        yield
    out = x1 + ffn
    if final_norm:
        out = (out * _rms_scale(out)) * gl_ref[...]
    out_ref[...] = out


def _attn_ffn_kernel(q_ref, kc_ref, kp_ref, vc_ref, vp_ref, qkv_hbm, x_ref, yl_ref, ga_ref,
                     wo_ref, gf_ref, wi_ref, wd_ref, gl_ref, out_ref,
                     yatt_ref, r_ref, acc_ref, kblk_ref, vblk_ref, sem,
                     *, tq, tk, nleft, scale2, tiles_per_seq, n_tiles, att_w, d_ff, ff_chunk,
                     final_norm):
    j = pl.program_id(0)
    ja = jnp.minimum(j, n_tiles - 1)
    bi = ja // tiles_per_seq
    ti = ja % tiles_per_seq
    nrb = tq // tk
    g0 = ti * nrb
    npair = att_w // LANES
    slot = j % 2

    @pl.when(j == 0)
    def _():
        yatt_ref[...] = jnp.zeros(yatt_ref.shape, F32)

    mixer = _ffn_stages(x_ref, yl_ref, yatt_ref[1 - slot], ga_ref, wo_ref, gf_ref, wi_ref, wd_ref,
                        gl_ref, out_ref, d_ff=d_ff, ff_chunk=ff_chunk, final_norm=final_norm)

    lane_q = lax.broadcasted_iota(jnp.int32, (tq, LANES), 1)
    rr = lax.broadcasted_iota(jnp.int32, (2 * tk, 2 * tk), 0)
    cc = lax.broadcasted_iota(jnp.int32, (2 * tk, 2 * tk), 1)
    key_j = jnp.where(rr >= tk, rr - tk, rr)
    tri = jnp.where((cc >= tk) | (key_j > cc), 1.0, 0.0).astype(BF16)
    below_diag = (lax.broadcasted_iota(jnp.int32, (tk, tk), 1)
                  < lax.broadcasted_iota(jnp.int32, (tk, tk), 0))
    has_left = ti > 0

    def softplus_parts(z):
        lb = jnp.minimum(z, 0.0) - jnp.log(1.0 + jnp.exp2(-jnp.abs(z))) * LOG2_E
        return lb, lb - z

    def suffix_and_total(l):
        hi = l.astype(BF16)
        lo = (l - hi.astype(F32)).astype(BF16)
        cs = jnp.dot(jnp.concatenate([hi, lo], axis=1), tri, preferred_element_type=F32)
        return cs[:, :tk], cs[:, tk:]

    def head_queries(g):
        q = (q_ref[0, :, g * LANES:(g + 1) * LANES].astype(F32) * scale2).astype(BF16)
        return (jnp.where(lane_q < HEAD_DIM, q, jnp.zeros_like(q)),
                jnp.where(lane_q >= HEAD_DIM, q, jnp.zeros_like(q)))

    def scores(qh, k_blk):
        return lax.dot_general(qh, k_blk, (((1,), (1,)), ((), ())), preferred_element_type=F32)

    def attention_stages():
        for g in range(npair):
            cols = slice(g * LANES, (g + 1) * LANES)
            q_heads = head_queries(g)
            k_all = jnp.concatenate([kp_ref[0, :, cols], kc_ref[0, :, cols]], axis=0)
            v_all = jnp.concatenate([vp_ref[0, :, cols], vc_ref[0, :, cols]], axis=0)
            chains = []
            for rb in range(nrb):
                rows = slice(rb * tk, (rb + 1) * tk)
                win = slice(rb * tk, (rb + nleft + 1) * tk)
                for h in range(2):
                    chains.append(dict(h=2 * g + h, rb=rb, rows=rows, v=v_all[win, :],
                                       z=scores(q_heads[h][rows, :], k_all[win, :])))
            yield
            for c in chains:
                c["lb"], c["parts"] = [], []
                for kb in range(nleft + 1):
                    zb = c["z"][:, kb * tk:(kb + 1) * tk]
                    if kb == nleft:
                        zb = jnp.where(below_diag, zb, -MASKED_SCORE)
                    elif c["rb"] + kb < nleft:
                        zb = jnp.where(has_left, zb, -MASKED_SCORE)
                    lb, l = softplus_parts(zb)
                    c["lb"].append(lb)
                    c["parts"].append(suffix_and_total(l))
            yield
            for c in chains:
                r = jnp.zeros((tk, tk), F32)
                atts = [None] * (nleft + 1)
                for kb in reversed(range(nleft + 1)):
                    suffix, total = c["parts"][kb]
                    atts[kb] = jnp.exp2(c["lb"][kb] + suffix + r).astype(BF16)
                    r = r + total
                c["att"] = jnp.concatenate(atts, axis=1)
                r_ref[c["h"], c["rows"], :] = r
            yield
            for c in chains:
                acc_ref[c["h"], c["rows"], :] = jnp.dot(c["att"], c["v"],
                                                        preferred_element_type=F32)
            yield

    streams = [attention_stages(), mixer]
    while streams:
        for st in list(streams):
            if next(st, StopIteration) is StopIteration:
                streams.remove(st)

    def cond(carry):
        s, alive = carry
        return jnp.logical_and(s <= g0 + (nrb - 1 - nleft - 1), alive)

    def body(carry):
        s, _ = carry
        for g in range(npair):
            q_heads = head_queries(g)
            for rb in range(nrb):
                jb = g0 + (rb - nleft - 1) - s

                @pl.when(jb >= 0)
                def _():
                    rows = slice(rb * tk, (rb + 1) * tk)
                    ks = pl.multiple_of(jb * tk, tk)
                    copies = [
                        pltpu.make_async_copy(
                            qkv_hbm.at[bi, pl.ds(ks, tk), pl.ds((1 + p) * att_w + g * LANES, LANES)],
                            dst, sem.at[p])
                        for p, dst in enumerate((kblk_ref, vblk_ref))]
                    for cp in copies:
                        cp.start()
                    for cp in copies:
                        cp.wait()
                    for h in range(2):
                        lb, l = softplus_parts(scores(q_heads[h][rows, :], kblk_ref[...]))
                        suffix, total = suffix_and_total(l)
                        r_old = r_ref[2 * g + h, rows, :]
                        att = jnp.exp2(lb + suffix + r_old).astype(BF16)
                        acc_ref[2 * g + h, rows, :] += jnp.dot(att, vblk_ref[...],
                                                               preferred_element_type=F32)
                        r_ref[2 * g + h, rows, :] = r_old + total
        return s + 1, jnp.max(r_ref[...]) >= ATT_DEAD_LOG2

    lax.while_loop(cond, body, (0, jnp.max(r_ref[...]) >= ATT_DEAD_LOG2))

    for g in range(npair):
        yatt_ref[slot, :, g * LANES:(g + 1) * LANES] = jnp.where(
            lane_q < HEAD_DIM, acc_ref[2 * g], acc_ref[2 * g + 1])


def _attn_ffn(qkv, x2, y_lru, g_att, w_out, g_ffn, w_ffn_in, w_ffn_out, g_last, tq, tk, nleft,
              ff_chunk, final_norm):
    bsz, s, three_w = qkv.shape
    att_w = three_w // 3
    n, d = x2.shape
    d_ff = w_ffn_out.shape[0]
    tiles_per_seq = s // tq
    n_tiles = bsz * tiles_per_seq
    left = nleft * tk
    assert tq % tk == 0 and tq // tk >= nleft and tq % left == 0 and att_w % LANES == 0

    def tile(j):
        ja = jnp.minimum(j, n_tiles - 1)
        return ja // tiles_per_seq, ja % tiles_per_seq

    def cur(part):
        return pl.BlockSpec((1, tq, att_w), lambda j: (*tile(j), part))

    def prev(part):
        def index(j):
            b, t = tile(j)
            return b, jnp.maximum(t * (tq // left) - 1, 0), part
        return pl.BlockSpec((1, left, att_w), index)

    row = lambda w: pl.BlockSpec((tq, w), lambda j: (jnp.maximum(j - 1, 0), 0))
    vec = lambda w: pl.BlockSpec((1, w), lambda j: (0, 0))
    whole = lambda a: pl.BlockSpec(a.shape, lambda j: (0, 0), pipeline_mode=pl.Buffered(1))
    nheads = att_w // HEAD_DIM
    return pl.pallas_call(
        functools.partial(_attn_ffn_kernel, tq=tq, tk=tk, nleft=nleft,
                          scale2=LOG2_E / math.sqrt(HEAD_DIM), tiles_per_seq=tiles_per_seq,
                          n_tiles=n_tiles, att_w=att_w, d_ff=d_ff, ff_chunk=ff_chunk,
                          final_norm=final_norm),
        grid=(n_tiles + 1,),
        in_specs=[cur(0), cur(1), prev(1), cur(2), prev(2),
                  pl.BlockSpec(memory_space=pl.ANY),
                  row(d), row(y_lru.shape[1]), vec(att_w), whole(w_out), vec(d),
                  whole(w_ffn_in), whole(w_ffn_out), vec(d)],
        out_specs=row(d),
        out_shape=jax.ShapeDtypeStruct((n, d), F32),
        scratch_shapes=[
            pltpu.VMEM((2, tq, att_w), F32),
            pltpu.VMEM((nheads, tq, tk), F32),
            pltpu.VMEM((nheads, tq, LANES), F32),
            pltpu.VMEM((tk, LANES), BF16),
            pltpu.VMEM((tk, LANES), BF16),
            pltpu.SemaphoreType.DMA((2,)),
        ],
        compiler_params=pltpu.CompilerParams(
            dimension_semantics=("arbitrary",), vmem_limit_bytes=VMEM_LIMIT_BYTES),
        name="attn_ffn",
    )(qkv, qkv, qkv, qkv, qkv, qkv, x2, y_lru, g_att, w_out, g_ffn, w_ffn_in, w_ffn_out, g_last)


def _largest_chunk(total, limit):
    best = LANES
    for c in range(LANES, limit + 1, LANES):
        if total % c == 0:
            best = c
    return best


def kernel(x, norm_mix, w_in, conv_w, conv_b, w_rg, b_rg, w_ig, b_ig, lru_lambda, norm_lru_out,
           norm_att_out, w_out, norm_ffn, w_ffn_in, w_ffn_out, norm_final):
    bsz, s, d = x.shape
    depth = w_in.shape[0]
    lru_w = conv_w.shape[2]
    att_w = norm_att_out.shape[1]
    nblk, blk_w = w_rg.shape[1], w_rg.shape[2]
    assert w_in.shape[2] == 2 * lru_w + 3 * att_w and att_w % LANES == 0
    assert MXU_DIM % blk_w == 0 and lru_w % MXU_DIM == 0

    n = bsz * s
    tm = 512
    ts = 1024
    tq, tk, nleft = 512, 128, 2
    per = MXU_DIM // blk_w
    row1 = lambda v: v.reshape(1, -1)

    x2 = x.reshape(n, d)
    for l in range(depth):
        u_lru, qkv = _inproj(x2, row1(norm_mix[l]), w_in[l].astype(BF16), 2 * lru_w, tm)
        y_lru = _lru(u_lru.reshape(bsz, s, 2 * lru_w), conv_w[l], row1(conv_b[l]),
                     _block_diag(w_rg[l], per).astype(BF16), row1(b_rg[l]),
                     _block_diag(w_ig[l], per).astype(BF16), row1(b_ig[l]),
                     row1(lru_lambda[l]), row1(norm_lru_out[l]), ts)
        d_ff = w_ffn_out.shape[1]
        x2 = _attn_ffn(qkv.reshape(bsz, s, 3 * att_w), x2, y_lru.reshape(n, lru_w),
                       row1(norm_att_out[l]), w_out[l].astype(BF16), row1(norm_ffn[l]),
                       w_ffn_in[l].astype(BF16), w_ffn_out[l].astype(BF16), row1(norm_final),
                       tq, tk, nleft, _largest_chunk(d_ff, 512), l == depth - 1)
    return x2.reshape(bsz, s, d)
```

```python
import functools
import math

import jax
import jax.numpy as jnp
from jax import lax
from jax.experimental import pallas as pl
from jax.experimental.pallas import tpu as pltpu

F32 = jnp.float32
BF16 = jnp.bfloat16

EPS = 1e-6
LRU_C = 8.0
CONV_W = 4
HEAD_DIM = 64

LANES = 128
SUBLANES = 8
MXU_DIM = 256
VMEM_LIMIT_BYTES = 56 * 1024 * 1024

LOG2_E = 1.4426950408889634
ATT_DEAD_LOG2 = -160.0
MASKED_SCORE = 1e30


def _rms_scale(x):
    return lax.rsqrt(jnp.mean(x * x, axis=-1, keepdims=True) + EPS)


def _sigmoid(x):
    return 1.0 / (1.0 + jnp.exp(-x))


def _softplus(x):
    return jnp.maximum(x, 0.0) + jnp.log(1.0 + jnp.exp(-jnp.abs(x)))


def _gelu_tanh(x):
    c = math.sqrt(2.0 / math.pi)
    return 0.5 * x * (1.0 + jnp.tanh(c * (x + 0.044715 * (x * x * x))))


def _inproj_stages(x_ref, g_ref, w_ref, u_slot_ref, qkv_ref, *, lru_cols, col_chunk):
    x = x_ref[...]
    h = ((x * _rms_scale(x)) * g_ref[...]).astype(BF16)
    yield
    for lo in range(0, w_ref.shape[1], col_chunk):
        u = jnp.dot(h, w_ref[:, lo:lo + col_chunk], preferred_element_type=F32)
        if lo < lru_cols:
            u_slot_ref[:, lo:lo + col_chunk] = u
        else:
            qkv_ref[:, lo - lru_cols:lo - lru_cols + col_chunk] = u.astype(BF16)
        yield


def _lru_stages(first, u_ref, cw_ref, cb_ref, wrg_ref, brg_ref, wig_ref, big_ref, lam_ref, gn_ref,
                out_ref, xs_ref, a_ref, b_ref, hc_ref, *, ts, width, scan_parts):
    halo = SUBLANES
    xs_ref[0:halo, :] = jnp.where(first, 0.0, xs_ref[ts:ts + halo, :])
    xs_ref[halo:halo + ts, :] = u_ref[:, 0:width]

    xc = jnp.broadcast_to(cb_ref[...], (ts, width))
    for i in range(CONV_W):
        off = halo - (CONV_W - 1) + i
        xc = xc + xs_ref[off:off + ts, :] * cw_ref[i:i + 1, :]
    yield

    xcb = xc.astype(BF16)
    rg, ig = [], []
    for c in range(width // MXU_DIM):
        sl = slice(c * MXU_DIM, (c + 1) * MXU_DIM)
        rg.append(jnp.dot(xcb[:, sl], wrg_ref[c], preferred_element_type=F32))
        ig.append(jnp.dot(xcb[:, sl], wig_ref[c], preferred_element_type=F32))
    r = _sigmoid(jnp.concatenate(rg, axis=1) + brg_ref[...])
    ig = _sigmoid(jnp.concatenate(ig, axis=1) + big_ref[...])
    yield

    log_a = (-LRU_C * r) * _softplus(-lam_ref[...])
    a_ref[...] = jnp.exp(log_a)
    th = jnp.tanh(log_a)
    b_ref[...] = jnp.sqrt((-2.0 * th) / (1.0 - th)) * (ig * xc)
    yield

    row = lax.broadcasted_iota(jnp.int32, (SUBLANES, width), 0)
    hprev = jnp.where(first, 0.0, hc_ref[...])
    groups = ts // SUBLANES
    for g in range(groups):
        rows = slice(g * SUBLANES, (g + 1) * SUBLANES)
        a8 = a_ref[rows, :]
        b8 = b_ref[rows, :]
        for d in (1, 2, 4):
            keep = row >= d
            b8 = jnp.where(keep, b8 + a8 * pltpu.roll(b8, d, axis=0), b8)
            a8 = jnp.where(keep, a8 * pltpu.roll(a8, d, axis=0), a8)
        h8 = b8 + a8 * hprev
        b_ref[rows, :] = h8
        hprev = jnp.broadcast_to(h8[SUBLANES - 1:SUBLANES, :], (SUBLANES, width))
        if (g + 1) % (groups // scan_parts) == 0:
            yield
    hc_ref[...] = hprev

    y = _gelu_tanh(u_ref[:, width:2 * width]) * b_ref[...]
    out_ref[...] = ((y * _rms_scale(y)) * gn_ref[...]).astype(out_ref.dtype)


def _interleave(*streams):
    streams = list(streams)
    while streams:
        for st in list(streams):
            if next(st, StopIteration) is StopIteration:
                streams.remove(st)


def _inproj_lru_kernel(x_ref, gm_ref, w_ref, cw_ref, cb_ref, wrg_ref, brg_ref, wig_ref, big_ref,
                       lam_ref, gn_ref, qkv_ref, ylru_ref, u_ref, xs_ref, a_ref, b_ref, hc_ref,
                       *, ts, width, tiles_per_seq, col_chunk, scan_parts):
    j = pl.program_id(0)
    slot = j % 2
    lt = jnp.maximum(j - 1, 0)

    @pl.when(j == 0)
    def _():
        for ref in (u_ref, xs_ref, hc_ref):
            ref[...] = jnp.zeros(ref.shape, ref.dtype)

    _interleave(
        _lru_stages(lt % tiles_per_seq == 0, u_ref.at[1 - slot], cw_ref, cb_ref, wrg_ref, brg_ref,
                    wig_ref, big_ref, lam_ref, gn_ref, ylru_ref, xs_ref, a_ref, b_ref, hc_ref,
                    ts=ts, width=width, scan_parts=scan_parts),
        _inproj_stages(x_ref, gm_ref, w_ref, u_ref.at[slot], qkv_ref, lru_cols=2 * width,
                       col_chunk=col_chunk))


def _inproj_lru(x2, gain, w_in_bf16, conv_w, conv_b, wrg_bd, b_rg, wig_bd, b_ig, lam, gain_lru,
                seq_len, ts):
    n, d = x2.shape
    cols = w_in_bf16.shape[1]
    width = conv_w.shape[1]
    n_tiles = n // ts
    tile = lambda lag: (lambda j: (jnp.clip(j - lag, 0, n_tiles - 1), 0))
    vec = lambda w: pl.BlockSpec((1, w), lambda j: (0, 0))
    wspec = pl.BlockSpec(wrg_bd.shape, lambda j: (0, 0, 0))
    return pl.pallas_call(
        functools.partial(_inproj_lru_kernel, ts=ts, width=width, tiles_per_seq=seq_len // ts,
                          col_chunk=2 * MXU_DIM, scan_parts=4),
        grid=(n_tiles + 1,),
        in_specs=[
            pl.BlockSpec((ts, d), tile(0)),
            vec(d),
            pl.BlockSpec((d, cols), lambda j: (0, 0), pipeline_mode=pl.Buffered(1)),
            pl.BlockSpec((CONV_W, width), lambda j: (0, 0)),
            vec(width), wspec, vec(width), wspec, vec(width), vec(width), vec(width),
        ],
        out_specs=[
            pl.BlockSpec((ts, cols - 2 * width), tile(0)),
            pl.BlockSpec((ts, width), tile(1)),
        ],
        out_shape=[
            jax.ShapeDtypeStruct((n, cols - 2 * width), BF16),
            jax.ShapeDtypeStruct((n, width), BF16),
        ],
        scratch_shapes=[
            pltpu.VMEM((2, ts, 2 * width), F32),
            pltpu.VMEM((ts + SUBLANES, width), F32),
            pltpu.VMEM((ts, width), F32),
            pltpu.VMEM((ts, width), F32),
            pltpu.VMEM((SUBLANES, width), F32),
        ],
        compiler_params=pltpu.CompilerParams(
            dimension_semantics=("arbitrary",), vmem_limit_bytes=VMEM_LIMIT_BYTES),
        name="inproj_lru",
    )(x2, gain, w_in_bf16, conv_w, conv_b, wrg_bd, b_rg, wig_bd, b_ig, lam, gain_lru)


def _block_diag(w, per):
    nb, bw, _ = w.shape
    w = w.reshape(nb // per, per, bw, bw)
    eye = jnp.eye(per, dtype=w.dtype)
    return jnp.einsum('gpij,pq->gpiqj', w, eye).reshape(nb // per, per * bw, per * bw)


def _ffn_stages(x_ref, yl_ref, ya, ga_ref, wo_ref, gf_ref, wi_ref, wd_ref, gl_ref, out_ref, *,
                d_ff, ff_chunk, final_norm):
    ya = ((ya * _rms_scale(ya)) * ga_ref[...]).astype(BF16)
    wl = yl_ref.shape[1]
    x1 = x_ref[...] + jnp.dot(yl_ref[...], wo_ref[0:wl, :], preferred_element_type=F32)
    x1 = x1 + jnp.dot(ya, wo_ref[wl:, :], preferred_element_type=F32)
    h2 = ((x1 * _rms_scale(x1)) * gf_ref[...]).astype(BF16)
    yield
    ffn = jnp.zeros_like(x1)
    for c in range(d_ff // ff_chunk):
        lo = c * ff_chunk
        gate = jnp.dot(h2, wi_ref[:, lo:lo + ff_chunk], preferred_element_type=F32)
        up = jnp.dot(h2, wi_ref[:, d_ff + lo:d_ff + lo + ff_chunk], preferred_element_type=F32)
        act = (gate * _sigmoid(gate)) * up
        ffn = ffn + jnp.dot(act.astype(BF16), wd_ref[lo:lo + ff_chunk, :],
                            preferred_element_type=F32)
        yield
    out = x1 + ffn
    if final_norm:
        out = (out * _rms_scale(out)) * gl_ref[...]
    out_ref[...] = out


def _attn_ffn_kernel(q_ref, kc_ref, kp_ref, vc_ref, vp_ref, qkv_hbm, x_ref, yl_ref, ga_ref,
                     wo_ref, gf_ref, wi_ref, wd_ref, gl_ref, out_ref,
                     yatt_ref, r_ref, acc_ref, kblk_ref, vblk_ref, sem,
                     *, tq, tk, nleft, scale2, tiles_per_seq, n_tiles, att_w, d_ff, ff_chunk,
                     final_norm):
    j = pl.program_id(0)
    ja = jnp.minimum(j, n_tiles - 1)
    bi = ja // tiles_per_seq
    ti = ja % tiles_per_seq
    nrb = tq // tk
    g0 = ti * nrb
    npair = att_w // LANES
    slot = j % 2

    @pl.when(j == 0)
    def _():
        yatt_ref[...] = jnp.zeros(yatt_ref.shape, F32)

    mixer = _ffn_stages(x_ref, yl_ref, yatt_ref[1 - slot], ga_ref, wo_ref, gf_ref, wi_ref, wd_ref,
                        gl_ref, out_ref, d_ff=d_ff, ff_chunk=ff_chunk, final_norm=final_norm)

    lane_q = lax.broadcasted_iota(jnp.int32, (tq, LANES), 1)
    rr = lax.broadcasted_iota(jnp.int32, (2 * tk, 2 * tk), 0)
    cc = lax.broadcasted_iota(jnp.int32, (2 * tk, 2 * tk), 1)
    key_j = jnp.where(rr >= tk, rr - tk, rr)
    tri = jnp.where((cc >= tk) | (key_j > cc), 1.0, 0.0).astype(BF16)
    below_diag = (lax.broadcasted_iota(jnp.int32, (tk, tk), 1)
                  < lax.broadcasted_iota(jnp.int32, (tk, tk), 0))
    has_left = ti > 0

    def softplus_parts(z):
        lb = jnp.minimum(z, 0.0) - jnp.log(1.0 + jnp.exp2(-jnp.abs(z))) * LOG2_E
        return lb, lb - z

    def suffix_and_total(l):
        hi = l.astype(BF16)
        lo = (l - hi.astype(F32)).astype(BF16)
        cs = jnp.dot(jnp.concatenate([hi, lo], axis=1), tri, preferred_element_type=F32)
        return cs[:, :tk], cs[:, tk:]

    def head_queries(g):
        q = (q_ref[0, :, g * LANES:(g + 1) * LANES].astype(F32) * scale2).astype(BF16)
        return (jnp.where(lane_q < HEAD_DIM, q, jnp.zeros_like(q)),
                jnp.where(lane_q >= HEAD_DIM, q, jnp.zeros_like(q)))

    def scores(qh, k_blk):
        return lax.dot_general(qh, k_blk, (((1,), (1,)), ((), ())), preferred_element_type=F32)

    def attention_stages():
        for g in range(npair):
            cols = slice(g * LANES, (g + 1) * LANES)
            q_heads = head_queries(g)
            k_all = jnp.concatenate([kp_ref[0, :, cols], kc_ref[0, :, cols]], axis=0)
            v_all = jnp.concatenate([vp_ref[0, :, cols], vc_ref[0, :, cols]], axis=0)
            chains = []
            for rb in range(nrb):
                rows = slice(rb * tk, (rb + 1) * tk)
                win = slice(rb * tk, (rb + nleft + 1) * tk)
                for h in range(2):
                    chains.append(dict(h=2 * g + h, rb=rb, rows=rows, v=v_all[win, :],
                                       z=scores(q_heads[h][rows, :], k_all[win, :])))
            yield
            for c in chains:
                c["lb"], c["parts"] = [], []
                for kb in range(nleft + 1):
                    zb = c["z"][:, kb * tk:(kb + 1) * tk]
                    if kb == nleft:
                        zb = jnp.where(below_diag, zb, -MASKED_SCORE)
                    elif c["rb"] + kb < nleft:
                        zb = jnp.where(has_left, zb, -MASKED_SCORE)
                    lb, l = softplus_parts(zb)
                    c["lb"].append(lb)
                    c["parts"].append(suffix_and_total(l))
            yield
            for c in chains:
                r = jnp.zeros((tk, tk), F32)
                atts = [None] * (nleft + 1)
                for kb in reversed(range(nleft + 1)):
                    suffix, total = c["parts"][kb]
                    atts[kb] = jnp.exp2(c["lb"][kb] + suffix + r).astype(BF16)
                    r = r + total
                c["att"] = jnp.concatenate(atts, axis=1)
                r_ref[c["h"], c["rows"], :] = r
            yield
            for c in chains:
                acc_ref[c["h"], c["rows"], :] = jnp.dot(c["att"], c["v"],
                                                        preferred_element_type=F32)
            yield

    _interleave(attention_stages(), mixer)

    def cond(carry):
        s, alive = carry
        return jnp.logical_and(s <= g0 + (nrb - 1 - nleft - 1), alive)

    def body(carry):
        s, _ = carry
        for g in range(npair):
            q_heads = head_queries(g)
            for rb in range(nrb):
                jb = g0 + (rb - nleft - 1) - s

                @pl.when(jb >= 0)
                def _():
                    rows = slice(rb * tk, (rb + 1) * tk)
                    ks = pl.multiple_of(jb * tk, tk)
                    copies = [
                        pltpu.make_async_copy(
                            qkv_hbm.at[bi, pl.ds(ks, tk), pl.ds((1 + p) * att_w + g * LANES, LANES)],
                            dst, sem.at[p])
                        for p, dst in enumerate((kblk_ref, vblk_ref))]
                    for cp in copies:
                        cp.start()
                    for cp in copies:
                        cp.wait()
                    for h in range(2):
                        lb, l = softplus_parts(scores(q_heads[h][rows, :], kblk_ref[...]))
                        suffix, total = suffix_and_total(l)
                        r_old = r_ref[2 * g + h, rows, :]
                        att = jnp.exp2(lb + suffix + r_old).astype(BF16)
                        acc_ref[2 * g + h, rows, :] += jnp.dot(att, vblk_ref[...],
                                                               preferred_element_type=F32)
                        r_ref[2 * g + h, rows, :] = r_old + total
        return s + 1, jnp.max(r_ref[...]) >= ATT_DEAD_LOG2

    lax.while_loop(cond, body, (0, jnp.max(r_ref[...]) >= ATT_DEAD_LOG2))

    for g in range(npair):
        yatt_ref[slot, :, g * LANES:(g + 1) * LANES] = jnp.where(
            lane_q < HEAD_DIM, acc_ref[2 * g], acc_ref[2 * g + 1])


def _attn_ffn(qkv, x2, y_lru, g_att, w_out, g_ffn, w_ffn_in, w_ffn_out, g_last, tq, tk, nleft,
              ff_chunk, final_norm):
    bsz, s, three_w = qkv.shape
    att_w = three_w // 3
    n, d = x2.shape
    d_ff = w_ffn_out.shape[0]
    tiles_per_seq = s // tq
    n_tiles = bsz * tiles_per_seq
    left = nleft * tk
    assert tq % tk == 0 and tq // tk >= nleft and tq % left == 0 and att_w % LANES == 0

    def tile(j):
        ja = jnp.minimum(j, n_tiles - 1)
        return ja // tiles_per_seq, ja % tiles_per_seq

    def cur(part):
        return pl.BlockSpec((1, tq, att_w), lambda j: (*tile(j), part))

    def prev(part):
        def index(j):
            b, t = tile(j)
            return b, jnp.maximum(t * (tq // left) - 1, 0), part
        return pl.BlockSpec((1, left, att_w), index)

    row = lambda w: pl.BlockSpec((tq, w), lambda j: (jnp.maximum(j - 1, 0), 0))
    vec = lambda w: pl.BlockSpec((1, w), lambda j: (0, 0))
    whole = lambda a: pl.BlockSpec(a.shape, lambda j: (0, 0), pipeline_mode=pl.Buffered(1))
    nheads = att_w // HEAD_DIM
    return pl.pallas_call(
        functools.partial(_attn_ffn_kernel, tq=tq, tk=tk, nleft=nleft,
                          scale2=LOG2_E / math.sqrt(HEAD_DIM), tiles_per_seq=tiles_per_seq,
                          n_tiles=n_tiles, att_w=att_w, d_ff=d_ff, ff_chunk=ff_chunk,
                          final_norm=final_norm),
        grid=(n_tiles + 1,),
        in_specs=[cur(0), cur(1), prev(1), cur(2), prev(2),
                  pl.BlockSpec(memory_space=pl.ANY),
                  row(d), row(y_lru.shape[1]), vec(att_w), whole(w_out), vec(d),
                  whole(w_ffn_in), whole(w_ffn_out), vec(d)],
        out_specs=row(d),
        out_shape=jax.ShapeDtypeStruct((n, d), F32),
        scratch_shapes=[
            pltpu.VMEM((2, tq, att_w), F32),
            pltpu.VMEM((nheads, tq, tk), F32),
            pltpu.VMEM((nheads, tq, LANES), F32),
            pltpu.VMEM((tk, LANES), BF16),
            pltpu.VMEM((tk, LANES), BF16),
            pltpu.SemaphoreType.DMA((2,)),
        ],
        compiler_params=pltpu.CompilerParams(
            dimension_semantics=("arbitrary",), vmem_limit_bytes=VMEM_LIMIT_BYTES),
        name="attn_ffn",
    )(qkv, qkv, qkv, qkv, qkv, qkv, x2, y_lru, g_att, w_out, g_ffn, w_ffn_in, w_ffn_out, g_last)


def _largest_chunk(total, limit):
    best = LANES
    for c in range(LANES, limit + 1, LANES):
        if total % c == 0:
            best = c
    return best


def kernel(x, norm_mix, w_in, conv_w, conv_b, w_rg, b_rg, w_ig, b_ig, lru_lambda, norm_lru_out,
           norm_att_out, w_out, norm_ffn, w_ffn_in, w_ffn_out, norm_final):
    bsz, s, d = x.shape
    depth = w_in.shape[0]
    lru_w = conv_w.shape[2]
    att_w = norm_att_out.shape[1]
    nblk, blk_w = w_rg.shape[1], w_rg.shape[2]
    assert w_in.shape[2] == 2 * lru_w + 3 * att_w and att_w % LANES == 0
    assert MXU_DIM % blk_w == 0 and lru_w % MXU_DIM == 0

    n = bsz * s
    ts = 512
    tq, tk, nleft = 512, 128, 2
    per = MXU_DIM // blk_w
    row1 = lambda v: v.reshape(1, -1)

    x2 = x.reshape(n, d)
    for l in range(depth):
        qkv, y_lru = _inproj_lru(x2, row1(norm_mix[l]), w_in[l].astype(BF16), conv_w[l],
                                 row1(conv_b[l]), _block_diag(w_rg[l], per).astype(BF16),
                                 row1(b_rg[l]), _block_diag(w_ig[l], per).astype(BF16),
                                 row1(b_ig[l]), row1(lru_lambda[l]), row1(norm_lru_out[l]), s, ts)
        d_ff = w_ffn_out.shape[1]
        x2 = _attn_ffn(qkv.reshape(bsz, s, 3 * att_w), x2, y_lru,
                       row1(norm_att_out[l]), w_out[l].astype(BF16), row1(norm_ffn[l]),
                       w_ffn_in[l].astype(BF16), w_ffn_out[l].astype(BF16), row1(norm_final),
                       tq, tk, nleft, _largest_chunk(d_ff, 512), l == depth - 1)
    return x2.reshape(bsz, s, d)
```

```python
import functools
import math

import jax
import jax.numpy as jnp
from jax import lax
from jax.experimental import pallas as pl
from jax.experimental.pallas import tpu as pltpu

F32 = jnp.float32
BF16 = jnp.bfloat16

EPS = 1e-6
LRU_C = 8.0
CONV_W = 4
HEAD_DIM = 64

LANES = 128
SUBLANES = 8
MXU_DIM = 256
VMEM_LIMIT_BYTES = 56 * 1024 * 1024

LOG2_E = 1.4426950408889634
ATT_DEAD_LOG2 = -160.0
MASKED_SCORE = 1e30


def _rms_scale(x):
    return lax.rsqrt(jnp.mean(x * x, axis=-1, keepdims=True) + EPS)


def _sigmoid(x):
    return 1.0 / (1.0 + jnp.exp(-x))


def _softplus(x):
    return jnp.maximum(x, 0.0) + jnp.log(1.0 + jnp.exp(-jnp.abs(x)))


def _gelu_tanh(x):
    c = math.sqrt(2.0 / math.pi)
    return 0.5 * x * (1.0 + jnp.tanh(c * (x + 0.044715 * (x * x * x))))


def _inproj_stages(x_ref, g_ref, w_ref, u_slot_ref, qkv_ref, *, lru_cols, col_chunk):
    x = x_ref[...]
    h = ((x * _rms_scale(x)) * g_ref[...]).astype(BF16)
    yield
    for lo in range(0, w_ref.shape[1], col_chunk):
        u = jnp.dot(h, w_ref[:, lo:lo + col_chunk], preferred_element_type=F32)
        if lo < lru_cols:
            u_slot_ref[:, lo:lo + col_chunk] = u
        else:
            qkv_ref[:, lo - lru_cols:lo - lru_cols + col_chunk] = u.astype(BF16)
        yield


def _lru_stages(first, u_ref, cw_ref, cb_ref, wrg_ref, brg_ref, wig_ref, big_ref, lam_ref, gn_ref,
                out_ref, xs_ref, hc_ref, *, ts, width, slab):
    halo = SUBLANES
    xs_ref[0:halo, :] = jnp.where(first, 0.0, xs_ref[ts:ts + halo, :])
    xs_ref[halo:halo + ts, :] = u_ref[:, 0:width]
    row = lax.broadcasted_iota(jnp.int32, (SUBLANES, width), 0)
    hprev = jnp.where(first, 0.0, hc_ref[...])
    neg_sp = -LRU_C * _softplus(-lam_ref[...])

    for s0 in range(0, ts, slab):
        xc = jnp.broadcast_to(cb_ref[...], (slab, width))
        for i in range(CONV_W):
            off = s0 + halo - (CONV_W - 1) + i
            xc = xc + xs_ref[off:off + slab, :] * cw_ref[i:i + 1, :]

        xcb = xc.astype(BF16)
        rg, ig = [], []
        for c in range(width // MXU_DIM):
            sl = slice(c * MXU_DIM, (c + 1) * MXU_DIM)
            rg.append(jnp.dot(xcb[:, sl], wrg_ref[c], preferred_element_type=F32))
            ig.append(jnp.dot(xcb[:, sl], wig_ref[c], preferred_element_type=F32))
        r = _sigmoid(jnp.concatenate(rg, axis=1) + brg_ref[...])
        ig = _sigmoid(jnp.concatenate(ig, axis=1) + big_ref[...])

        log_a = r * neg_sp
        a = jnp.exp(log_a)
        th = jnp.tanh(log_a)
        b = jnp.sqrt((-2.0 * th) / (1.0 - th)) * (ig * xc)
        yield

        hs = []
        for g in range(slab // SUBLANES):
            rows = slice(g * SUBLANES, (g + 1) * SUBLANES)
            a8, b8 = a[rows, :], b[rows, :]
            for d in (1, 2, 4):
                keep = row >= d
                b8 = jnp.where(keep, b8 + a8 * pltpu.roll(b8, d, axis=0), b8)
                a8 = jnp.where(keep, a8 * pltpu.roll(a8, d, axis=0), a8)
            h8 = b8 + a8 * hprev
            hs.append(h8)
            hprev = jnp.broadcast_to(h8[SUBLANES - 1:SUBLANES, :], (SUBLANES, width))
        y = _gelu_tanh(u_ref[s0:s0 + slab, width:2 * width]) * jnp.concatenate(hs, axis=0)
        out_ref[s0:s0 + slab, :] = ((y * _rms_scale(y)) * gn_ref[...]).astype(out_ref.dtype)
        yield
    hc_ref[...] = hprev


def _interleave(*streams):
    streams = list(streams)
    while streams:
        for st in list(streams):
            if next(st, StopIteration) is StopIteration:
                streams.remove(st)


def _inproj_lru_kernel(x_ref, gm_ref, w_ref, cw_ref, cb_ref, wrg_ref, brg_ref, wig_ref, big_ref,
                       lam_ref, gn_ref, *rest, ts, width, tiles_per_seq, col_chunk, slab,
                       cast_slabs):
    nw = len(cast_slabs)
    wsrc_refs, (qkv_ref, ylru_ref) = rest[:nw], rest[nw:nw + 2]
    wdst_refs = rest[nw + 2:2 * nw + 2]
    u_ref, xs_ref, hc_ref = rest[2 * nw + 2:]
    j = pl.program_id(0)
    for src_ref, dst_ref, n_slabs in zip(wsrc_refs, wdst_refs, cast_slabs):
        @pl.when(j < n_slabs)
        def _():
            dst_ref[...] = src_ref[...].astype(dst_ref.dtype)

    slot = j % 2
    lt = jnp.maximum(j - 1, 0)

    @pl.when(j == 0)
    def _():
        for ref in (u_ref, xs_ref, hc_ref):
            ref[...] = jnp.zeros(ref.shape, ref.dtype)

    _interleave(
        _lru_stages(lt % tiles_per_seq == 0, u_ref.at[1 - slot], cw_ref, cb_ref, wrg_ref, brg_ref,
                    wig_ref, big_ref, lam_ref, gn_ref, ylru_ref, xs_ref, hc_ref,
                    ts=ts, width=width, slab=slab),
        _inproj_stages(x_ref, gm_ref, w_ref, u_ref.at[slot], qkv_ref, lru_cols=2 * width,
                       col_chunk=col_chunk))


def _cast_slab_rows(rows, max_slabs):
    step = 2 * SUBLANES
    for slab_rows in range(step, rows + 1, step):
        if rows % slab_rows == 0 and rows // slab_rows <= max_slabs:
            return slab_rows
    raise ValueError(f"no bf16 row slab for {rows} rows in {max_slabs} steps")


def _inproj_lru(x2, gain, w_in_bf16, conv_w, conv_b, wrg_bd, b_rg, wig_bd, b_ig, lam, gain_lru,
                later_weights, seq_len, ts):
    n, d = x2.shape
    cols = w_in_bf16.shape[1]
    width = conv_w.shape[1]
    n_tiles = n // ts
    slab_rows = [_cast_slab_rows(w.shape[0], n_tiles) for w in later_weights]
    cast_slabs = tuple(w.shape[0] // r for w, r in zip(later_weights, slab_rows))
    wslab = lambda w, r: pl.BlockSpec(
        (r, w.shape[1]), lambda j: (jnp.minimum(j, w.shape[0] // r - 1), 0))
    wspecs = [wslab(w, r) for w, r in zip(later_weights, slab_rows)]
    tile = lambda lag: (lambda j: (jnp.clip(j - lag, 0, n_tiles - 1), 0))
    vec = lambda w: pl.BlockSpec((1, w), lambda j: (0, 0))
    wspec = pl.BlockSpec(wrg_bd.shape, lambda j: (0, 0, 0))
    return pl.pallas_call(
        functools.partial(_inproj_lru_kernel, ts=ts, width=width, tiles_per_seq=seq_len // ts,
                          col_chunk=MXU_DIM, slab=LANES // 2, cast_slabs=cast_slabs),
        grid=(n_tiles + 1,),
        in_specs=[
            pl.BlockSpec((ts, d), tile(0)),
            vec(d),
            pl.BlockSpec((d, cols), lambda j: (0, 0), pipeline_mode=pl.Buffered(1)),
            pl.BlockSpec((CONV_W, width), lambda j: (0, 0)),
            vec(width), wspec, vec(width), wspec, vec(width), vec(width), vec(width),
            *wspecs,
        ],
        out_specs=[
            pl.BlockSpec((ts, cols - 2 * width), tile(0)),
            pl.BlockSpec((ts, width), tile(1)),
            *wspecs,
        ],
        out_shape=[
            jax.ShapeDtypeStruct((n, cols - 2 * width), BF16),
            jax.ShapeDtypeStruct((n, width), BF16),
            *[jax.ShapeDtypeStruct(w.shape, BF16) for w in later_weights],
        ],
        scratch_shapes=[
            pltpu.VMEM((2, ts, 2 * width), F32),
            pltpu.VMEM((ts + SUBLANES, width), F32),
            pltpu.VMEM((SUBLANES, width), F32),
        ],
        compiler_params=pltpu.CompilerParams(
            dimension_semantics=("arbitrary",), vmem_limit_bytes=VMEM_LIMIT_BYTES),
        name="inproj_lru",
    )(x2, gain, w_in_bf16, conv_w, conv_b, wrg_bd, b_rg, wig_bd, b_ig, lam, gain_lru,
      *later_weights)


def _block_diag(w, per):
    nb, bw, _ = w.shape
    w = w.reshape(nb // per, per, bw, bw)
    eye = jnp.eye(per, dtype=w.dtype)
    return jnp.einsum('gpij,pq->gpiqj', w, eye).reshape(nb // per, per * bw, per * bw)


def _ffn_stages(x_ref, yl_ref, ya, ga_ref, wo_ref, gf_ref, wi_ref, wd_ref, gl_ref, out_ref, *,
                d_ff, ff_chunk, final_norm):
    ya = ((ya * _rms_scale(ya)) * ga_ref[...]).astype(BF16)
    wl = yl_ref.shape[1]
    x1 = x_ref[...] + jnp.dot(yl_ref[...], wo_ref[0:wl, :], preferred_element_type=F32)
    x1 = x1 + jnp.dot(ya, wo_ref[wl:, :], preferred_element_type=F32)
    h2 = ((x1 * _rms_scale(x1)) * gf_ref[...]).astype(BF16)
    yield
    ffn = jnp.zeros_like(x1)
    for c in range(d_ff // ff_chunk):
        lo = c * ff_chunk
        gate = jnp.dot(h2, wi_ref[:, lo:lo + ff_chunk], preferred_element_type=F32)
        up = jnp.dot(h2, wi_ref[:, d_ff + lo:d_ff + lo + ff_chunk], preferred_element_type=F32)
        act = (gate * _sigmoid(gate)) * up
        ffn = ffn + jnp.dot(act.astype(BF16), wd_ref[lo:lo + ff_chunk, :],
                            preferred_element_type=F32)
        yield
    out = x1 + ffn
    if final_norm:
        out = (out * _rms_scale(out)) * gl_ref[...]
    out_ref[...] = out


def _attn_ffn_kernel(q_ref, kc_ref, kp_ref, vc_ref, vp_ref, qkv_hbm, x_ref, yl_ref, ga_ref,
                     wo_ref, gf_ref, wi_ref, wd_ref, gl_ref, out_ref,
                     yatt_ref, r_ref, acc_ref, kblk_ref, vblk_ref, sem,
                     *, tq, tk, nleft, scale2, tiles_per_seq, n_tiles, att_w, d_ff, ff_chunk,
                     final_norm):
    j = pl.program_id(0)
    ja = jnp.minimum(j, n_tiles - 1)
    bi = ja // tiles_per_seq
    ti = ja % tiles_per_seq
    nrb = tq // tk
    g0 = ti * nrb
    npair = att_w // LANES
    slot = j % 2

    @pl.when(j == 0)
    def _():
        yatt_ref[...] = jnp.zeros(yatt_ref.shape, F32)

    mixer = _ffn_stages(x_ref, yl_ref, yatt_ref[1 - slot], ga_ref, wo_ref, gf_ref, wi_ref, wd_ref,
                        gl_ref, out_ref, d_ff=d_ff, ff_chunk=ff_chunk, final_norm=final_norm)

    lane_q = lax.broadcasted_iota(jnp.int32, (tq, LANES), 1)
    rr = lax.broadcasted_iota(jnp.int32, (2 * tk, 2 * tk), 0)
    cc = lax.broadcasted_iota(jnp.int32, (2 * tk, 2 * tk), 1)
    key_j = jnp.where(rr >= tk, rr - tk, rr)
    tri = jnp.where((cc >= tk) | (key_j > cc), 1.0, 0.0).astype(BF16)
    below_diag = (lax.broadcasted_iota(jnp.int32, (tk, tk), 1)
                  < lax.broadcasted_iota(jnp.int32, (tk, tk), 0))
    has_left = ti > 0

    def softplus_parts(z):
        lb = jnp.minimum(z, 0.0) - jnp.log(1.0 + jnp.exp2(-jnp.abs(z))) * LOG2_E
        return lb, lb - z

    def suffix_and_total(l):
        hi = l.astype(BF16)
        lo = (l - hi.astype(F32)).astype(BF16)
        cs = jnp.dot(jnp.concatenate([hi, lo], axis=1), tri, preferred_element_type=F32)
        return cs[:, :tk], cs[:, tk:]

    def head_queries(g):
        q = (q_ref[0, :, g * LANES:(g + 1) * LANES].astype(F32) * scale2).astype(BF16)
        return (jnp.where(lane_q < HEAD_DIM, q, jnp.zeros_like(q)),
                jnp.where(lane_q >= HEAD_DIM, q, jnp.zeros_like(q)))

    def scores(qh, k_blk):
        return lax.dot_general(qh, k_blk, (((1,), (1,)), ((), ())), preferred_element_type=F32)

    def attention_stages():
        for g in range(npair):
            cols = slice(g * LANES, (g + 1) * LANES)
            q_heads = head_queries(g)
            k_all = jnp.concatenate([kp_ref[0, :, cols], kc_ref[0, :, cols]], axis=0)
            v_all = jnp.concatenate([vp_ref[0, :, cols], vc_ref[0, :, cols]], axis=0)
            chains = []
            for rb in range(nrb):
                rows = slice(rb * tk, (rb + 1) * tk)
                win = slice(rb * tk, (rb + nleft + 1) * tk)
                for h in range(2):
                    chains.append(dict(h=2 * g + h, rb=rb, rows=rows, v=v_all[win, :],
                                       z=scores(q_heads[h][rows, :], k_all[win, :])))
            yield
            for c in chains:
                c["lb"], c["parts"] = [], []
                for kb in range(nleft + 1):
                    zb = c["z"][:, kb * tk:(kb + 1) * tk]
                    if kb == nleft:
                        zb = jnp.where(below_diag, zb, -MASKED_SCORE)
                    elif c["rb"] + kb < nleft:
                        zb = jnp.where(has_left, zb, -MASKED_SCORE)
                    lb, l = softplus_parts(zb)
                    c["lb"].append(lb)
                    c["parts"].append(suffix_and_total(l))
            yield
            for c in chains:
                r = jnp.zeros((tk, tk), F32)
                atts = [None] * (nleft + 1)
                for kb in reversed(range(nleft + 1)):
                    suffix, total = c["parts"][kb]
                    atts[kb] = jnp.exp2(c["lb"][kb] + suffix + r).astype(BF16)
                    r = r + total
                c["att"] = jnp.concatenate(atts, axis=1)
                r_ref[c["h"], c["rows"], :] = r
            yield
            for c in chains:
                acc_ref[c["h"], c["rows"], :] = jnp.dot(c["att"], c["v"],
                                                        preferred_element_type=F32)
            yield

    _interleave(attention_stages(), mixer)

    def cond(carry):
        s, alive = carry
        return jnp.logical_and(s <= g0 + (nrb - 1 - nleft - 1), alive)

    def body(carry):
        s, _ = carry
        for g in range(npair):
            q_heads = head_queries(g)
            for rb in range(nrb):
                jb = g0 + (rb - nleft - 1) - s

                @pl.when(jb >= 0)
                def _():
                    rows = slice(rb * tk, (rb + 1) * tk)
                    ks = pl.multiple_of(jb * tk, tk)
                    copies = [
                        pltpu.make_async_copy(
                            qkv_hbm.at[bi, pl.ds(ks, tk), pl.ds((1 + p) * att_w + g * LANES, LANES)],
                            dst, sem.at[p])
                        for p, dst in enumerate((kblk_ref, vblk_ref))]
                    for cp in copies:
                        cp.start()
                    for cp in copies:
                        cp.wait()
                    for h in range(2):
                        lb, l = softplus_parts(scores(q_heads[h][rows, :], kblk_ref[...]))
                        suffix, total = suffix_and_total(l)
                        r_old = r_ref[2 * g + h, rows, :]
                        att = jnp.exp2(lb + suffix + r_old).astype(BF16)
                        acc_ref[2 * g + h, rows, :] += jnp.dot(att, vblk_ref[...],
                                                               preferred_element_type=F32)
                        r_ref[2 * g + h, rows, :] = r_old + total
        return s + 1, jnp.max(r_ref[...]) >= ATT_DEAD_LOG2

    lax.while_loop(cond, body, (0, jnp.max(r_ref[...]) >= ATT_DEAD_LOG2))

    for g in range(npair):
        yatt_ref[slot, :, g * LANES:(g + 1) * LANES] = jnp.where(
            lane_q < HEAD_DIM, acc_ref[2 * g], acc_ref[2 * g + 1])


def _attn_ffn(qkv, x2, y_lru, g_att, w_out, g_ffn, w_ffn_in, w_ffn_out, g_last, tq, tk, nleft,
              ff_chunk, final_norm):
    bsz, s, three_w = qkv.shape
    att_w = three_w // 3
    n, d = x2.shape
    d_ff = w_ffn_out.shape[0]
    tiles_per_seq = s // tq
    n_tiles = bsz * tiles_per_seq
    left = nleft * tk
    assert tq % tk == 0 and tq // tk >= nleft and tq % left == 0 and att_w % LANES == 0

    def tile(j):
        ja = jnp.minimum(j, n_tiles - 1)
        return ja // tiles_per_seq, ja % tiles_per_seq

    def cur(part):
        return pl.BlockSpec((1, tq, att_w), lambda j: (*tile(j), part))

    def prev(part):
        def index(j):
            b, t = tile(j)
            return b, jnp.maximum(t * (tq // left) - 1, 0), part
        return pl.BlockSpec((1, left, att_w), index)

    row = lambda w: pl.BlockSpec((tq, w), lambda j: (jnp.maximum(j - 1, 0), 0))
    vec = lambda w: pl.BlockSpec((1, w), lambda j: (0, 0))
    whole = lambda a: pl.BlockSpec(a.shape, lambda j: (0, 0), pipeline_mode=pl.Buffered(1))
    nheads = att_w // HEAD_DIM
    return pl.pallas_call(
        functools.partial(_attn_ffn_kernel, tq=tq, tk=tk, nleft=nleft,
                          scale2=LOG2_E / math.sqrt(HEAD_DIM), tiles_per_seq=tiles_per_seq,
                          n_tiles=n_tiles, att_w=att_w, d_ff=d_ff, ff_chunk=ff_chunk,
                          final_norm=final_norm),
        grid=(n_tiles + 1,),
        in_specs=[cur(0), cur(1), prev(1), cur(2), prev(2),
                  pl.BlockSpec(memory_space=pl.ANY),
                  row(d), row(y_lru.shape[1]), vec(att_w), whole(w_out), vec(d),
                  whole(w_ffn_in), whole(w_ffn_out), vec(d)],
        out_specs=row(d),
        out_shape=jax.ShapeDtypeStruct((n, d), F32),
        scratch_shapes=[
            pltpu.VMEM((2, tq, att_w), F32),
            pltpu.VMEM((nheads, tq, tk), F32),
            pltpu.VMEM((nheads, tq, LANES), F32),
            pltpu.VMEM((tk, LANES), BF16),
            pltpu.VMEM((tk, LANES), BF16),
            pltpu.SemaphoreType.DMA((2,)),
        ],
        compiler_params=pltpu.CompilerParams(
            dimension_semantics=("arbitrary",), vmem_limit_bytes=VMEM_LIMIT_BYTES),
        name="attn_ffn",
    )(qkv, qkv, qkv, qkv, qkv, qkv, x2, y_lru, g_att, w_out, g_ffn, w_ffn_in, w_ffn_out, g_last)


def _largest_chunk(total, limit):
    best = LANES
    for c in range(LANES, limit + 1, LANES):
        if total % c == 0:
            best = c
    return best


def kernel(x, norm_mix, w_in, conv_w, conv_b, w_rg, b_rg, w_ig, b_ig, lru_lambda, norm_lru_out,
           norm_att_out, w_out, norm_ffn, w_ffn_in, w_ffn_out, norm_final):
    bsz, s, d = x.shape
    depth = w_in.shape[0]
    lru_w = conv_w.shape[2]
    att_w = norm_att_out.shape[1]
    nblk, blk_w = w_rg.shape[1], w_rg.shape[2]
    assert w_in.shape[2] == 2 * lru_w + 3 * att_w and att_w % LANES == 0
    assert MXU_DIM % blk_w == 0 and lru_w % MXU_DIM == 0

    n = bsz * s
    ts = 512
    tq, tk, nleft = 512, 128, 2
    per = MXU_DIM // blk_w
    row1 = lambda v: v.reshape(1, -1)

    x2 = x.reshape(n, d)
    for l in range(depth):
        qkv, y_lru, w_out_b, w_ffn_in_b, w_ffn_out_b = _inproj_lru(
            x2, row1(norm_mix[l]), w_in[l].astype(BF16), conv_w[l], row1(conv_b[l]),
            _block_diag(w_rg[l], per).astype(BF16), row1(b_rg[l]),
            _block_diag(w_ig[l], per).astype(BF16), row1(b_ig[l]), row1(lru_lambda[l]),
            row1(norm_lru_out[l]), (w_out[l], w_ffn_in[l], w_ffn_out[l]), s, ts)
        d_ff = w_ffn_out.shape[1]
        x2 = _attn_ffn(qkv.reshape(bsz, s, 3 * att_w), x2, y_lru, row1(norm_att_out[l]), w_out_b,
                       row1(norm_ffn[l]), w_ffn_in_b, w_ffn_out_b, row1(norm_final),
                       tq, tk, nleft, _largest_chunk(d_ff, 512), l == depth - 1)
    return x2.reshape(bsz, s, d)
```

```python
import functools
import math

import jax
import jax.numpy as jnp
from jax import lax
from jax.experimental import pallas as pl
from jax.experimental.pallas import tpu as pltpu

F32 = jnp.float32
BF16 = jnp.bfloat16

EPS = 1e-6
LRU_C = 8.0
CONV_W = 4
HEAD_DIM = 64

LANES = 128
SUBLANES = 8
MXU_DIM = 256
VMEM_LIMIT_BYTES = 56 * 1024 * 1024

LOG2_E = 1.4426950408889634
ATT_DEAD_LOG2 = -160.0
MASKED_SCORE = 1e30
CAST_STEPS = 8


def _rms_scale(x):
    return lax.rsqrt(jnp.mean(x * x, axis=-1, keepdims=True) + EPS)


def _sigmoid(x):
    return 1.0 / (1.0 + jnp.exp(-x))


def _softplus(x):
    return jnp.maximum(x, 0.0) + jnp.log(1.0 + jnp.exp(-jnp.abs(x)))


def _gelu_tanh(x):
    c = math.sqrt(2.0 / math.pi)
    return 0.5 * x * (1.0 + jnp.tanh(c * (x + 0.044715 * (x * x * x))))


def _inproj_stages(x_ref, g_ref, w_ref, u_slot_ref, qkv_ref, *, lru_cols, col_chunk):
    x = x_ref[...]
    h = ((x * _rms_scale(x)) * g_ref[...]).astype(BF16)
    yield
    for lo in range(0, w_ref.shape[1], col_chunk):
        u = jnp.dot(h, w_ref[:, lo:lo + col_chunk], preferred_element_type=F32)
        if lo < lru_cols:
            u_slot_ref[:, lo:lo + col_chunk] = u
        else:
            qkv_ref[:, lo - lru_cols:lo - lru_cols + col_chunk] = u.astype(BF16)
        yield


def _lru_stages(first, u_ref, cw_ref, cb_ref, wrg_ref, brg_ref, wig_ref, big_ref, lam_ref, gn_ref,
                out_ref, xs_ref, hc_ref, *, ts, width, slab):
    halo = SUBLANES
    xs_ref[0:halo, :] = jnp.where(first, 0.0, xs_ref[ts:ts + halo, :])
    xs_ref[halo:halo + ts, :] = u_ref[:, 0:width]
    row = lax.broadcasted_iota(jnp.int32, (SUBLANES, width), 0)
    hprev = jnp.where(first, 0.0, hc_ref[...])
    neg_sp = -LRU_C * _softplus(-lam_ref[...])

    for s0 in range(0, ts, slab):
        xc = jnp.broadcast_to(cb_ref[...], (slab, width))
        for i in range(CONV_W):
            off = s0 + halo - (CONV_W - 1) + i
            xc = xc + xs_ref[off:off + slab, :] * cw_ref[i:i + 1, :]

        xcb = xc.astype(BF16)
        rg, ig = [], []
        for c in range(width // MXU_DIM):
            sl = slice(c * MXU_DIM, (c + 1) * MXU_DIM)
            rg.append(jnp.dot(xcb[:, sl], wrg_ref[c], preferred_element_type=F32))
            ig.append(jnp.dot(xcb[:, sl], wig_ref[c], preferred_element_type=F32))
        r = _sigmoid(jnp.concatenate(rg, axis=1) + brg_ref[...])
        ig = _sigmoid(jnp.concatenate(ig, axis=1) + big_ref[...])

        log_a = r * neg_sp
        a = jnp.exp(log_a)
        th = jnp.tanh(log_a)
        b = jnp.sqrt((-2.0 * th) / (1.0 - th)) * (ig * xc)
        yield

        hs = []
        for g in range(slab // SUBLANES):
            rows = slice(g * SUBLANES, (g + 1) * SUBLANES)
            a8, b8 = a[rows, :], b[rows, :]
            for d in (1, 2, 4):
                keep = row >= d
                b8 = jnp.where(keep, b8 + a8 * pltpu.roll(b8, d, axis=0), b8)
                a8 = jnp.where(keep, a8 * pltpu.roll(a8, d, axis=0), a8)
            h8 = b8 + a8 * hprev
            hs.append(h8)
            hprev = jnp.broadcast_to(h8[SUBLANES - 1:SUBLANES, :], (SUBLANES, width))
        y = _gelu_tanh(u_ref[s0:s0 + slab, width:2 * width]) * jnp.concatenate(hs, axis=0)
        out_ref[s0:s0 + slab, :] = ((y * _rms_scale(y)) * gn_ref[...]).astype(out_ref.dtype)
        yield
    hc_ref[...] = hprev


def _interleave(*streams):
    streams = list(streams)
    while streams:
        for st in list(streams):
            if next(st, StopIteration) is StopIteration:
                streams.remove(st)


def _inproj_lru_kernel(x_ref, gm_ref, w_ref, cw_ref, cb_ref, wrg_ref, brg_ref, wig_ref, big_ref,
                       lam_ref, gn_ref, *rest, ts, width, tiles_per_seq, col_chunk, slab,
                       cast_slabs):
    nw = len(cast_slabs)
    wsrc_refs, (qkv_ref, ylru_ref) = rest[:nw], rest[nw:nw + 2]
    wdst_refs = rest[nw + 2:2 * nw + 2]
    u_ref, xs_ref, hc_ref = rest[2 * nw + 2:]
    j = pl.program_id(0)
    for src_ref, dst_ref, n_slabs in zip(wsrc_refs, wdst_refs, cast_slabs):
        @pl.when(j < n_slabs)
        def _():
            dst_ref[...] = src_ref[...].astype(dst_ref.dtype)

    slot = j % 2
    lt = jnp.maximum(j - 1, 0)

    @pl.when(j == 0)
    def _():
        for ref in (u_ref, xs_ref, hc_ref):
            ref[...] = jnp.zeros(ref.shape, ref.dtype)

    _interleave(
        _lru_stages(lt % tiles_per_seq == 0, u_ref.at[1 - slot], cw_ref, cb_ref, wrg_ref, brg_ref,
                    wig_ref, big_ref, lam_ref, gn_ref, ylru_ref, xs_ref, hc_ref,
                    ts=ts, width=width, slab=slab),
        _inproj_stages(x_ref, gm_ref, w_ref, u_ref.at[slot], qkv_ref, lru_cols=2 * width,
                       col_chunk=col_chunk))


def _cast_slab_rows(rows, max_slabs):
    step = 2 * SUBLANES
    for slab_rows in range(step, rows + 1, step):
        if rows % slab_rows == 0 and rows // slab_rows <= max_slabs:
            return slab_rows
    raise ValueError(f"no bf16 row slab for {rows} rows in {max_slabs} steps")


def _inproj_lru(x2, gain, w_in_bf16, conv_w, conv_b, wrg_bd, b_rg, wig_bd, b_ig, lam, gain_lru,
                later_weights, seq_len, ts):
    n, d = x2.shape
    cols = w_in_bf16.shape[1]
    width = conv_w.shape[1]
    n_tiles = n // ts
    slab_rows = [_cast_slab_rows(w.shape[0], min(n_tiles, CAST_STEPS)) for w in later_weights]
    cast_slabs = tuple(w.shape[0] // r for w, r in zip(later_weights, slab_rows))
    wslab = lambda w, r: pl.BlockSpec(
        (r, w.shape[1]), lambda j: (jnp.minimum(j, w.shape[0] // r - 1), 0))
    wspecs = [wslab(w, r) for w, r in zip(later_weights, slab_rows)]
    tile = lambda lag: (lambda j: (jnp.clip(j - lag, 0, n_tiles - 1), 0))
    vec = lambda w: pl.BlockSpec((1, w), lambda j: (0, 0))
    wspec = pl.BlockSpec(wrg_bd.shape, lambda j: (0, 0, 0))
    return pl.pallas_call(
        functools.partial(_inproj_lru_kernel, ts=ts, width=width, tiles_per_seq=seq_len // ts,
                          col_chunk=MXU_DIM, slab=LANES // 2, cast_slabs=cast_slabs),
        grid=(n_tiles + 1,),
        in_specs=[
            pl.BlockSpec((ts, d), tile(0)),
            vec(d),
            pl.BlockSpec((d, cols), lambda j: (0, 0), pipeline_mode=pl.Buffered(1)),
            pl.BlockSpec((CONV_W, width), lambda j: (0, 0)),
            vec(width), wspec, vec(width), wspec, vec(width), vec(width), vec(width),
            *wspecs,
        ],
        out_specs=[
            pl.BlockSpec((ts, cols - 2 * width), tile(0)),
            pl.BlockSpec((ts, width), tile(1)),
            *wspecs,
        ],
        out_shape=[
            jax.ShapeDtypeStruct((n, cols - 2 * width), BF16),
            jax.ShapeDtypeStruct((n, width), BF16),
            *[jax.ShapeDtypeStruct(w.shape, BF16) for w in later_weights],
        ],
        scratch_shapes=[
            pltpu.VMEM((2, ts, 2 * width), F32),
            pltpu.VMEM((ts + SUBLANES, width), F32),
            pltpu.VMEM((SUBLANES, width), F32),
        ],
        compiler_params=pltpu.CompilerParams(
            dimension_semantics=("arbitrary",), vmem_limit_bytes=VMEM_LIMIT_BYTES),
        name="inproj_lru",
    )(x2, gain, w_in_bf16, conv_w, conv_b, wrg_bd, b_rg, wig_bd, b_ig, lam, gain_lru,
      *later_weights)


def _block_diag(w, per):
    nb, bw, _ = w.shape
    w = w.reshape(nb // per, per, bw, bw)
    eye = jnp.eye(per, dtype=w.dtype)
    return jnp.einsum('gpij,pq->gpiqj', w, eye).reshape(nb // per, per * bw, per * bw)


def _ffn_stages(x_ref, yl_ref, ya, ga_ref, wo_ref, gf_ref, wi_ref, wd_ref, gl_ref, out_ref, *,
                d_ff, ff_chunk, final_norm):
    ya = ((ya * _rms_scale(ya)) * ga_ref[...]).astype(BF16)
    wl = yl_ref.shape[1]
    x1 = x_ref[...] + jnp.dot(yl_ref[...], wo_ref[0:wl, :], preferred_element_type=F32)
    x1 = x1 + jnp.dot(ya, wo_ref[wl:, :], preferred_element_type=F32)
    h2 = ((x1 * _rms_scale(x1)) * gf_ref[...]).astype(BF16)
    yield
    acts = []
    for c in range(d_ff // ff_chunk):
        lo = c * ff_chunk
        gate = jnp.dot(h2, wi_ref[:, lo:lo + ff_chunk], preferred_element_type=F32)
        up = jnp.dot(h2, wi_ref[:, d_ff + lo:d_ff + lo + ff_chunk], preferred_element_type=F32)
        acts.append(((gate * _sigmoid(gate)) * up).astype(BF16))
        yield
    out = x1 + jnp.dot(jnp.concatenate(acts, axis=1), wd_ref[...], preferred_element_type=F32)
    yield
    if final_norm:
        out = (out * _rms_scale(out)) * gl_ref[...]
    out_ref[...] = out


def _attn_ffn_kernel(q_ref, kc_ref, kp_ref, vc_ref, vp_ref, qkv_hbm, x_ref, yl_ref, ga_ref,
                     wo_ref, gf_ref, wi_ref, wd_ref, gl_ref, out_ref,
                     yatt_ref, r_ref, acc_ref, kblk_ref, vblk_ref, sem,
                     *, tq, tk, nleft, scale2, tiles_per_seq, n_tiles, att_w, d_ff, ff_chunk,
                     final_norm):
    j = pl.program_id(0)
    ja = jnp.minimum(j, n_tiles - 1)
    bi = ja // tiles_per_seq
    ti = ja % tiles_per_seq
    nrb = tq // tk
    g0 = ti * nrb
    npair = att_w // LANES
    slot = j % 2

    @pl.when(j == 0)
    def _():
        yatt_ref[...] = jnp.zeros(yatt_ref.shape, F32)

    mixer = _ffn_stages(x_ref, yl_ref, yatt_ref[1 - slot], ga_ref, wo_ref, gf_ref, wi_ref, wd_ref,
                        gl_ref, out_ref, d_ff=d_ff, ff_chunk=ff_chunk, final_norm=final_norm)

    lane_q = lax.broadcasted_iota(jnp.int32, (tq, LANES), 1)
    rr = lax.broadcasted_iota(jnp.int32, (2 * tk, 2 * tk), 0)
    cc = lax.broadcasted_iota(jnp.int32, (2 * tk, 2 * tk), 1)
    key_j = jnp.where(rr >= tk, rr - tk, rr)
    tri = jnp.where((cc >= tk) | (key_j > cc), 1.0, 0.0).astype(BF16)
    below_diag = (lax.broadcasted_iota(jnp.int32, (tk, tk), 1)
                  < lax.broadcasted_iota(jnp.int32, (tk, tk), 0))
    has_left = ti > 0

    def softplus_parts(z):
        lb = jnp.minimum(z, 0.0) - jnp.log(1.0 + jnp.exp2(-jnp.abs(z))) * LOG2_E
        return lb, lb - z

    def suffix_and_total(l):
        hi = l.astype(BF16)
        lo = (l - hi.astype(F32)).astype(BF16)
        cs = jnp.dot(jnp.concatenate([hi, lo], axis=1), tri, preferred_element_type=F32)
        return cs[:, :tk], cs[:, tk:]

    def head_queries(g):
        q = (q_ref[0, :, g * LANES:(g + 1) * LANES].astype(F32) * scale2).astype(BF16)
        return (jnp.where(lane_q < HEAD_DIM, q, jnp.zeros_like(q)),
                jnp.where(lane_q >= HEAD_DIM, q, jnp.zeros_like(q)))

    def scores(qh, k_blk):
        return lax.dot_general(qh, k_blk, (((1,), (1,)), ((), ())), preferred_element_type=F32)

    def attention_stages():
        for g in range(npair):
            cols = slice(g * LANES, (g + 1) * LANES)
            q_heads = head_queries(g)
            k_all = jnp.concatenate([kp_ref[0, :, cols], kc_ref[0, :, cols]], axis=0)
            v_all = jnp.concatenate([vp_ref[0, :, cols], vc_ref[0, :, cols]], axis=0)
            chains = []
            for rb in range(nrb):
                rows = slice(rb * tk, (rb + 1) * tk)
                win = slice(rb * tk, (rb + nleft + 1) * tk)
                for h in range(2):
                    chains.append(dict(h=2 * g + h, rb=rb, rows=rows, v=v_all[win, :],
                                       z=scores(q_heads[h][rows, :], k_all[win, :])))
            yield
            for c in chains:
                c["lb"], c["parts"] = [], []
                for kb in range(nleft + 1):
                    zb = c["z"][:, kb * tk:(kb + 1) * tk]
                    if kb == nleft:
                        zb = jnp.where(below_diag, zb, -MASKED_SCORE)
                    elif c["rb"] + kb < nleft:
                        zb = jnp.where(has_left, zb, -MASKED_SCORE)
                    lb, l = softplus_parts(zb)
                    c["lb"].append(lb)
                    c["parts"].append(suffix_and_total(l))
            yield
            for c in chains:
                r = jnp.zeros((tk, tk), F32)
                atts = [None] * (nleft + 1)
                for kb in reversed(range(nleft + 1)):
                    suffix, total = c["parts"][kb]
                    atts[kb] = jnp.exp2(c["lb"][kb] + suffix + r).astype(BF16)
                    r = r + total
                c["att"] = jnp.concatenate(atts, axis=1)
                r_ref[c["h"], c["rows"], :] = r
            yield
            for c in chains:
                acc_ref[c["h"], c["rows"], :] = jnp.dot(c["att"], c["v"],
                                                        preferred_element_type=F32)
            yield

    _interleave(attention_stages(), mixer)

    def cond(carry):
        s, alive = carry
        return jnp.logical_and(s <= g0 + (nrb - 1 - nleft - 1), alive)

    def body(carry):
        s, _ = carry
        for g in range(npair):
            q_heads = head_queries(g)
            for rb in range(nrb):
                jb = g0 + (rb - nleft - 1) - s

                @pl.when(jb >= 0)
                def _():
                    rows = slice(rb * tk, (rb + 1) * tk)
                    ks = pl.multiple_of(jb * tk, tk)
                    copies = [
                        pltpu.make_async_copy(
                            qkv_hbm.at[bi, pl.ds(ks, tk), pl.ds((1 + p) * att_w + g * LANES, LANES)],
                            dst, sem.at[p])
                        for p, dst in enumerate((kblk_ref, vblk_ref))]
                    for cp in copies:
                        cp.start()
                    for cp in copies:
                        cp.wait()
                    for h in range(2):
                        lb, l = softplus_parts(scores(q_heads[h][rows, :], kblk_ref[...]))
                        suffix, total = suffix_and_total(l)
                        r_old = r_ref[2 * g + h, rows, :]
                        att = jnp.exp2(lb + suffix + r_old).astype(BF16)
                        acc_ref[2 * g + h, rows, :] += jnp.dot(att, vblk_ref[...],
                                                               preferred_element_type=F32)
                        r_ref[2 * g + h, rows, :] = r_old + total
        return s + 1, jnp.max(r_ref[...]) >= ATT_DEAD_LOG2

    lax.while_loop(cond, body, (0, jnp.max(r_ref[...]) >= ATT_DEAD_LOG2))

    for g in range(npair):
        yatt_ref[slot, :, g * LANES:(g + 1) * LANES] = jnp.where(
            lane_q < HEAD_DIM, acc_ref[2 * g], acc_ref[2 * g + 1])


def _attn_ffn(qkv, x2, y_lru, g_att, w_out, g_ffn, w_ffn_in, w_ffn_out, g_last, tq, tk, nleft,
              ff_chunk, final_norm):
    bsz, s, three_w = qkv.shape
    att_w = three_w // 3
    n, d = x2.shape
    d_ff = w_ffn_out.shape[0]
    tiles_per_seq = s // tq
    n_tiles = bsz * tiles_per_seq
    left = nleft * tk
    assert tq % tk == 0 and tq // tk >= nleft and tq % left == 0 and att_w % LANES == 0

    def tile(j):
        ja = jnp.minimum(j, n_tiles - 1)
        return ja // tiles_per_seq, ja % tiles_per_seq

    def cur(part):
        return pl.BlockSpec((1, tq, att_w), lambda j: (*tile(j), part))

    def prev(part):
        def index(j):
            b, t = tile(j)
            return b, jnp.maximum(t * (tq // left) - 1, 0), part
        return pl.BlockSpec((1, left, att_w), index)

    row = lambda w: pl.BlockSpec((tq, w), lambda j: (jnp.maximum(j - 1, 0), 0))
    vec = lambda w: pl.BlockSpec((1, w), lambda j: (0, 0))
    whole = lambda a: pl.BlockSpec(a.shape, lambda j: (0, 0), pipeline_mode=pl.Buffered(1))
    nheads = att_w // HEAD_DIM
    return pl.pallas_call(
        functools.partial(_attn_ffn_kernel, tq=tq, tk=tk, nleft=nleft,
                          scale2=LOG2_E / math.sqrt(HEAD_DIM), tiles_per_seq=tiles_per_seq,
                          n_tiles=n_tiles, att_w=att_w, d_ff=d_ff, ff_chunk=ff_chunk,
                          final_norm=final_norm),
        grid=(n_tiles + 1,),
        in_specs=[cur(0), cur(1), prev(1), cur(2), prev(2),
                  pl.BlockSpec(memory_space=pl.ANY),
                  row(d), row(y_lru.shape[1]), vec(att_w), whole(w_out), vec(d),
                  whole(w_ffn_in), whole(w_ffn_out), vec(d)],
        out_specs=row(d),
        out_shape=jax.ShapeDtypeStruct((n, d), F32),
        scratch_shapes=[
            pltpu.VMEM((2, tq, att_w), F32),
            pltpu.VMEM((nheads, tq, tk), F32),
            pltpu.VMEM((nheads, tq, LANES), F32),
            pltpu.VMEM((tk, LANES), BF16),
            pltpu.VMEM((tk, LANES), BF16),
            pltpu.SemaphoreType.DMA((2,)),
        ],
        compiler_params=pltpu.CompilerParams(
            dimension_semantics=("arbitrary",), vmem_limit_bytes=VMEM_LIMIT_BYTES),
        name="attn_ffn",
    )(qkv, qkv, qkv, qkv, qkv, qkv, x2, y_lru, g_att, w_out, g_ffn, w_ffn_in, w_ffn_out, g_last)


def _largest_chunk(total, limit):
    best = LANES
    for c in range(LANES, limit + 1, LANES):
        if total % c == 0:
            best = c
    return best


def kernel(x, norm_mix, w_in, conv_w, conv_b, w_rg, b_rg, w_ig, b_ig, lru_lambda, norm_lru_out,
           norm_att_out, w_out, norm_ffn, w_ffn_in, w_ffn_out, norm_final):
    bsz, s, d = x.shape
    depth = w_in.shape[0]
    lru_w = conv_w.shape[2]
    att_w = norm_att_out.shape[1]
    nblk, blk_w = w_rg.shape[1], w_rg.shape[2]
    assert w_in.shape[2] == 2 * lru_w + 3 * att_w and att_w % LANES == 0
    assert MXU_DIM % blk_w == 0 and lru_w % MXU_DIM == 0

    n = bsz * s
    ts = 512
    tq, tk, nleft = 512, 128, 2
    per = MXU_DIM // blk_w
    row1 = lambda v: v.reshape(1, -1)

    x2 = x.reshape(n, d)
    for l in range(depth):
        qkv, y_lru, w_out_b, w_ffn_in_b, w_ffn_out_b = _inproj_lru(
            x2, row1(norm_mix[l]), w_in[l].astype(BF16), conv_w[l], row1(conv_b[l]),
            _block_diag(w_rg[l], per).astype(BF16), row1(b_rg[l]),
            _block_diag(w_ig[l], per).astype(BF16), row1(b_ig[l]), row1(lru_lambda[l]),
            row1(norm_lru_out[l]), (w_out[l], w_ffn_in[l], w_ffn_out[l]), s, ts)
        d_ff = w_ffn_out.shape[1]
        x2 = _attn_ffn(qkv.reshape(bsz, s, 3 * att_w), x2, y_lru, row1(norm_att_out[l]), w_out_b,
                       row1(norm_ffn[l]), w_ffn_in_b, w_ffn_out_b, row1(norm_final),
                       tq, tk, nleft, _largest_chunk(d_ff, 512), l == depth - 1)
    return x2.reshape(bsz, s, d)
```

```python
import functools
import math

import jax
import jax.numpy as jnp
from jax import lax
from jax.experimental import pallas as pl
from jax.experimental.pallas import tpu as pltpu

F32 = jnp.float32
BF16 = jnp.bfloat16

EPS = 1e-6
LRU_C = 8.0
CONV_W = 4
HEAD_DIM = 64

LANES = 128
SUBLANES = 8
MXU_DIM = 256
VMEM_LIMIT_BYTES = 56 * 1024 * 1024

LOG2_E = 1.4426950408889634
ATT_DEAD_LOG2 = -160.0
MASKED_SCORE = 1e30
CAST_STEPS = 8


def _rms_scale(x):
    return lax.rsqrt(jnp.mean(x * x, axis=-1, keepdims=True) + EPS)


def _sigmoid(x):
    return 1.0 / (1.0 + jnp.exp(-x))


def _softplus(x):
    return jnp.maximum(x, 0.0) + jnp.log(1.0 + jnp.exp(-jnp.abs(x)))


def _gelu_tanh(x):
    c = math.sqrt(2.0 / math.pi)
    return 0.5 * x * (1.0 + jnp.tanh(c * (x + 0.044715 * (x * x * x))))


def _inproj_stages(x_ref, g_ref, w_ref, u_slot_ref, qkv_ref, *, lru_cols, col_chunk):
    x = x_ref[...]
    h = ((x * _rms_scale(x)) * g_ref[...]).astype(BF16)
    yield
    for lo in range(0, w_ref.shape[1], col_chunk):
        u = jnp.dot(h, w_ref[:, lo:lo + col_chunk], preferred_element_type=F32)
        if lo < lru_cols:
            u_slot_ref[:, lo:lo + col_chunk] = u
        else:
            qkv_ref[:, lo - lru_cols:lo - lru_cols + col_chunk] = u.astype(BF16)
        yield


def _lru_stages(first, u_ref, cw_ref, cb_ref, wrg_ref, brg_ref, wig_ref, big_ref, lam_ref, gn_ref,
                out_ref, xs_ref, hc_ref, *, ts, width, slab):
    halo = SUBLANES
    xs_ref[0:halo, :] = jnp.where(first, 0.0, xs_ref[ts:ts + halo, :])
    xs_ref[halo:halo + ts, :] = u_ref[:, 0:width]
    row = lax.broadcasted_iota(jnp.int32, (SUBLANES, width), 0)
    hprev = jnp.where(first, 0.0, hc_ref[...])
    neg_sp = -LRU_C * _softplus(-lam_ref[...])

    for s0 in range(0, ts, slab):
        xc = jnp.broadcast_to(cb_ref[...], (slab, width))
        for i in range(CONV_W):
            off = s0 + halo - (CONV_W - 1) + i
            xc = xc + xs_ref[off:off + slab, :] * cw_ref[i:i + 1, :]

        xcb = xc.astype(BF16)
        rg, ig = [], []
        for c in range(width // MXU_DIM):
            sl = slice(c * MXU_DIM, (c + 1) * MXU_DIM)
            rg.append(jnp.dot(xcb[:, sl], wrg_ref[c], preferred_element_type=F32))
            ig.append(jnp.dot(xcb[:, sl], wig_ref[c], preferred_element_type=F32))
        r = _sigmoid(jnp.concatenate(rg, axis=1) + brg_ref[...])
        ig = _sigmoid(jnp.concatenate(ig, axis=1) + big_ref[...])

        log_a = r * neg_sp
        a = jnp.exp(log_a)
        th = jnp.minimum(jnp.tanh(log_a), 0.0)
        b = jnp.sqrt((-2.0 * th) / (1.0 - th)) * (ig * xc)
        yield

        hs = []
        for g in range(slab // SUBLANES):
            rows = slice(g * SUBLANES, (g + 1) * SUBLANES)
            a8, b8 = a[rows, :], b[rows, :]
            for d in (1, 2, 4):
                keep = row >= d
                b8 = jnp.where(keep, b8 + a8 * pltpu.roll(b8, d, axis=0), b8)
                a8 = jnp.where(keep, a8 * pltpu.roll(a8, d, axis=0), a8)
            h8 = b8 + a8 * hprev
            hs.append(h8)
            hprev = jnp.broadcast_to(h8[SUBLANES - 1:SUBLANES, :], (SUBLANES, width))
        y = _gelu_tanh(u_ref[s0:s0 + slab, width:2 * width]) * jnp.concatenate(hs, axis=0)
        out_ref[s0:s0 + slab, :] = ((y * _rms_scale(y)) * gn_ref[...]).astype(out_ref.dtype)
        yield
    hc_ref[...] = hprev


def _interleave(*streams):
    streams = list(streams)
    while streams:
        for st in list(streams):
            if next(st, StopIteration) is StopIteration:
                streams.remove(st)


def _inproj_lru_kernel(x_ref, gm_ref, w_ref, cw_ref, cb_ref, wrg_ref, brg_ref, wig_ref, big_ref,
                       lam_ref, gn_ref, *rest, ts, width, tiles_per_seq, col_chunk, slab,
                       cast_slabs):
    nw = len(cast_slabs)
    wsrc_refs, (qkv_ref, ylru_ref) = rest[:nw], rest[nw:nw + 2]
    wdst_refs = rest[nw + 2:2 * nw + 2]
    u_ref, xs_ref, hc_ref = rest[2 * nw + 2:]
    j = pl.program_id(0)
    for src_ref, dst_ref, n_slabs in zip(wsrc_refs, wdst_refs, cast_slabs):
        @pl.when(j < n_slabs)
        def _():
            dst_ref[...] = src_ref[...].astype(dst_ref.dtype)

    slot = j % 2
    lt = jnp.maximum(j - 1, 0)

    @pl.when(j == 0)
    def _():
        for ref in (u_ref, xs_ref, hc_ref):
            ref[...] = jnp.zeros(ref.shape, ref.dtype)

    _interleave(
        _lru_stages(lt % tiles_per_seq == 0, u_ref.at[1 - slot], cw_ref, cb_ref, wrg_ref, brg_ref,
                    wig_ref, big_ref, lam_ref, gn_ref, ylru_ref, xs_ref, hc_ref,
                    ts=ts, width=width, slab=slab),
        _inproj_stages(x_ref, gm_ref, w_ref, u_ref.at[slot], qkv_ref, lru_cols=2 * width,
                       col_chunk=col_chunk))


def _cast_slab_rows(rows, max_slabs):
    step = 2 * SUBLANES
    for slab_rows in range(step, rows + 1, step):
        if rows % slab_rows == 0 and rows // slab_rows <= max_slabs:
            return slab_rows
    raise ValueError(f"no bf16 row slab for {rows} rows in {max_slabs} steps")


def _inproj_lru(x2, gain, w_in_bf16, conv_w, conv_b, wrg_bd, b_rg, wig_bd, b_ig, lam, gain_lru,
                later_weights, seq_len, ts):
    n, d = x2.shape
    cols = w_in_bf16.shape[1]
    width = conv_w.shape[1]
    n_tiles = n // ts
    slab_rows = [_cast_slab_rows(w.shape[0], min(n_tiles, CAST_STEPS)) for w in later_weights]
    cast_slabs = tuple(w.shape[0] // r for w, r in zip(later_weights, slab_rows))
    wslab = lambda w, r: pl.BlockSpec(
        (r, w.shape[1]), lambda j: (jnp.minimum(j, w.shape[0] // r - 1), 0))
    wspecs = [wslab(w, r) for w, r in zip(later_weights, slab_rows)]
    tile = lambda lag: (lambda j: (jnp.clip(j - lag, 0, n_tiles - 1), 0))
    vec = lambda w: pl.BlockSpec((1, w), lambda j: (0, 0))
    wspec = pl.BlockSpec(wrg_bd.shape, lambda j: (0, 0, 0))
    return pl.pallas_call(
        functools.partial(_inproj_lru_kernel, ts=ts, width=width, tiles_per_seq=seq_len // ts,
                          col_chunk=MXU_DIM, slab=LANES // 2, cast_slabs=cast_slabs),
        grid=(n_tiles + 1,),
        in_specs=[
            pl.BlockSpec((ts, d), tile(0)),
            vec(d),
            pl.BlockSpec((d, cols), lambda j: (0, 0), pipeline_mode=pl.Buffered(1)),
            pl.BlockSpec((CONV_W, width), lambda j: (0, 0)),
            vec(width), wspec, vec(width), wspec, vec(width), vec(width), vec(width),
            *wspecs,
        ],
        out_specs=[
            pl.BlockSpec((ts, cols - 2 * width), tile(0)),
            pl.BlockSpec((ts, width), tile(1)),
            *wspecs,
        ],
        out_shape=[
            jax.ShapeDtypeStruct((n, cols - 2 * width), BF16),
            jax.ShapeDtypeStruct((n, width), BF16),
            *[jax.ShapeDtypeStruct(w.shape, BF16) for w in later_weights],
        ],
        scratch_shapes=[
            pltpu.VMEM((2, ts, 2 * width), F32),
            pltpu.VMEM((ts + SUBLANES, width), F32),
            pltpu.VMEM((SUBLANES, width), F32),
        ],
        compiler_params=pltpu.CompilerParams(
            dimension_semantics=("arbitrary",), vmem_limit_bytes=VMEM_LIMIT_BYTES),
        name="inproj_lru",
    )(x2, gain, w_in_bf16, conv_w, conv_b, wrg_bd, b_rg, wig_bd, b_ig, lam, gain_lru,
      *later_weights)


def _block_diag(w, per):
    nb, bw, _ = w.shape
    w = w.reshape(nb // per, per, bw, bw)
    eye = jnp.eye(per, dtype=w.dtype)
    return jnp.einsum('gpij,pq->gpiqj', w, eye).reshape(nb // per, per * bw, per * bw)


def _ffn_stages(x_ref, yl_ref, ya, ga_ref, wo_ref, gf_ref, wi_ref, wd_ref, gl_ref, out_ref, *,
                d_ff, ff_chunk, final_norm):
    ya = ((ya * _rms_scale(ya)) * ga_ref[...]).astype(BF16)
    wl = yl_ref.shape[1]
    x1 = x_ref[...] + jnp.dot(yl_ref[...], wo_ref[0:wl, :], preferred_element_type=F32)
    x1 = x1 + jnp.dot(ya, wo_ref[wl:, :], preferred_element_type=F32)
    h2 = ((x1 * _rms_scale(x1)) * gf_ref[...]).astype(BF16)
    yield
    acts = []
    for c in range(d_ff // ff_chunk):
        lo = c * ff_chunk
        gate = jnp.dot(h2, wi_ref[:, lo:lo + ff_chunk], preferred_element_type=F32)
        up = jnp.dot(h2, wi_ref[:, d_ff + lo:d_ff + lo + ff_chunk], preferred_element_type=F32)
        acts.append(((gate * _sigmoid(gate)) * up).astype(BF16))
        yield
    out = x1 + jnp.dot(jnp.concatenate(acts, axis=1), wd_ref[...], preferred_element_type=F32)
    yield
    if final_norm:
        out = (out * _rms_scale(out)) * gl_ref[...]
    out_ref[...] = out


def _attn_ffn_kernel(q_ref, kc_ref, kp_ref, vc_ref, vp_ref, qkv_hbm, x_ref, yl_ref, ga_ref,
                     wo_ref, gf_ref, wi_ref, wd_ref, gl_ref, out_ref,
                     yatt_ref, r_ref, acc_ref, kblk_ref, vblk_ref, sem,
                     *, tq, tk, nleft, scale2, tiles_per_seq, n_tiles, att_w, d_ff, ff_chunk,
                     final_norm):
    j = pl.program_id(0)
    ja = jnp.minimum(j, n_tiles - 1)
    bi = ja // tiles_per_seq
    ti = ja % tiles_per_seq
    npair = att_w // LANES
    slot = j % 2

    @pl.when(j == 0)
    def _():
        yatt_ref[...] = jnp.zeros(yatt_ref.shape, F32)

    mixer = _ffn_stages(x_ref, yl_ref, yatt_ref[1 - slot], ga_ref, wo_ref, gf_ref, wi_ref, wd_ref,
                        gl_ref, out_ref, d_ff=d_ff, ff_chunk=ff_chunk, final_norm=final_norm)

    lane_q = lax.broadcasted_iota(jnp.int32, (tq, LANES), 1)
    rr = lax.broadcasted_iota(jnp.int32, (2 * tk, 2 * tk), 0)
    cc = lax.broadcasted_iota(jnp.int32, (2 * tk, 2 * tk), 1)
    key_j = jnp.where(rr >= tk, rr - tk, rr)
    tri = jnp.where((cc >= tk) | (key_j > cc), 1.0, 0.0).astype(BF16)
    half = tk // 2
    key_lane = lax.broadcasted_iota(jnp.int32, (half, tk), 1)
    causal = key_lane < lax.broadcasted_iota(jnp.int32, (half, tk), 0) + half
    has_left = ti > 0

    def softplus_parts(z):
        lb = jnp.minimum(z, 0.0) - jnp.log(1.0 + jnp.exp2(-jnp.abs(z))) * LOG2_E
        return lb, lb - z

    def suffix_and_total(l):
        hi = l.astype(BF16)
        lo = (l - hi.astype(F32)).astype(BF16)
        cs = jnp.dot(jnp.concatenate([hi, lo], axis=1), tri, preferred_element_type=F32)
        return cs[:, :tk], cs[:, tk:]

    def head_queries(g):
        q = (q_ref[0, :, g * LANES:(g + 1) * LANES].astype(F32) * scale2).astype(BF16)
        return (jnp.where(lane_q < HEAD_DIM, q, jnp.zeros_like(q)),
                jnp.where(lane_q >= HEAD_DIM, q, jnp.zeros_like(q)))

    def scores(qh, k_blk):
        return lax.dot_general(qh, k_blk, (((1,), (1,)), ((), ())), preferred_element_type=F32)

    def chain_window(c):
        rows = slice(c * half, (c + 1) * half)
        first = nleft * tk + (c + 1) * half - 2 * tk
        return rows, slice(first, first + 2 * tk)

    def attention_stages():
        for g in range(npair):
            cols = slice(g * LANES, (g + 1) * LANES)
            q_heads = head_queries(g)
            k_all = jnp.concatenate([kp_ref[0, :, cols], kc_ref[0, :, cols]], axis=0)
            v_all = jnp.concatenate([vp_ref[0, :, cols], vc_ref[0, :, cols]], axis=0)
            chains = []
            for c in range(tq // half):
                rows, win = chain_window(c)
                for h in range(2):
                    chains.append(dict(h=2 * g + h, c=c, rows=rows, v=v_all[win, :],
                                       z=scores(q_heads[h][rows, :], k_all[win, :])))
            yield
            for ch in chains:
                ch["lb"], ch["parts"] = [], []
                for kb in range(2):
                    zb = ch["z"][:, kb * tk:(kb + 1) * tk]
                    if kb == 1:
                        zb = jnp.where(causal, zb, -MASKED_SCORE)
                    n_left = min(max((2 - kb) * tk - (ch["c"] + 1) * half, 0), tk)
                    if n_left:
                        zb = jnp.where(jnp.logical_or(has_left, key_lane >= n_left), zb,
                                       -MASKED_SCORE)
                    lb, l = softplus_parts(zb)
                    ch["lb"].append(lb)
                    ch["parts"].append(suffix_and_total(l))
            yield
            for ch in chains:
                (suffix0, total0), (suffix1, total1) = ch["parts"]
                att = jnp.concatenate(
                    [jnp.exp2(ch["lb"][0] + suffix0 + total1).astype(BF16),
                     jnp.exp2(ch["lb"][1] + suffix1).astype(BF16)], axis=1)
                ch["att"] = att
                r_ref[ch["h"], ch["rows"], :] = total0 + total1
            yield
            for ch in chains:
                acc_ref[ch["h"], ch["rows"], :] = jnp.dot(ch["att"], ch["v"],
                                                          preferred_element_type=F32)
            yield

    _interleave(attention_stages(), mixer)

    tile_start = ti * tq

    def cond(carry):
        s, alive = carry
        return jnp.logical_and(tile_start + tq - 2 * tk - s * tk > 0, alive)

    def body(carry):
        s, _ = carry
        for g in range(npair):
            q_heads = head_queries(g)
            for c in range(tq // half):
                end = tile_start + ((c + 1) * half - 2 * tk) - s * tk

                @pl.when(end > 0)
                def _():
                    rows = slice(c * half, (c + 1) * half)
                    ks = pl.multiple_of(jnp.maximum(end - tk, 0), half)
                    copies = [
                        pltpu.make_async_copy(
                            qkv_hbm.at[bi, pl.ds(ks, tk), pl.ds((1 + p) * att_w + g * LANES, LANES)],
                            dst, sem.at[p])
                        for p, dst in enumerate((kblk_ref, vblk_ref))]
                    for cp in copies:
                        cp.start()
                    for cp in copies:
                        cp.wait()
                    uncovered = key_lane < end - ks
                    for h in range(2):
                        z = scores(q_heads[h][rows, :], kblk_ref[...])
                        lb, l = softplus_parts(jnp.where(uncovered, z, -MASKED_SCORE))
                        suffix, total = suffix_and_total(l)
                        r_old = r_ref[2 * g + h, rows, :]
                        att = jnp.exp2(lb + suffix + r_old).astype(BF16)
                        acc_ref[2 * g + h, rows, :] += jnp.dot(att, vblk_ref[...],
                                                               preferred_element_type=F32)
                        r_ref[2 * g + h, rows, :] = r_old + total
        return s + 1, jnp.max(r_ref[...]) >= ATT_DEAD_LOG2

    lax.while_loop(cond, body, (0, jnp.max(r_ref[...]) >= ATT_DEAD_LOG2))

    for g in range(npair):
        yatt_ref[slot, :, g * LANES:(g + 1) * LANES] = jnp.where(
            lane_q < HEAD_DIM, acc_ref[2 * g], acc_ref[2 * g + 1])


def _attn_ffn(qkv, x2, y_lru, g_att, w_out, g_ffn, w_ffn_in, w_ffn_out, g_last, tq, tk, nleft,
              ff_chunk, final_norm):
    bsz, s, three_w = qkv.shape
    att_w = three_w // 3
    n, d = x2.shape
    d_ff = w_ffn_out.shape[0]
    tiles_per_seq = s // tq
    n_tiles = bsz * tiles_per_seq
    left = nleft * tk
    assert s % tq == 0 and tq % tk == 0 and nleft == 2 and tq % left == 0 and att_w % LANES == 0

    def tile(j):
        ja = jnp.minimum(j, n_tiles - 1)
        return ja // tiles_per_seq, ja % tiles_per_seq

    def cur(part):
        return pl.BlockSpec((1, tq, att_w), lambda j: (*tile(j), part))

    def prev(part):
        def index(j):
            b, t = tile(j)
            return b, jnp.maximum(t * (tq // left) - 1, 0), part
        return pl.BlockSpec((1, left, att_w), index)

    row = lambda w: pl.BlockSpec((tq, w), lambda j: (jnp.maximum(j - 1, 0), 0))
    vec = lambda w: pl.BlockSpec((1, w), lambda j: (0, 0))
    whole = lambda a: pl.BlockSpec(a.shape, lambda j: (0, 0), pipeline_mode=pl.Buffered(1))
    nheads = att_w // HEAD_DIM
    return pl.pallas_call(
        functools.partial(_attn_ffn_kernel, tq=tq, tk=tk, nleft=nleft,
                          scale2=LOG2_E / math.sqrt(HEAD_DIM), tiles_per_seq=tiles_per_seq,
                          n_tiles=n_tiles, att_w=att_w, d_ff=d_ff, ff_chunk=ff_chunk,
                          final_norm=final_norm),
        grid=(n_tiles + 1,),
        in_specs=[cur(0), cur(1), prev(1), cur(2), prev(2),
                  pl.BlockSpec(memory_space=pl.ANY),
                  row(d), row(y_lru.shape[1]), vec(att_w), whole(w_out), vec(d),
                  whole(w_ffn_in), whole(w_ffn_out), vec(d)],
        out_specs=row(d),
        out_shape=jax.ShapeDtypeStruct((n, d), F32),
        scratch_shapes=[
            pltpu.VMEM((2, tq, att_w), F32),
            pltpu.VMEM((nheads, tq, tk), F32),
            pltpu.VMEM((nheads, tq, LANES), F32),
            pltpu.VMEM((tk, LANES), BF16),
            pltpu.VMEM((tk, LANES), BF16),
            pltpu.SemaphoreType.DMA((2,)),
        ],
        compiler_params=pltpu.CompilerParams(
            dimension_semantics=("arbitrary",), vmem_limit_bytes=VMEM_LIMIT_BYTES),
        name="attn_ffn",
    )(qkv, qkv, qkv, qkv, qkv, qkv, x2, y_lru, g_att, w_out, g_ffn, w_ffn_in, w_ffn_out, g_last)


def _largest_chunk(total, limit):
    best = LANES
    for c in range(LANES, limit + 1, LANES):
        if total % c == 0:
            best = c
    return best


def kernel(x, norm_mix, w_in, conv_w, conv_b, w_rg, b_rg, w_ig, b_ig, lru_lambda, norm_lru_out,
           norm_att_out, w_out, norm_ffn, w_ffn_in, w_ffn_out, norm_final):
    bsz, s, d = x.shape
    depth = w_in.shape[0]
    lru_w = conv_w.shape[2]
    att_w = norm_att_out.shape[1]
    blk_w = w_rg.shape[2]
    assert w_in.shape[2] == 2 * lru_w + 3 * att_w and att_w % LANES == 0
    assert MXU_DIM % blk_w == 0 and lru_w % MXU_DIM == 0

    n = bsz * s
    ts = 512
    assert s % ts == 0
    tq, tk, nleft = 512, 128, 2
    per = MXU_DIM // blk_w
    row1 = lambda v: v.reshape(1, -1)

    x2 = x.reshape(n, d)
    for l in range(depth):
        qkv, y_lru, w_out_b, w_ffn_in_b, w_ffn_out_b = _inproj_lru(
            x2, row1(norm_mix[l]), w_in[l].astype(BF16), conv_w[l], row1(conv_b[l]),
            _block_diag(w_rg[l], per).astype(BF16), row1(b_rg[l]),
            _block_diag(w_ig[l], per).astype(BF16), row1(b_ig[l]), row1(lru_lambda[l]),
            row1(norm_lru_out[l]), (w_out[l], w_ffn_in[l], w_ffn_out[l]), s, ts)
        d_ff = w_ffn_out.shape[1]
        x2 = _attn_ffn(qkv.reshape(bsz, s, 3 * att_w), x2, y_lru, row1(norm_att_out[l]), w_out_b,
                       row1(norm_ffn[l]), w_ffn_in_b, w_ffn_out_b, row1(norm_final),
                       tq, tk, nleft, _largest_chunk(d_ff, 512), l == depth - 1)
    return x2.reshape(bsz, s, d)
```

```python
import functools
import math

import jax
import jax.numpy as jnp
from jax import lax
from jax.experimental import pallas as pl
from jax.experimental.pallas import tpu as pltpu

F32 = jnp.float32
BF16 = jnp.bfloat16

EPS = 1e-6
LRU_C = 8.0
CONV_W = 4
HEAD_DIM = 64

LANES = 128
SUBLANES = 8
MXU_DIM = 256
VMEM_LIMIT_BYTES = 56 * 1024 * 1024

LOG2_E = 1.4426950408889634
ATT_DEAD_LOG2 = -160.0
MASKED_SCORE = 1e30
CAST_STEPS = 8


def _rms_scale(x):
    return lax.rsqrt(jnp.mean(x * x, axis=-1, keepdims=True) + EPS)


def _sigmoid(x):
    return 1.0 / (1.0 + jnp.exp(-x))


def _softplus(x):
    return jnp.maximum(x, 0.0) + jnp.log(1.0 + jnp.exp(-jnp.abs(x)))


def _gelu_tanh(x):
    c = math.sqrt(2.0 / math.pi)
    return 0.5 * x * (1.0 + jnp.tanh(c * (x + 0.044715 * (x * x * x))))


def _inproj_stages(x_ref, g_ref, w_ref, u_slot_ref, qkv_ref, *, lru_cols, col_chunk):
    x = x_ref[...]
    h = ((x * _rms_scale(x)) * g_ref[...]).astype(BF16)
    yield
    for lo in range(0, w_ref.shape[1], col_chunk):
        u = jnp.dot(h, w_ref[:, lo:lo + col_chunk], preferred_element_type=F32)
        if lo < lru_cols:
            u_slot_ref[:, lo:lo + col_chunk] = u
        else:
            qkv_ref[:, lo - lru_cols:lo - lru_cols + col_chunk] = u.astype(BF16)
        yield


def _lru_stages(first, u_ref, cw_ref, cb_ref, wrg_ref, brg_ref, wig_ref, big_ref, lam_ref, gn_ref,
                out_ref, xs_ref, hc_ref, *, ts, width, slab):
    halo = SUBLANES
    xs_ref[0:halo, :] = jnp.where(first, 0.0, xs_ref[ts:ts + halo, :])
    xs_ref[halo:halo + ts, :] = u_ref[:, 0:width]
    row = lax.broadcasted_iota(jnp.int32, (SUBLANES, width), 0)
    hprev = jnp.where(first, 0.0, hc_ref[...])
    neg_sp = -LRU_C * _softplus(-lam_ref[...])

    for s0 in range(0, ts, slab):
        xc = jnp.broadcast_to(cb_ref[...], (slab, width))
        for i in range(CONV_W):
            off = s0 + halo - (CONV_W - 1) + i
            xc = xc + xs_ref[off:off + slab, :] * cw_ref[i:i + 1, :]

        xcb = xc.astype(BF16)
        rg, ig = [], []
        for c in range(width // MXU_DIM):
            sl = slice(c * MXU_DIM, (c + 1) * MXU_DIM)
            rg.append(jnp.dot(xcb[:, sl], wrg_ref[c], preferred_element_type=F32))
            ig.append(jnp.dot(xcb[:, sl], wig_ref[c], preferred_element_type=F32))
        r = _sigmoid(jnp.concatenate(rg, axis=1) + brg_ref[...])
        ig = _sigmoid(jnp.concatenate(ig, axis=1) + big_ref[...])

        log_a = r * neg_sp
        a = jnp.exp(log_a)
        th = jnp.minimum(jnp.tanh(log_a), 0.0)
        b = jnp.sqrt((-2.0 * th) / (1.0 - th)) * (ig * xc)
        yield

        hs = []
        for g in range(slab // SUBLANES):
            rows = slice(g * SUBLANES, (g + 1) * SUBLANES)
            a8, b8 = a[rows, :], b[rows, :]
            for d in (1, 2, 4):
                keep = row >= d
                b8 = jnp.where(keep, b8 + a8 * pltpu.roll(b8, d, axis=0), b8)
                a8 = jnp.where(keep, a8 * pltpu.roll(a8, d, axis=0), a8)
            h8 = b8 + a8 * hprev
            hs.append(h8)
            hprev = jnp.broadcast_to(h8[SUBLANES - 1:SUBLANES, :], (SUBLANES, width))
        y = _gelu_tanh(u_ref[s0:s0 + slab, width:2 * width]) * jnp.concatenate(hs, axis=0)
        out_ref[s0:s0 + slab, :] = ((y * _rms_scale(y)) * gn_ref[...]).astype(out_ref.dtype)
        yield
    hc_ref[...] = hprev


def _interleave(*streams):
    streams = list(streams)
    while streams:
        for st in list(streams):
            if next(st, StopIteration) is StopIteration:
                streams.remove(st)


def _inproj_lru_kernel(x_ref, gm_ref, w_ref, cw_ref, cb_ref, wrg_ref, brg_ref, wig_ref, big_ref,
                       lam_ref, gn_ref, *rest, ts, width, tiles_per_seq, col_chunk, slab,
                       cast_slabs):
    nw = len(cast_slabs)
    wsrc_refs, (qkv_ref, ylru_ref) = rest[:nw], rest[nw:nw + 2]
    wdst_refs = rest[nw + 2:2 * nw + 2]
    u_ref, xs_ref, hc_ref = rest[2 * nw + 2:]
    j = pl.program_id(0)
    for src_ref, dst_ref, n_slabs in zip(wsrc_refs, wdst_refs, cast_slabs):
        @pl.when(j < n_slabs)
        def _():
            dst_ref[...] = src_ref[...].astype(dst_ref.dtype)

    slot = j % 2
    lt = jnp.maximum(j - 1, 0)

    @pl.when(j == 0)
    def _():
        for ref in (u_ref, xs_ref, hc_ref):
            ref[...] = jnp.zeros(ref.shape, ref.dtype)

    _interleave(
        _lru_stages(lt % tiles_per_seq == 0, u_ref.at[1 - slot], cw_ref, cb_ref, wrg_ref, brg_ref,
                    wig_ref, big_ref, lam_ref, gn_ref, ylru_ref, xs_ref, hc_ref,
                    ts=ts, width=width, slab=slab),
        _inproj_stages(x_ref, gm_ref, w_ref, u_ref.at[slot], qkv_ref, lru_cols=2 * width,
                       col_chunk=col_chunk))


def _cast_slab_rows(rows, max_slabs):
    step = 2 * SUBLANES
    for slab_rows in range(step, rows + 1, step):
        if rows % slab_rows == 0 and rows // slab_rows <= max_slabs:
            return slab_rows
    raise ValueError(f"no bf16 row slab for {rows} rows in {max_slabs} steps")


def _inproj_lru(x2, gain, w_in_bf16, conv_w, conv_b, wrg_bd, b_rg, wig_bd, b_ig, lam, gain_lru,
                later_weights, seq_len, ts):
    n, d = x2.shape
    cols = w_in_bf16.shape[1]
    width = conv_w.shape[1]
    n_tiles = n // ts
    slab_rows = [_cast_slab_rows(w.shape[0], min(n_tiles, CAST_STEPS)) for w in later_weights]
    cast_slabs = tuple(w.shape[0] // r for w, r in zip(later_weights, slab_rows))
    wslab = lambda w, r: pl.BlockSpec(
        (r, w.shape[1]), lambda j: (jnp.minimum(j, w.shape[0] // r - 1), 0))
    wspecs = [wslab(w, r) for w, r in zip(later_weights, slab_rows)]
    tile = lambda lag: (lambda j: (jnp.clip(j - lag, 0, n_tiles - 1), 0))
    vec = lambda w: pl.BlockSpec((1, w), lambda j: (0, 0))
    wspec = pl.BlockSpec(wrg_bd.shape, lambda j: (0, 0, 0))
    return pl.pallas_call(
        functools.partial(_inproj_lru_kernel, ts=ts, width=width, tiles_per_seq=seq_len // ts,
                          col_chunk=MXU_DIM, slab=LANES // 2, cast_slabs=cast_slabs),
        grid=(n_tiles + 1,),
        in_specs=[
            pl.BlockSpec((ts, d), tile(0)),
            vec(d),
            pl.BlockSpec((d, cols), lambda j: (0, 0), pipeline_mode=pl.Buffered(1)),
            pl.BlockSpec((CONV_W, width), lambda j: (0, 0)),
            vec(width), wspec, vec(width), wspec, vec(width), vec(width), vec(width),
            *wspecs,
        ],
        out_specs=[
            pl.BlockSpec((ts, cols - 2 * width), tile(0)),
            pl.BlockSpec((ts, width), tile(1)),
            *wspecs,
        ],
        out_shape=[
            jax.ShapeDtypeStruct((n, cols - 2 * width), BF16),
            jax.ShapeDtypeStruct((n, width), BF16),
            *[jax.ShapeDtypeStruct(w.shape, BF16) for w in later_weights],
        ],
        scratch_shapes=[
            pltpu.VMEM((2, ts, 2 * width), F32),
            pltpu.VMEM((ts + SUBLANES, width), F32),
            pltpu.VMEM((SUBLANES, width), F32),
        ],
        compiler_params=pltpu.CompilerParams(
            dimension_semantics=("arbitrary",), vmem_limit_bytes=VMEM_LIMIT_BYTES),
        name="inproj_lru",
    )(x2, gain, w_in_bf16, conv_w, conv_b, wrg_bd, b_rg, wig_bd, b_ig, lam, gain_lru,
      *later_weights)


def _block_diag(w, per):
    nb, bw, _ = w.shape
    w = w.reshape(nb // per, per, bw, bw)
    eye = jnp.eye(per, dtype=w.dtype)
    return jnp.einsum('gpij,pq->gpiqj', w, eye).reshape(nb // per, per * bw, per * bw)


def _ffn_stages(x_ref, yl_ref, ya, ga_ref, wo_ref, gf_ref, wi_ref, wd_ref, gl_ref, out_ref, *,
                d_ff, ff_chunk, final_norm):
    ya = ((ya * _rms_scale(ya)) * ga_ref[...]).astype(BF16)
    wl = yl_ref.shape[1]
    x1 = x_ref[...] + jnp.dot(yl_ref[...], wo_ref[0:wl, :], preferred_element_type=F32)
    x1 = x1 + jnp.dot(ya, wo_ref[wl:, :], preferred_element_type=F32)
    h2 = ((x1 * _rms_scale(x1)) * gf_ref[...]).astype(BF16)
    yield
    acts = []
    for c in range(d_ff // ff_chunk):
        lo = c * ff_chunk
        gate = jnp.dot(h2, wi_ref[:, lo:lo + ff_chunk], preferred_element_type=F32)
        up = jnp.dot(h2, wi_ref[:, d_ff + lo:d_ff + lo + ff_chunk], preferred_element_type=F32)
        acts.append(((gate * _sigmoid(gate)) * up).astype(BF16))
        yield
    out = x1 + jnp.dot(jnp.concatenate(acts, axis=1), wd_ref[...], preferred_element_type=F32)
    yield
    if final_norm:
        out = (out * _rms_scale(out)) * gl_ref[...]
    out_ref[...] = out


def _attn_ffn_kernel(q_ref, kc_ref, kp_ref, vc_ref, vp_ref, qkv_hbm, x_ref, yl_ref, ga_ref,
                     wo_ref, gf_ref, wi_ref, wd_ref, gl_ref, out_ref,
                     yatt_ref, r_ref, acc_ref, kblk_ref, vblk_ref, sem,
                     *, tq, tk, nleft, scale2, tiles_per_seq, n_tiles, att_w, d_ff, ff_chunk,
                     final_norm):
    j = pl.program_id(0)
    ja = jnp.minimum(j, n_tiles - 1)
    bi = ja // tiles_per_seq
    ti = ja % tiles_per_seq
    npair = att_w // LANES
    slot = j % 2

    @pl.when(j == 0)
    def _():
        yatt_ref[...] = jnp.zeros(yatt_ref.shape, F32)

    mixer = _ffn_stages(x_ref, yl_ref, yatt_ref[1 - slot], ga_ref, wo_ref, gf_ref, wi_ref, wd_ref,
                        gl_ref, out_ref, d_ff=d_ff, ff_chunk=ff_chunk, final_norm=final_norm)

    lane_q = lax.broadcasted_iota(jnp.int32, (tq, LANES), 1)
    rr = lax.broadcasted_iota(jnp.int32, (2 * tk, 2 * tk), 0)
    cc = lax.broadcasted_iota(jnp.int32, (2 * tk, 2 * tk), 1)
    key_j = jnp.where(rr >= tk, rr - tk, rr)
    tri = jnp.where((cc >= tk) | (key_j > cc), 1.0, 0.0).astype(BF16)
    half = tk // 2
    key_lane = lax.broadcasted_iota(jnp.int32, (half, tk), 1)
    causal = key_lane < lax.broadcasted_iota(jnp.int32, (half, tk), 0) + half
    has_left = ti > 0

    def softplus_parts(z):
        lb = jnp.minimum(z, 0.0) - jnp.log(1.0 + jnp.exp2(-jnp.abs(z))) * LOG2_E
        return lb, lb - z

    def suffix_and_total(l):
        hi = l.astype(BF16)
        lo = (l - hi.astype(F32)).astype(BF16)
        cs = jnp.dot(jnp.concatenate([hi, lo], axis=1), tri, preferred_element_type=F32)
        return cs[:, :tk], cs[:, tk:]

    def head_queries(g):
        q = (q_ref[0, :, g * LANES:(g + 1) * LANES].astype(F32) * scale2).astype(BF16)
        return (jnp.where(lane_q < HEAD_DIM, q, jnp.zeros_like(q)),
                jnp.where(lane_q >= HEAD_DIM, q, jnp.zeros_like(q)))

    def scores(qh, k_blk):
        return lax.dot_general(qh, k_blk, (((1,), (1,)), ((), ())), preferred_element_type=F32)

    def chain_window(c):
        rows = slice(c * half, (c + 1) * half)
        first = nleft * tk + (c + 1) * half - 2 * tk
        return rows, slice(first, first + 2 * tk)

    def attention_stages():
        for g in range(npair):
            cols = slice(g * LANES, (g + 1) * LANES)
            q_heads = head_queries(g)
            k_all = jnp.concatenate([kp_ref[0, :, cols], kc_ref[0, :, cols]], axis=0)
            v_all = jnp.concatenate([vp_ref[0, :, cols], vc_ref[0, :, cols]], axis=0)
            chains = []
            for c in range(tq // half):
                rows, win = chain_window(c)
                for h in range(2):
                    chains.append(dict(h=2 * g + h, c=c, rows=rows, v=v_all[win, :],
                                       z=scores(q_heads[h][rows, :], k_all[win, :])))
            yield
            for ch in chains:
                ch["lb"], ch["parts"] = [], []
                for kb in range(2):
                    zb = ch["z"][:, kb * tk:(kb + 1) * tk]
                    if kb == 1:
                        zb = jnp.where(causal, zb, -MASKED_SCORE)
                    n_left = min(max((2 - kb) * tk - (ch["c"] + 1) * half, 0), tk)
                    if n_left:
                        zb = jnp.where(jnp.logical_or(has_left, key_lane >= n_left), zb,
                                       -MASKED_SCORE)
                    lb, l = softplus_parts(zb)
                    ch["lb"].append(lb)
                    ch["parts"].append(suffix_and_total(l))
            yield
            for ch in chains:
                (suffix0, total0), (suffix1, total1) = ch["parts"]
                att = jnp.concatenate(
                    [jnp.exp2(ch["lb"][0] + suffix0 + total1).astype(BF16),
                     jnp.exp2(ch["lb"][1] + suffix1).astype(BF16)], axis=1)
                ch["att"] = att
                r_ref[ch["h"], ch["rows"], :] = total0 + total1
            yield
            for ch in chains:
                acc_ref[ch["h"], ch["rows"], :] = jnp.dot(ch["att"], ch["v"],
                                                          preferred_element_type=F32)
            yield

    _interleave(attention_stages(), mixer)

    tile_start = ti * tq

    def cond(carry):
        s, alive = carry
        return jnp.logical_and(tile_start + tq - 2 * tk - s * tk > 0, alive)

    def body(carry):
        s, _ = carry
        for g in range(npair):
            q_heads = head_queries(g)
            for c in range(tq // half):
                end = tile_start + ((c + 1) * half - 2 * tk) - s * tk
                rows = slice(c * half, (c + 1) * half)
                chain_alive = jnp.max(r_ref[2 * g:2 * g + 2, rows, :]) >= ATT_DEAD_LOG2

                @pl.when(jnp.logical_and(end > 0, chain_alive))
                def _():
                    ks = pl.multiple_of(jnp.maximum(end - tk, 0), half)
                    copies = [
                        pltpu.make_async_copy(
                            qkv_hbm.at[bi, pl.ds(ks, tk), pl.ds((1 + p) * att_w + g * LANES, LANES)],
                            dst, sem.at[p])
                        for p, dst in enumerate((kblk_ref, vblk_ref))]
                    for cp in copies:
                        cp.start()
                    for cp in copies:
                        cp.wait()
                    uncovered = key_lane < end - ks
                    for h in range(2):
                        z = scores(q_heads[h][rows, :], kblk_ref[...])
                        lb, l = softplus_parts(jnp.where(uncovered, z, -MASKED_SCORE))
                        suffix, total = suffix_and_total(l)
                        r_old = r_ref[2 * g + h, rows, :]
                        att = jnp.exp2(lb + suffix + r_old).astype(BF16)
                        acc_ref[2 * g + h, rows, :] += jnp.dot(att, vblk_ref[...],
                                                               preferred_element_type=F32)
                        r_ref[2 * g + h, rows, :] = r_old + total
        return s + 1, jnp.max(r_ref[...]) >= ATT_DEAD_LOG2

    lax.while_loop(cond, body, (0, jnp.max(r_ref[...]) >= ATT_DEAD_LOG2))

    for g in range(npair):
        yatt_ref[slot, :, g * LANES:(g + 1) * LANES] = jnp.where(
            lane_q < HEAD_DIM, acc_ref[2 * g], acc_ref[2 * g + 1])


def _attn_ffn(qkv, x2, y_lru, g_att, w_out, g_ffn, w_ffn_in, w_ffn_out, g_last, tq, tk, nleft,
              ff_chunk, final_norm):
    bsz, s, three_w = qkv.shape
    att_w = three_w // 3
    n, d = x2.shape
    d_ff = w_ffn_out.shape[0]
    tiles_per_seq = s // tq
    n_tiles = bsz * tiles_per_seq
    left = nleft * tk
    assert s % tq == 0 and tq % tk == 0 and nleft == 2 and tq % left == 0 and att_w % LANES == 0

    def tile(j):
        ja = jnp.minimum(j, n_tiles - 1)
        return ja // tiles_per_seq, ja % tiles_per_seq

    def cur(part):
        return pl.BlockSpec((1, tq, att_w), lambda j: (*tile(j), part))

    def prev(part):
        def index(j):
            b, t = tile(j)
            return b, jnp.maximum(t * (tq // left) - 1, 0), part
        return pl.BlockSpec((1, left, att_w), index)

    row = lambda w: pl.BlockSpec((tq, w), lambda j: (jnp.maximum(j - 1, 0), 0))
    vec = lambda w: pl.BlockSpec((1, w), lambda j: (0, 0))
    whole = lambda a: pl.BlockSpec(a.shape, lambda j: (0, 0), pipeline_mode=pl.Buffered(1))
    nheads = att_w // HEAD_DIM
    return pl.pallas_call(
        functools.partial(_attn_ffn_kernel, tq=tq, tk=tk, nleft=nleft,
                          scale2=LOG2_E / math.sqrt(HEAD_DIM), tiles_per_seq=tiles_per_seq,
                          n_tiles=n_tiles, att_w=att_w, d_ff=d_ff, ff_chunk=ff_chunk,
                          final_norm=final_norm),
        grid=(n_tiles + 1,),
        in_specs=[cur(0), cur(1), prev(1), cur(2), prev(2),
                  pl.BlockSpec(memory_space=pl.ANY),
                  row(d), row(y_lru.shape[1]), vec(att_w), whole(w_out), vec(d),
                  whole(w_ffn_in), whole(w_ffn_out), vec(d)],
        out_specs=row(d),
        out_shape=jax.ShapeDtypeStruct((n, d), F32),
        scratch_shapes=[
            pltpu.VMEM((2, tq, att_w), F32),
            pltpu.VMEM((nheads, tq, tk), F32),
            pltpu.VMEM((nheads, tq, LANES), F32),
            pltpu.VMEM((tk, LANES), BF16),
            pltpu.VMEM((tk, LANES), BF16),
            pltpu.SemaphoreType.DMA((2,)),
        ],
        compiler_params=pltpu.CompilerParams(
            dimension_semantics=("arbitrary",), vmem_limit_bytes=VMEM_LIMIT_BYTES),
        name="attn_ffn",
    )(qkv, qkv, qkv, qkv, qkv, qkv, x2, y_lru, g_att, w_out, g_ffn, w_ffn_in, w_ffn_out, g_last)


def _largest_chunk(total, limit):
    best = LANES
    for c in range(LANES, limit + 1, LANES):
        if total % c == 0:
            best = c
    return best


def kernel(x, norm_mix, w_in, conv_w, conv_b, w_rg, b_rg, w_ig, b_ig, lru_lambda, norm_lru_out,
           norm_att_out, w_out, norm_ffn, w_ffn_in, w_ffn_out, norm_final):
    bsz, s, d = x.shape
    depth = w_in.shape[0]
    lru_w = conv_w.shape[2]
    att_w = norm_att_out.shape[1]
    blk_w = w_rg.shape[2]
    assert w_in.shape[2] == 2 * lru_w + 3 * att_w and att_w % LANES == 0
    assert MXU_DIM % blk_w == 0 and lru_w % MXU_DIM == 0

    n = bsz * s
    ts = 512
    assert s % ts == 0
    tq, tk, nleft = 512, 128, 2
    per = MXU_DIM // blk_w
    row1 = lambda v: v.reshape(1, -1)

    x2 = x.reshape(n, d)
    for l in range(depth):
        qkv, y_lru, w_out_b, w_ffn_in_b, w_ffn_out_b = _inproj_lru(
            x2, row1(norm_mix[l]), w_in[l].astype(BF16), conv_w[l], row1(conv_b[l]),
            _block_diag(w_rg[l], per).astype(BF16), row1(b_rg[l]),
            _block_diag(w_ig[l], per).astype(BF16), row1(b_ig[l]), row1(lru_lambda[l]),
            row1(norm_lru_out[l]), (w_out[l], w_ffn_in[l], w_ffn_out[l]), s, ts)
        d_ff = w_ffn_out.shape[1]
        x2 = _attn_ffn(qkv.reshape(bsz, s, 3 * att_w), x2, y_lru, row1(norm_att_out[l]), w_out_b,
                       row1(norm_ffn[l]), w_ffn_in_b, w_ffn_out_b, row1(norm_final),
                       tq, tk, nleft, _largest_chunk(d_ff, 512), l == depth - 1)
    return x2.reshape(bsz, s, d)
```

```python
import functools
import math

import jax
import jax.numpy as jnp
from jax import lax
from jax.experimental import pallas as pl
from jax.experimental.pallas import tpu as pltpu

F32 = jnp.float32
BF16 = jnp.bfloat16

EPS = 1e-6
LRU_C = 8.0
CONV_W = 4
HEAD_DIM = 64

LANES = 128
SUBLANES = 8
MXU_DIM = 256
VMEM_LIMIT_BYTES = 56 * 1024 * 1024

LOG2_E = 1.4426950408889634
ATT_DEAD_LOG2 = -160.0
MASKED_SCORE = 1e30
CAST_STEPS = 8


def _rms_scale(x):
    return lax.rsqrt(jnp.mean(x * x, axis=-1, keepdims=True) + EPS)


def _sigmoid(x):
    return 1.0 / (1.0 + jnp.exp(-x))


def _softplus(x):
    return jnp.maximum(x, 0.0) + jnp.log(1.0 + jnp.exp(-jnp.abs(x)))


def _gelu_tanh(x):
    c = math.sqrt(2.0 / math.pi)
    return 0.5 * x * (1.0 + jnp.tanh(c * (x + 0.044715 * (x * x * x))))


def _inproj_stages(x_ref, g_ref, w_ref, u_slot_ref, qkv_ref, *, lru_cols, col_chunk):
    x = x_ref[...]
    h = ((x * _rms_scale(x)) * g_ref[...]).astype(BF16)
    yield
    for lo in range(0, w_ref.shape[1], col_chunk):
        u = jnp.dot(h, w_ref[:, lo:lo + col_chunk], preferred_element_type=F32)
        if lo < lru_cols:
            u_slot_ref[:, lo:lo + col_chunk] = u
        else:
            qkv_ref[:, lo - lru_cols:lo - lru_cols + col_chunk] = u.astype(BF16)
        yield


def _lru_stages(first, u_ref, cw_ref, cb_ref, wrg_ref, brg_ref, wig_ref, big_ref, lam_ref, gn_ref,
                out_ref, xs_ref, hc_ref, *, ts, width, slab):
    halo = SUBLANES
    xs_ref[0:halo, :] = jnp.where(first, 0.0, xs_ref[ts:ts + halo, :])
    xs_ref[halo:halo + ts, :] = u_ref[:, 0:width]
    row = lax.broadcasted_iota(jnp.int32, (SUBLANES, width), 0)
    hprev = jnp.where(first, 0.0, hc_ref[...])
    neg_sp = -LRU_C * _softplus(-lam_ref[...])

    for s0 in range(0, ts, slab):
        xc = jnp.broadcast_to(cb_ref[...], (slab, width))
        for i in range(CONV_W):
            off = s0 + halo - (CONV_W - 1) + i
            xc = xc + xs_ref[off:off + slab, :] * cw_ref[i:i + 1, :]

        xcb = xc.astype(BF16)
        rg, ig = [], []
        for c in range(width // MXU_DIM):
            sl = slice(c * MXU_DIM, (c + 1) * MXU_DIM)
            rg.append(jnp.dot(xcb[:, sl], wrg_ref[c], preferred_element_type=F32))
            ig.append(jnp.dot(xcb[:, sl], wig_ref[c], preferred_element_type=F32))
        r = _sigmoid(jnp.concatenate(rg, axis=1) + brg_ref[...])
        ig = _sigmoid(jnp.concatenate(ig, axis=1) + big_ref[...])

        log_a = r * neg_sp
        a = jnp.exp(log_a)
        th = jnp.tanh(log_a)
        b = jnp.sqrt((-2.0 * th) / (1.0 - th)) * (ig * xc)
        yield

        hs = []
        for g in range(slab // SUBLANES):
            rows = slice(g * SUBLANES, (g + 1) * SUBLANES)
            a8, b8 = a[rows, :], b[rows, :]
            for d in (1, 2, 4):
                keep = row >= d
                b8 = jnp.where(keep, b8 + a8 * pltpu.roll(b8, d, axis=0), b8)
                a8 = jnp.where(keep, a8 * pltpu.roll(a8, d, axis=0), a8)
            h8 = b8 + a8 * hprev
            hs.append(h8)
            hprev = jnp.broadcast_to(h8[SUBLANES - 1:SUBLANES, :], (SUBLANES, width))
        y = _gelu_tanh(u_ref[s0:s0 + slab, width:2 * width]) * jnp.concatenate(hs, axis=0)
        out_ref[s0:s0 + slab, :] = ((y * _rms_scale(y)) * gn_ref[...]).astype(out_ref.dtype)
        yield
    hc_ref[...] = hprev


def _interleave(*streams):
    streams = list(streams)
    while streams:
        for st in list(streams):
            if next(st, StopIteration) is StopIteration:
                streams.remove(st)


def _inproj_lru_kernel(x_ref, gm_ref, w_ref, cw_ref, cb_ref, wrg_ref, brg_ref, wig_ref, big_ref,
                       lam_ref, gn_ref, *rest, ts, width, tiles_per_seq, col_chunk, slab,
                       cast_slabs):
    nw = len(cast_slabs)
    wsrc_refs, (qkv_ref, ylru_ref) = rest[:nw], rest[nw:nw + 2]
    wdst_refs = rest[nw + 2:2 * nw + 2]
    u_ref, xs_ref, hc_ref = rest[2 * nw + 2:]
    j = pl.program_id(0)
    for src_ref, dst_ref, n_slabs in zip(wsrc_refs, wdst_refs, cast_slabs):
        @pl.when(j < n_slabs)
        def _():
            dst_ref[...] = src_ref[...].astype(dst_ref.dtype)

    slot = j % 2
    lt = jnp.maximum(j - 1, 0)

    @pl.when(j == 0)
    def _():
        for ref in (u_ref, xs_ref, hc_ref):
            ref[...] = jnp.zeros(ref.shape, ref.dtype)

    _interleave(
        _lru_stages(lt % tiles_per_seq == 0, u_ref.at[1 - slot], cw_ref, cb_ref, wrg_ref, brg_ref,
                    wig_ref, big_ref, lam_ref, gn_ref, ylru_ref, xs_ref, hc_ref,
                    ts=ts, width=width, slab=slab),
        _inproj_stages(x_ref, gm_ref, w_ref, u_ref.at[slot], qkv_ref, lru_cols=2 * width,
                       col_chunk=col_chunk))


def _cast_slab_rows(rows, max_slabs):
    step = 2 * SUBLANES
    for slab_rows in range(step, rows + 1, step):
        if rows % slab_rows == 0 and rows // slab_rows <= max_slabs:
            return slab_rows
    raise ValueError(f"no bf16 row slab for {rows} rows in {max_slabs} steps")


def _inproj_lru(x2, gain, w_in_bf16, conv_w, conv_b, wrg_bd, b_rg, wig_bd, b_ig, lam, gain_lru,
                later_weights, seq_len, ts):
    n, d = x2.shape
    cols = w_in_bf16.shape[1]
    width = conv_w.shape[1]
    n_tiles = n // ts
    slab_rows = [_cast_slab_rows(w.shape[0], min(n_tiles, CAST_STEPS)) for w in later_weights]
    cast_slabs = tuple(w.shape[0] // r for w, r in zip(later_weights, slab_rows))
    wslab = lambda w, r: pl.BlockSpec(
        (r, w.shape[1]), lambda j: (jnp.minimum(j, w.shape[0] // r - 1), 0))
    wspecs = [wslab(w, r) for w, r in zip(later_weights, slab_rows)]
    tile = lambda lag: (lambda j: (jnp.clip(j - lag, 0, n_tiles - 1), 0))
    vec = lambda w: pl.BlockSpec((1, w), lambda j: (0, 0))
    wspec = pl.BlockSpec(wrg_bd.shape, lambda j: (0, 0, 0))
    return pl.pallas_call(
        functools.partial(_inproj_lru_kernel, ts=ts, width=width, tiles_per_seq=seq_len // ts,
                          col_chunk=MXU_DIM, slab=LANES // 2, cast_slabs=cast_slabs),
        grid=(n_tiles + 1,),
        in_specs=[
            pl.BlockSpec((ts, d), tile(0)),
            vec(d),
            pl.BlockSpec((d, cols), lambda j: (0, 0), pipeline_mode=pl.Buffered(1)),
            pl.BlockSpec((CONV_W, width), lambda j: (0, 0)),
            vec(width), wspec, vec(width), wspec, vec(width), vec(width), vec(width),
            *wspecs,
        ],
        out_specs=[
            pl.BlockSpec((ts, cols - 2 * width), tile(0)),
            pl.BlockSpec((ts, width), tile(1)),
            *wspecs,
        ],
        out_shape=[
            jax.ShapeDtypeStruct((n, cols - 2 * width), BF16),
            jax.ShapeDtypeStruct((n, width), BF16),
            *[jax.ShapeDtypeStruct(w.shape, BF16) for w in later_weights],
        ],
        scratch_shapes=[
            pltpu.VMEM((2, ts, 2 * width), F32),
            pltpu.VMEM((ts + SUBLANES, width), F32),
            pltpu.VMEM((SUBLANES, width), F32),
        ],
        compiler_params=pltpu.CompilerParams(
            dimension_semantics=("arbitrary",), vmem_limit_bytes=VMEM_LIMIT_BYTES),
        name="inproj_lru",
    )(x2, gain, w_in_bf16, conv_w, conv_b, wrg_bd, b_rg, wig_bd, b_ig, lam, gain_lru,
      *later_weights)


def _block_diag(w, per):
    nb, bw, _ = w.shape
    w = w.reshape(nb // per, per, bw, bw)
    eye = jnp.eye(per, dtype=w.dtype)
    return jnp.einsum('gpij,pq->gpiqj', w, eye).reshape(nb // per, per * bw, per * bw)


def _ffn_stages(x_ref, yl_ref, ya, ga_ref, wo_ref, gf_ref, wi_ref, wd_ref, gl_ref, out_ref, *,
                d_ff, ff_chunk, final_norm):
    ya = ((ya * _rms_scale(ya)) * ga_ref[...]).astype(BF16)
    wl = yl_ref.shape[1]
    x1 = x_ref[...] + jnp.dot(yl_ref[...], wo_ref[0:wl, :], preferred_element_type=F32)
    x1 = x1 + jnp.dot(ya, wo_ref[wl:, :], preferred_element_type=F32)
    h2 = ((x1 * _rms_scale(x1)) * gf_ref[...]).astype(BF16)
    yield
    acts = []
    for c in range(d_ff // ff_chunk):
        lo = c * ff_chunk
        gate = jnp.dot(h2, wi_ref[:, lo:lo + ff_chunk], preferred_element_type=F32)
        up = jnp.dot(h2, wi_ref[:, d_ff + lo:d_ff + lo + ff_chunk], preferred_element_type=F32)
        acts.append(((gate * _sigmoid(gate)) * up).astype(BF16))
        yield
    out = x1 + jnp.dot(jnp.concatenate(acts, axis=1), wd_ref[...], preferred_element_type=F32)
    yield
    if final_norm:
        out = (out * _rms_scale(out)) * gl_ref[...]
    out_ref[...] = out


def _attn_ffn_kernel(q_ref, kc_ref, kp_ref, vc_ref, vp_ref, qkv_hbm, x_ref, yl_ref, ga_ref,
                     wo_ref, gf_ref, wi_ref, wd_ref, gl_ref, out_ref,
                     yatt_ref, r_ref, acc_ref, kblk_ref, vblk_ref, sem,
                     *, tq, tk, nleft, scale2, tiles_per_seq, n_tiles, att_w, d_ff, ff_chunk,
                     final_norm):
    j = pl.program_id(0)
    ja = jnp.minimum(j, n_tiles - 1)
    bi = ja // tiles_per_seq
    ti = ja % tiles_per_seq
    nrb = tq // tk
    g0 = ti * nrb
    npair = att_w // LANES
    slot = j % 2

    @pl.when(j == 0)
    def _():
        yatt_ref[...] = jnp.zeros(yatt_ref.shape, F32)

    mixer = _ffn_stages(x_ref, yl_ref, yatt_ref[1 - slot], ga_ref, wo_ref, gf_ref, wi_ref, wd_ref,
                        gl_ref, out_ref, d_ff=d_ff, ff_chunk=ff_chunk, final_norm=final_norm)

    lane_q = lax.broadcasted_iota(jnp.int32, (tq, LANES), 1)
    rr = lax.broadcasted_iota(jnp.int32, (2 * tk, 2 * tk), 0)
    cc = lax.broadcasted_iota(jnp.int32, (2 * tk, 2 * tk), 1)
    key_j = jnp.where(rr >= tk, rr - tk, rr)
    tri = jnp.where((cc >= tk) | (key_j > cc), 1.0, 0.0).astype(BF16)
    below_diag = (lax.broadcasted_iota(jnp.int32, (tk, tk), 1)
                  < lax.broadcasted_iota(jnp.int32, (tk, tk), 0))
    has_left = ti > 0

    def softplus_parts(z):
        lb = jnp.minimum(z, 0.0) - jnp.log(1.0 + jnp.exp2(-jnp.abs(z))) * LOG2_E
        return lb, lb - z

    def suffix_and_total(l):
        hi = l.astype(BF16)
        lo = (l - hi.astype(F32)).astype(BF16)
        cs = jnp.dot(jnp.concatenate([hi, lo], axis=1), tri, preferred_element_type=F32)
        return cs[:, :tk], cs[:, tk:]

    def head_queries(g):
        q = (q_ref[0, :, g * LANES:(g + 1) * LANES].astype(F32) * scale2).astype(BF16)
        return (jnp.where(lane_q < HEAD_DIM, q, jnp.zeros_like(q)),
                jnp.where(lane_q >= HEAD_DIM, q, jnp.zeros_like(q)))

    def scores(qh, k_blk):
        return lax.dot_general(qh, k_blk, (((1,), (1,)), ((), ())), preferred_element_type=F32)

    def attention_stages():
        for g in range(npair):
            cols = slice(g * LANES, (g + 1) * LANES)
            q_heads = head_queries(g)
            k_all = jnp.concatenate([kp_ref[0, :, cols], kc_ref[0, :, cols]], axis=0)
            v_all = jnp.concatenate([vp_ref[0, :, cols], vc_ref[0, :, cols]], axis=0)
            chains = []
            for rb in range(nrb):
                rows = slice(rb * tk, (rb + 1) * tk)
                win = slice(rb * tk, (rb + nleft + 1) * tk)
                for h in range(2):
                    chains.append(dict(h=2 * g + h, rb=rb, rows=rows, v=v_all[win, :],
                                       z=scores(q_heads[h][rows, :], k_all[win, :])))
            yield
            for c in chains:
                c["lb"], c["parts"] = [], []
                for kb in range(nleft + 1):
                    zb = c["z"][:, kb * tk:(kb + 1) * tk]
                    if kb == nleft:
                        zb = jnp.where(below_diag, zb, -MASKED_SCORE)
                    elif c["rb"] + kb < nleft:
                        zb = jnp.where(has_left, zb, -MASKED_SCORE)
                    lb, l = softplus_parts(zb)
                    c["lb"].append(lb)
                    c["parts"].append(suffix_and_total(l))
            yield
            for c in chains:
                r = jnp.zeros((tk, tk), F32)
                atts = [None] * (nleft + 1)
                for kb in reversed(range(nleft + 1)):
                    suffix, total = c["parts"][kb]
                    atts[kb] = jnp.exp2(c["lb"][kb] + suffix + r).astype(BF16)
                    r = r + total
                c["att"] = jnp.concatenate(atts, axis=1)
                r_ref[c["h"], c["rows"], :] = r
            yield
            for c in chains:
                acc_ref[c["h"], c["rows"], :] = jnp.dot(c["att"], c["v"],
                                                        preferred_element_type=F32)
            yield

    _interleave(attention_stages(), mixer)

    def cond(carry):
        s, alive = carry
        return jnp.logical_and(s <= g0 + (nrb - 1 - nleft - 1), alive)

    def body(carry):
        s, _ = carry
        for g in range(npair):
            q_heads = head_queries(g)
            for rb in range(nrb):
                jb = g0 + (rb - nleft - 1) - s
                rows = slice(rb * tk, (rb + 1) * tk)
                rows_alive = jnp.max(r_ref[2 * g:2 * g + 2, rows, :]) >= ATT_DEAD_LOG2

                @pl.when(jnp.logical_and(jb >= 0, rows_alive))
                def _():
                    ks = pl.multiple_of(jb * tk, tk)
                    copies = [
                        pltpu.make_async_copy(
                            qkv_hbm.at[bi, pl.ds(ks, tk), pl.ds((1 + p) * att_w + g * LANES, LANES)],
                            dst, sem.at[p])
                        for p, dst in enumerate((kblk_ref, vblk_ref))]
                    for cp in copies:
                        cp.start()
                    for cp in copies:
                        cp.wait()
                    for h in range(2):
                        lb, l = softplus_parts(scores(q_heads[h][rows, :], kblk_ref[...]))
                        suffix, total = suffix_and_total(l)
                        r_old = r_ref[2 * g + h, rows, :]
                        att = jnp.exp2(lb + suffix + r_old).astype(BF16)
                        acc_ref[2 * g + h, rows, :] += jnp.dot(att, vblk_ref[...],
                                                               preferred_element_type=F32)
                        r_ref[2 * g + h, rows, :] = r_old + total
        return s + 1, jnp.max(r_ref[...]) >= ATT_DEAD_LOG2

    lax.while_loop(cond, body, (0, jnp.max(r_ref[...]) >= ATT_DEAD_LOG2))

    for g in range(npair):
        yatt_ref[slot, :, g * LANES:(g + 1) * LANES] = jnp.where(
            lane_q < HEAD_DIM, acc_ref[2 * g], acc_ref[2 * g + 1])


def _attn_ffn(qkv, x2, y_lru, g_att, w_out, g_ffn, w_ffn_in, w_ffn_out, g_last, tq, tk, nleft,
              ff_chunk, final_norm):
    bsz, s, three_w = qkv.shape
    att_w = three_w // 3
    n, d = x2.shape
    d_ff = w_ffn_out.shape[0]
    tiles_per_seq = s // tq
    n_tiles = bsz * tiles_per_seq
    left = nleft * tk
    assert s % tq == 0 and tq % tk == 0 and tq // tk >= nleft and tq % left == 0
    assert att_w % LANES == 0

    def tile(j):
        ja = jnp.minimum(j, n_tiles - 1)
        return ja // tiles_per_seq, ja % tiles_per_seq

    def cur(part):
        return pl.BlockSpec((1, tq, att_w), lambda j: (*tile(j), part))

    def prev(part):
        def index(j):
            b, t = tile(j)
            return b, jnp.maximum(t * (tq // left) - 1, 0), part
        return pl.BlockSpec((1, left, att_w), index)

    row = lambda w: pl.BlockSpec((tq, w), lambda j: (jnp.maximum(j - 1, 0), 0))
    vec = lambda w: pl.BlockSpec((1, w), lambda j: (0, 0))
    whole = lambda a: pl.BlockSpec(a.shape, lambda j: (0, 0), pipeline_mode=pl.Buffered(1))
    nheads = att_w // HEAD_DIM
    return pl.pallas_call(
        functools.partial(_attn_ffn_kernel, tq=tq, tk=tk, nleft=nleft,
                          scale2=LOG2_E / math.sqrt(HEAD_DIM), tiles_per_seq=tiles_per_seq,
                          n_tiles=n_tiles, att_w=att_w, d_ff=d_ff, ff_chunk=ff_chunk,
                          final_norm=final_norm),
        grid=(n_tiles + 1,),
        in_specs=[cur(0), cur(1), prev(1), cur(2), prev(2),
                  pl.BlockSpec(memory_space=pl.ANY),
                  row(d), row(y_lru.shape[1]), vec(att_w), whole(w_out), vec(d),
                  whole(w_ffn_in), whole(w_ffn_out), vec(d)],
        out_specs=row(d),
        out_shape=jax.ShapeDtypeStruct((n, d), F32),
        scratch_shapes=[
            pltpu.VMEM((2, tq, att_w), F32),
            pltpu.VMEM((nheads, tq, tk), F32),
            pltpu.VMEM((nheads, tq, LANES), F32),
            pltpu.VMEM((tk, LANES), BF16),
            pltpu.VMEM((tk, LANES), BF16),
            pltpu.SemaphoreType.DMA((2,)),
        ],
        compiler_params=pltpu.CompilerParams(
            dimension_semantics=("arbitrary",), vmem_limit_bytes=VMEM_LIMIT_BYTES),
        name="attn_ffn",
    )(qkv, qkv, qkv, qkv, qkv, qkv, x2, y_lru, g_att, w_out, g_ffn, w_ffn_in, w_ffn_out, g_last)


def _largest_chunk(total, limit):
    best = LANES
    for c in range(LANES, limit + 1, LANES):
        if total % c == 0:
            best = c
    return best


def kernel(x, norm_mix, w_in, conv_w, conv_b, w_rg, b_rg, w_ig, b_ig, lru_lambda, norm_lru_out,
           norm_att_out, w_out, norm_ffn, w_ffn_in, w_ffn_out, norm_final):
    bsz, s, d = x.shape
    depth = w_in.shape[0]
    lru_w = conv_w.shape[2]
    att_w = norm_att_out.shape[1]
    blk_w = w_rg.shape[2]
    assert w_in.shape[2] == 2 * lru_w + 3 * att_w and att_w % LANES == 0
    assert MXU_DIM % blk_w == 0 and lru_w % MXU_DIM == 0

    n = bsz * s
    ts = 512
    assert s % ts == 0
    tq, tk, nleft = 512, 128, 2
    per = MXU_DIM // blk_w
    row1 = lambda v: v.reshape(1, -1)

    x2 = x.reshape(n, d)
    for l in range(depth):
        qkv, y_lru, w_out_b, w_ffn_in_b, w_ffn_out_b = _inproj_lru(
            x2, row1(norm_mix[l]), w_in[l].astype(BF16), conv_w[l], row1(conv_b[l]),
            _block_diag(w_rg[l], per).astype(BF16), row1(b_rg[l]),
            _block_diag(w_ig[l], per).astype(BF16), row1(b_ig[l]), row1(lru_lambda[l]),
            row1(norm_lru_out[l]), (w_out[l], w_ffn_in[l], w_ffn_out[l]), s, ts)
        d_ff = w_ffn_out.shape[1]
        x2 = _attn_ffn(qkv.reshape(bsz, s, 3 * att_w), x2, y_lru, row1(norm_att_out[l]), w_out_b,
                       row1(norm_ffn[l]), w_ffn_in_b, w_ffn_out_b, row1(norm_final),
                       tq, tk, nleft, _largest_chunk(d_ff, 512), l == depth - 1)
    return x2.reshape(bsz, s, d)
```

```python
import functools
import math

import jax
import jax.numpy as jnp
from jax import lax
from jax.experimental import pallas as pl
from jax.experimental.pallas import tpu as pltpu

F32 = jnp.float32
BF16 = jnp.bfloat16

EPS = 1e-6
LRU_C = 8.0
CONV_W = 4
HEAD_DIM = 64

LANES = 128
SUBLANES = 8
MXU_DIM = 256
VMEM_LIMIT_BYTES = 56 * 1024 * 1024

LOG2_E = 1.4426950408889634
ATT_DEAD_LOG2 = -160.0
MASKED_SCORE = 1e30
CAST_STEPS = 8


def _rms_scale(x):
    return lax.rsqrt(jnp.mean(x * x, axis=-1, keepdims=True) + EPS)


def _sigmoid(x):
    return 1.0 / (1.0 + jnp.exp(-x))


def _softplus(x):
    return jnp.maximum(x, 0.0) + jnp.log(1.0 + jnp.exp(-jnp.abs(x)))


def _gelu_tanh(x):
    c = math.sqrt(2.0 / math.pi)
    return 0.5 * x * (1.0 + jnp.tanh(c * (x + 0.044715 * (x * x * x))))


def _inproj_stages(x_ref, g_ref, w_ref, u_slot_ref, qkv_ref, *, lru_cols, col_chunk):
    x = x_ref[...]
    h = ((x * _rms_scale(x)) * g_ref[...]).astype(BF16)
    yield
    for lo in range(0, w_ref.shape[1], col_chunk):
        u = jnp.dot(h, w_ref[:, lo:lo + col_chunk], preferred_element_type=F32)
        if lo < lru_cols:
            u_slot_ref[:, lo:lo + col_chunk] = u
        else:
            qkv_ref[:, lo - lru_cols:lo - lru_cols + col_chunk] = u.astype(BF16)
        yield


def _lru_stages(first, u_ref, cw_ref, cb_ref, wrg_ref, brg_ref, wig_ref, big_ref, lam_ref, gn_ref,
                out_ref, xs_ref, hc_ref, *, ts, width, slab):
    halo = SUBLANES
    xs_ref[0:halo, :] = jnp.where(first, 0.0, xs_ref[ts:ts + halo, :])
    xs_ref[halo:halo + ts, :] = u_ref[:, 0:width]
    row = lax.broadcasted_iota(jnp.int32, (SUBLANES, width), 0)
    hprev = jnp.where(first, 0.0, hc_ref[...])
    neg_sp = -LRU_C * _softplus(-lam_ref[...])

    for s0 in range(0, ts, slab):
        xc = jnp.broadcast_to(cb_ref[...], (slab, width))
        for i in range(CONV_W):
            off = s0 + halo - (CONV_W - 1) + i
            xc = xc + xs_ref[off:off + slab, :] * cw_ref[i:i + 1, :]

        xcb = xc.astype(BF16)
        rg, ig = [], []
        for c in range(width // MXU_DIM):
            sl = slice(c * MXU_DIM, (c + 1) * MXU_DIM)
            rg.append(jnp.dot(xcb[:, sl], wrg_ref[c], preferred_element_type=F32))
            ig.append(jnp.dot(xcb[:, sl], wig_ref[c], preferred_element_type=F32))
        r = _sigmoid(jnp.concatenate(rg, axis=1) + brg_ref[...])
        ig = _sigmoid(jnp.concatenate(ig, axis=1) + big_ref[...])

        log_a = r * neg_sp
        a = jnp.exp(log_a)
        th = jnp.tanh(log_a)
        b = jnp.sqrt((-2.0 * th) / (1.0 - th)) * (ig * xc)
        yield

        hs = []
        for g in range(slab // SUBLANES):
            rows = slice(g * SUBLANES, (g + 1) * SUBLANES)
            a8, b8 = a[rows, :], b[rows, :]
            for d in (1, 2, 4):
                keep = row >= d
                b8 = jnp.where(keep, b8 + a8 * pltpu.roll(b8, d, axis=0), b8)
                a8 = jnp.where(keep, a8 * pltpu.roll(a8, d, axis=0), a8)
            h8 = b8 + a8 * hprev
            hs.append(h8)
            hprev = jnp.broadcast_to(h8[SUBLANES - 1:SUBLANES, :], (SUBLANES, width))
        y = _gelu_tanh(u_ref[s0:s0 + slab, width:2 * width]) * jnp.concatenate(hs, axis=0)
        out_ref[s0:s0 + slab, :] = ((y * _rms_scale(y)) * gn_ref[...]).astype(out_ref.dtype)
        yield
    hc_ref[...] = hprev


def _interleave(*streams):
    streams = list(streams)
    while streams:
        for st in list(streams):
            if next(st, StopIteration) is StopIteration:
                streams.remove(st)


def _inproj_lru_kernel(x_ref, gm_ref, w_ref, cw_ref, cb_ref, wrg_ref, brg_ref, wig_ref, big_ref,
                       lam_ref, gn_ref, *rest, ts, width, tiles_per_seq, col_chunk, slab,
                       cast_slabs):
    nw = len(cast_slabs)
    wsrc_refs, (qkv_ref, ylru_ref) = rest[:nw], rest[nw:nw + 2]
    wdst_refs = rest[nw + 2:2 * nw + 2]
    u_ref, xs_ref, hc_ref = rest[2 * nw + 2:]
    j = pl.program_id(0)
    for src_ref, dst_ref, n_slabs in zip(wsrc_refs, wdst_refs, cast_slabs):
        @pl.when(j < n_slabs)
        def _():
            dst_ref[...] = src_ref[...].astype(dst_ref.dtype)

    slot = j % 2
    lt = jnp.maximum(j - 1, 0)

    @pl.when(j == 0)
    def _():
        for ref in (u_ref, xs_ref, hc_ref):
            ref[...] = jnp.zeros(ref.shape, ref.dtype)

    _interleave(
        _lru_stages(lt % tiles_per_seq == 0, u_ref.at[1 - slot], cw_ref, cb_ref, wrg_ref, brg_ref,
                    wig_ref, big_ref, lam_ref, gn_ref, ylru_ref, xs_ref, hc_ref,
                    ts=ts, width=width, slab=slab),
        _inproj_stages(x_ref, gm_ref, w_ref, u_ref.at[slot], qkv_ref, lru_cols=2 * width,
                       col_chunk=col_chunk))


def _cast_slab_rows(rows, max_slabs):
    step = 2 * SUBLANES
    for slab_rows in range(step, rows + 1, step):
        if rows % slab_rows == 0 and rows // slab_rows <= max_slabs:
            return slab_rows
    raise ValueError(f"no bf16 row slab for {rows} rows in {max_slabs} steps")


def _inproj_lru(x2, gain, w_in_bf16, conv_w, conv_b, wrg_bd, b_rg, wig_bd, b_ig, lam, gain_lru,
                later_weights, seq_len, ts):
    n, d = x2.shape
    cols = w_in_bf16.shape[1]
    width = conv_w.shape[1]
    n_tiles = n // ts
    slab_rows = [_cast_slab_rows(w.shape[0], min(n_tiles, CAST_STEPS)) for w in later_weights]
    cast_slabs = tuple(w.shape[0] // r for w, r in zip(later_weights, slab_rows))
    wslab = lambda w, r: pl.BlockSpec(
        (r, w.shape[1]), lambda j: (jnp.minimum(j, w.shape[0] // r - 1), 0))
    wspecs = [wslab(w, r) for w, r in zip(later_weights, slab_rows)]
    tile = lambda lag: (lambda j: (jnp.clip(j - lag, 0, n_tiles - 1), 0))
    vec = lambda w: pl.BlockSpec((1, w), lambda j: (0, 0))
    wspec = pl.BlockSpec(wrg_bd.shape, lambda j: (0, 0, 0))
    return pl.pallas_call(
        functools.partial(_inproj_lru_kernel, ts=ts, width=width, tiles_per_seq=seq_len // ts,
                          col_chunk=MXU_DIM, slab=LANES // 2, cast_slabs=cast_slabs),
        grid=(n_tiles + 1,),
        in_specs=[
            pl.BlockSpec((ts, d), tile(0)),
            vec(d),
            pl.BlockSpec((d, cols), lambda j: (0, 0), pipeline_mode=pl.Buffered(1)),
            pl.BlockSpec((CONV_W, width), lambda j: (0, 0)),
            vec(width), wspec, vec(width), wspec, vec(width), vec(width), vec(width),
            *wspecs,
        ],
        out_specs=[
            pl.BlockSpec((ts, cols - 2 * width), tile(0)),
            pl.BlockSpec((ts, width), tile(1)),
            *wspecs,
        ],
        out_shape=[
            jax.ShapeDtypeStruct((n, cols - 2 * width), BF16),
            jax.ShapeDtypeStruct((n, width), BF16),
            *[jax.ShapeDtypeStruct(w.shape, BF16) for w in later_weights],
        ],
        scratch_shapes=[
            pltpu.VMEM((2, ts, 2 * width), F32),
            pltpu.VMEM((ts + SUBLANES, width), F32),
            pltpu.VMEM((SUBLANES, width), F32),
        ],
        compiler_params=pltpu.CompilerParams(
            dimension_semantics=("arbitrary",), vmem_limit_bytes=VMEM_LIMIT_BYTES),
        name="inproj_lru",
    )(x2, gain, w_in_bf16, conv_w, conv_b, wrg_bd, b_rg, wig_bd, b_ig, lam, gain_lru,
      *later_weights)


def _block_diag(w, per):
    nb, bw, _ = w.shape
    w = w.reshape(nb // per, per, bw, bw)
    eye = jnp.eye(per, dtype=w.dtype)
    return jnp.einsum('gpij,pq->gpiqj', w, eye).reshape(nb // per, per * bw, per * bw)


def _ffn_stages(x_ref, yl_ref, ya, ga_ref, wo_ref, gf_ref, wi_ref, wd_ref, gl_ref, out_ref, *,
                d_ff, ff_chunk, final_norm):
    ya = ((ya * _rms_scale(ya)) * ga_ref[...]).astype(BF16)
    wl = yl_ref.shape[1]
    x1 = x_ref[...] + jnp.dot(yl_ref[...], wo_ref[0:wl, :], preferred_element_type=F32)
    x1 = x1 + jnp.dot(ya, wo_ref[wl:, :], preferred_element_type=F32)
    h2 = ((x1 * _rms_scale(x1)) * gf_ref[...]).astype(BF16)
    yield
    acts = []
    for c in range(d_ff // ff_chunk):
        lo = c * ff_chunk
        gate = jnp.dot(h2, wi_ref[:, lo:lo + ff_chunk], preferred_element_type=F32)
        up = jnp.dot(h2, wi_ref[:, d_ff + lo:d_ff + lo + ff_chunk], preferred_element_type=F32)
        acts.append(((gate * _sigmoid(gate)) * up).astype(BF16))
        yield
    out = x1 + jnp.dot(jnp.concatenate(acts, axis=1), wd_ref[...], preferred_element_type=F32)
    yield
    if final_norm:
        out = (out * _rms_scale(out)) * gl_ref[...]
    out_ref[...] = out


def _attn_ffn_kernel(q_ref, kc_ref, kp_ref, vc_ref, vp_ref, qkv_hbm, x_ref, yl_ref, ga_ref,
                     wo_ref, gf_ref, wi_ref, wd_ref, gl_ref, out_ref,
                     yatt_ref, r_ref, acc_ref, kblk_ref, vblk_ref, sem,
                     *, tq, tk, nleft, scale2, tiles_per_seq, n_tiles, att_w, d_ff, ff_chunk,
                     final_norm):
    j = pl.program_id(0)
    ja = jnp.minimum(j, n_tiles - 1)
    bi = ja // tiles_per_seq
    ti = ja % tiles_per_seq
    nrb = tq // tk
    g0 = ti * nrb
    npair = att_w // LANES
    slot = j % 2

    @pl.when(j == 0)
    def _():
        yatt_ref[...] = jnp.zeros(yatt_ref.shape, F32)

    mixer = _ffn_stages(x_ref, yl_ref, yatt_ref[1 - slot], ga_ref, wo_ref, gf_ref, wi_ref, wd_ref,
                        gl_ref, out_ref, d_ff=d_ff, ff_chunk=ff_chunk, final_norm=final_norm)

    rr = lax.broadcasted_iota(jnp.int32, (2 * tk, 2 * tk), 0)
    cc = lax.broadcasted_iota(jnp.int32, (2 * tk, 2 * tk), 1)
    key_j = jnp.where(rr >= tk, rr - tk, rr)
    tri = jnp.where((cc >= tk) | (key_j > cc), 1.0, 0.0).astype(BF16)
    below_diag = (lax.broadcasted_iota(jnp.int32, (tk, tk), 1)
                  < lax.broadcasted_iota(jnp.int32, (tk, tk), 0))
    has_left = ti > 0

    def softplus_parts(z):
        lb = jnp.minimum(z, 0.0) - jnp.log(1.0 + jnp.exp2(-jnp.abs(z))) * LOG2_E
        return lb, lb - z

    def suffix_and_total(l):
        hi = l.astype(BF16)
        lo = (l - hi.astype(F32)).astype(BF16)
        cs = jnp.dot(jnp.concatenate([hi, lo], axis=1), tri, preferred_element_type=F32)
        return cs[:, :tk], cs[:, tk:]

    def pair_queries(g):
        return (q_ref[0, :, g * LANES:(g + 1) * LANES].astype(F32) * scale2).astype(BF16)

    def by_head(kv):
        lane = lax.broadcasted_iota(jnp.int32, kv.shape, 1)
        zero = jnp.zeros_like(kv)
        return jnp.where(lane < HEAD_DIM, kv, zero), jnp.where(lane >= HEAD_DIM, kv, zero)

    def pair_scores(q, k_a, k_b):
        return lax.dot_general(q, jnp.concatenate([k_a, k_b], axis=0), (((1,), (1,)), ((), ())),
                               preferred_element_type=F32)

    def pair_output(att_a, att_b, v_a, v_b):
        return jnp.dot(jnp.concatenate([att_a, att_b], axis=1),
                       jnp.concatenate([v_a, v_b], axis=0), preferred_element_type=F32)

    def attention_stages():
        nk = (nleft + 1) * tk
        for g in range(npair):
            cols = slice(g * LANES, (g + 1) * LANES)
            q = pair_queries(g)
            k_heads = by_head(jnp.concatenate([kp_ref[0, :, cols], kc_ref[0, :, cols]], axis=0))
            v_heads = by_head(jnp.concatenate([vp_ref[0, :, cols], vc_ref[0, :, cols]], axis=0))
            chains = []
            for rb in range(nrb):
                rows = slice(rb * tk, (rb + 1) * tk)
                win = slice(rb * tk, (rb + nleft + 1) * tk)
                z = pair_scores(q[rows, :], k_heads[0][win, :], k_heads[1][win, :])
                chains.append(dict(rb=rb, rows=rows, v=(v_heads[0][win, :], v_heads[1][win, :]),
                                   z=(z[:, :nk], z[:, nk:])))
            yield
            for c in chains:
                c["lb"], c["parts"] = ([], []), ([], [])
                for h in range(2):
                    for kb in range(nleft + 1):
                        zb = c["z"][h][:, kb * tk:(kb + 1) * tk]
                        if kb == nleft:
                            zb = jnp.where(below_diag, zb, -MASKED_SCORE)
                        elif c["rb"] + kb < nleft:
                            zb = jnp.where(has_left, zb, -MASKED_SCORE)
                        lb, l = softplus_parts(zb)
                        c["lb"][h].append(lb)
                        c["parts"][h].append(suffix_and_total(l))
            yield
            for c in chains:
                c["att"] = []
                for h in range(2):
                    r = jnp.zeros((tk, tk), F32)
                    atts = [None] * (nleft + 1)
                    for kb in reversed(range(nleft + 1)):
                        suffix, total = c["parts"][h][kb]
                        atts[kb] = jnp.exp2(c["lb"][h][kb] + suffix + r).astype(BF16)
                        r = r + total
                    c["att"].append(jnp.concatenate(atts, axis=1))
                    r_ref[2 * g + h, c["rows"], :] = r
            yield
            for c in chains:
                acc_ref[g, c["rows"], :] = pair_output(c["att"][0], c["att"][1], *c["v"])
            yield

    _interleave(attention_stages(), mixer)

    def cond(carry):
        s, alive = carry
        return jnp.logical_and(s <= g0 + (nrb - 1 - nleft - 1), alive)

    def body(carry):
        s, _ = carry
        for g in range(npair):
            q = pair_queries(g)
            for rb in range(nrb):
                jb = g0 + (rb - nleft - 1) - s
                rows = slice(rb * tk, (rb + 1) * tk)
                rows_alive = jnp.max(r_ref[2 * g:2 * g + 2, rows, :]) >= ATT_DEAD_LOG2

                @pl.when(jnp.logical_and(jb >= 0, rows_alive))
                def _():
                    ks = pl.multiple_of(jb * tk, tk)
                    copies = [
                        pltpu.make_async_copy(
                            qkv_hbm.at[bi, pl.ds(ks, tk), pl.ds((1 + p) * att_w + g * LANES, LANES)],
                            dst, sem.at[p])
                        for p, dst in enumerate((kblk_ref, vblk_ref))]
                    for cp in copies:
                        cp.start()
                    for cp in copies:
                        cp.wait()
                    z = pair_scores(q[rows, :], *by_head(kblk_ref[...]))
                    atts = []
                    for h in range(2):
                        lb, l = softplus_parts(z[:, h * tk:(h + 1) * tk])
                        suffix, total = suffix_and_total(l)
                        r_old = r_ref[2 * g + h, rows, :]
                        atts.append(jnp.exp2(lb + suffix + r_old).astype(BF16))
                        r_ref[2 * g + h, rows, :] = r_old + total
                    acc_ref[g, rows, :] += pair_output(*atts, *by_head(vblk_ref[...]))
        return s + 1, jnp.max(r_ref[...]) >= ATT_DEAD_LOG2

    lax.while_loop(cond, body, (0, jnp.max(r_ref[...]) >= ATT_DEAD_LOG2))

    for g in range(npair):
        yatt_ref[slot, :, g * LANES:(g + 1) * LANES] = acc_ref[g]


def _attn_ffn(qkv, x2, y_lru, g_att, w_out, g_ffn, w_ffn_in, w_ffn_out, g_last, tq, tk, nleft,
              ff_chunk, final_norm):
    bsz, s, three_w = qkv.shape
    att_w = three_w // 3
    n, d = x2.shape
    d_ff = w_ffn_out.shape[0]
    tiles_per_seq = s // tq
    n_tiles = bsz * tiles_per_seq
    left = nleft * tk
    assert s % tq == 0 and tq % tk == 0 and tq // tk >= nleft and tq % left == 0
    assert att_w % LANES == 0

    def tile(j):
        ja = jnp.minimum(j, n_tiles - 1)
        return ja // tiles_per_seq, ja % tiles_per_seq

    def cur(part):
        return pl.BlockSpec((1, tq, att_w), lambda j: (*tile(j), part))

    def prev(part):
        def index(j):
            b, t = tile(j)
            return b, jnp.maximum(t * (tq // left) - 1, 0), part
        return pl.BlockSpec((1, left, att_w), index)

    row = lambda w: pl.BlockSpec((tq, w), lambda j: (jnp.maximum(j - 1, 0), 0))
    vec = lambda w: pl.BlockSpec((1, w), lambda j: (0, 0))
    whole = lambda a: pl.BlockSpec(a.shape, lambda j: (0, 0), pipeline_mode=pl.Buffered(1))
    nheads = att_w // HEAD_DIM
    return pl.pallas_call(
        functools.partial(_attn_ffn_kernel, tq=tq, tk=tk, nleft=nleft,
                          scale2=LOG2_E / math.sqrt(HEAD_DIM), tiles_per_seq=tiles_per_seq,
                          n_tiles=n_tiles, att_w=att_w, d_ff=d_ff, ff_chunk=ff_chunk,
                          final_norm=final_norm),
        grid=(n_tiles + 1,),
        in_specs=[cur(0), cur(1), prev(1), cur(2), prev(2),
                  pl.BlockSpec(memory_space=pl.ANY),
                  row(d), row(y_lru.shape[1]), vec(att_w), whole(w_out), vec(d),
                  whole(w_ffn_in), whole(w_ffn_out), vec(d)],
        out_specs=row(d),
        out_shape=jax.ShapeDtypeStruct((n, d), F32),
        scratch_shapes=[
            pltpu.VMEM((2, tq, att_w), F32),
            pltpu.VMEM((nheads, tq, tk), F32),
            pltpu.VMEM((att_w // LANES, tq, LANES), F32),
            pltpu.VMEM((tk, LANES), BF16),
            pltpu.VMEM((tk, LANES), BF16),
            pltpu.SemaphoreType.DMA((2,)),
        ],
        compiler_params=pltpu.CompilerParams(
            dimension_semantics=("arbitrary",), vmem_limit_bytes=VMEM_LIMIT_BYTES),
        name="attn_ffn",
    )(qkv, qkv, qkv, qkv, qkv, qkv, x2, y_lru, g_att, w_out, g_ffn, w_ffn_in, w_ffn_out, g_last)


def _largest_chunk(total, limit):
    best = LANES
    for c in range(LANES, limit + 1, LANES):
        if total % c == 0:
            best = c
    return best


def kernel(x, norm_mix, w_in, conv_w, conv_b, w_rg, b_rg, w_ig, b_ig, lru_lambda, norm_lru_out,
           norm_att_out, w_out, norm_ffn, w_ffn_in, w_ffn_out, norm_final):
    bsz, s, d = x.shape
    depth = w_in.shape[0]
    lru_w = conv_w.shape[2]
    att_w = norm_att_out.shape[1]
    blk_w = w_rg.shape[2]
    assert w_in.shape[2] == 2 * lru_w + 3 * att_w and att_w % LANES == 0
    assert MXU_DIM % blk_w == 0 and lru_w % MXU_DIM == 0

    n = bsz * s
    ts = 512
    assert s % ts == 0
    tq, tk, nleft = 512, 128, 2
    per = MXU_DIM // blk_w
    row1 = lambda v: v.reshape(1, -1)

    x2 = x.reshape(n, d)
    for l in range(depth):
        qkv, y_lru, w_out_b, w_ffn_in_b, w_ffn_out_b = _inproj_lru(
            x2, row1(norm_mix[l]), w_in[l].astype(BF16), conv_w[l], row1(conv_b[l]),
            _block_diag(w_rg[l], per).astype(BF16), row1(b_rg[l]),
            _block_diag(w_ig[l], per).astype(BF16), row1(b_ig[l]), row1(lru_lambda[l]),
            row1(norm_lru_out[l]), (w_out[l], w_ffn_in[l], w_ffn_out[l]), s, ts)
        d_ff = w_ffn_out.shape[1]
        x2 = _attn_ffn(qkv.reshape(bsz, s, 3 * att_w), x2, y_lru, row1(norm_att_out[l]), w_out_b,
                       row1(norm_ffn[l]), w_ffn_in_b, w_ffn_out_b, row1(norm_final),
                       tq, tk, nleft, _largest_chunk(d_ff, 512), l == depth - 1)
    return x2.reshape(bsz, s, d)
```

```python
import functools
import math

import jax
import jax.numpy as jnp
from jax import lax
from jax.experimental import pallas as pl
from jax.experimental.pallas import tpu as pltpu

F32 = jnp.float32
BF16 = jnp.bfloat16

EPS = 1e-6
LRU_C = 8.0
CONV_W = 4
HEAD_DIM = 64

LANES = 128
SUBLANES = 8
MXU_DIM = 256
VMEM_LIMIT_BYTES = 56 * 1024 * 1024

LOG2_E = 1.4426950408889634
ATT_DEAD_LOG2 = -160.0
MASKED_SCORE = 1e30
CAST_STEPS = 8


def _rms_scale(x):
    return lax.rsqrt(jnp.mean(x * x, axis=-1, keepdims=True) + EPS)


def _sigmoid(x):
    return 1.0 / (1.0 + jnp.exp(-x))


def _softplus(x):
    return jnp.maximum(x, 0.0) + jnp.log(1.0 + jnp.exp(-jnp.abs(x)))


def _gelu_tanh(x):
    c = math.sqrt(2.0 / math.pi)
    return 0.5 * x * (1.0 + jnp.tanh(c * (x + 0.044715 * (x * x * x))))


def _inproj_stages(x_ref, g_ref, w_ref, u_slot_ref, qkv_ref, *, lru_cols, col_chunk):
    x = x_ref[...]
    h = ((x * _rms_scale(x)) * g_ref[...]).astype(BF16)
    yield
    for lo in range(0, w_ref.shape[1], col_chunk):
        u = jnp.dot(h, w_ref[:, lo:lo + col_chunk], preferred_element_type=F32)
        if lo < lru_cols:
            u_slot_ref[:, lo:lo + col_chunk] = u
        else:
            qkv_ref[:, lo - lru_cols:lo - lru_cols + col_chunk] = u.astype(BF16)
        yield


def _lru_stages(first, u_ref, cw_ref, cb_ref, wrg_ref, brg_ref, wig_ref, big_ref, lam_ref, gn_ref,
                out_ref, xs_ref, hc_ref, *, ts, width, slab):
    halo = SUBLANES
    xs_ref[0:halo, :] = jnp.where(first, 0.0, xs_ref[ts:ts + halo, :])
    xs_ref[halo:halo + ts, :] = u_ref[:, 0:width]
    row = lax.broadcasted_iota(jnp.int32, (SUBLANES, width), 0)
    hprev = jnp.where(first, 0.0, hc_ref[...])
    neg_sp = -LRU_C * _softplus(-lam_ref[...])

    for s0 in range(0, ts, slab):
        xc = jnp.broadcast_to(cb_ref[...], (slab, width))
        for i in range(CONV_W):
            off = s0 + halo - (CONV_W - 1) + i
            xc = xc + xs_ref[off:off + slab, :] * cw_ref[i:i + 1, :]

        xcb = xc.astype(BF16)
        rg, ig = [], []
        for c in range(width // MXU_DIM):
            sl = slice(c * MXU_DIM, (c + 1) * MXU_DIM)
            rg.append(jnp.dot(xcb[:, sl], wrg_ref[c], preferred_element_type=F32))
            ig.append(jnp.dot(xcb[:, sl], wig_ref[c], preferred_element_type=F32))
        r = _sigmoid(jnp.concatenate(rg, axis=1) + brg_ref[...])
        ig = _sigmoid(jnp.concatenate(ig, axis=1) + big_ref[...])

        log_a = r * neg_sp
        a = jnp.exp(log_a)
        th = jnp.tanh(log_a)
        b = jnp.sqrt((-2.0 * th) / (1.0 - th)) * (ig * xc)
        yield

        hs = []
        for g in range(slab // SUBLANES):
            rows = slice(g * SUBLANES, (g + 1) * SUBLANES)
            a8, b8 = a[rows, :], b[rows, :]
            for d in (1, 2, 4):
                keep = row >= d
                b8 = jnp.where(keep, b8 + a8 * pltpu.roll(b8, d, axis=0), b8)
                a8 = jnp.where(keep, a8 * pltpu.roll(a8, d, axis=0), a8)
            h8 = b8 + a8 * hprev
            hs.append(h8)
            hprev = jnp.broadcast_to(h8[SUBLANES - 1:SUBLANES, :], (SUBLANES, width))
        y = _gelu_tanh(u_ref[s0:s0 + slab, width:2 * width]) * jnp.concatenate(hs, axis=0)
        out_ref[s0:s0 + slab, :] = ((y * _rms_scale(y)) * gn_ref[...]).astype(out_ref.dtype)
        yield
    hc_ref[...] = hprev


def _interleave(*streams):
    streams = list(streams)
    while streams:
        for st in list(streams):
            if next(st, StopIteration) is StopIteration:
                streams.remove(st)


def _inproj_lru_kernel(x_ref, gm_ref, w_ref, cw_ref, cb_ref, wrg_ref, brg_ref, wig_ref, big_ref,
                       lam_ref, gn_ref, *rest, ts, width, tiles_per_seq, col_chunk, slab,
                       cast_slabs):
    nw = len(cast_slabs)
    wsrc_refs, (qkv_ref, ylru_ref) = rest[:nw], rest[nw:nw + 2]
    wdst_refs = rest[nw + 2:2 * nw + 2]
    u_ref, xs_ref, hc_ref = rest[2 * nw + 2:]
    j = pl.program_id(0)
    for src_ref, dst_ref, n_slabs in zip(wsrc_refs, wdst_refs, cast_slabs):
        @pl.when(j < n_slabs)
        def _():
            dst_ref[...] = src_ref[...].astype(dst_ref.dtype)

    slot = j % 2
    lt = jnp.maximum(j - 1, 0)

    @pl.when(j == 0)
    def _():
        for ref in (u_ref, xs_ref, hc_ref):
            ref[...] = jnp.zeros(ref.shape, ref.dtype)

    _interleave(
        _lru_stages(lt % tiles_per_seq == 0, u_ref.at[1 - slot], cw_ref, cb_ref, wrg_ref, brg_ref,
                    wig_ref, big_ref, lam_ref, gn_ref, ylru_ref, xs_ref, hc_ref,
                    ts=ts, width=width, slab=slab),
        _inproj_stages(x_ref, gm_ref, w_ref, u_ref.at[slot], qkv_ref, lru_cols=2 * width,
                       col_chunk=col_chunk))


def _cast_slab_rows(rows, max_slabs):
    step = 2 * SUBLANES
    for slab_rows in range(step, rows + 1, step):
        if rows % slab_rows == 0 and rows // slab_rows <= max_slabs:
            return slab_rows
    raise ValueError(f"no bf16 row slab for {rows} rows in {max_slabs} steps")


def _inproj_lru(x2, gain, w_in_bf16, conv_w, conv_b, wrg_bd, b_rg, wig_bd, b_ig, lam, gain_lru,
                later_weights, seq_len, ts):
    n, d = x2.shape
    cols = w_in_bf16.shape[1]
    width = conv_w.shape[1]
    n_tiles = n // ts
    slab_rows = [_cast_slab_rows(w.shape[0], min(n_tiles, CAST_STEPS)) for w in later_weights]
    cast_slabs = tuple(w.shape[0] // r for w, r in zip(later_weights, slab_rows))
    wslab = lambda w, r: pl.BlockSpec(
        (r, w.shape[1]), lambda j: (jnp.minimum(j, w.shape[0] // r - 1), 0))
    wspecs = [wslab(w, r) for w, r in zip(later_weights, slab_rows)]
    tile = lambda lag: (lambda j: (jnp.clip(j - lag, 0, n_tiles - 1), 0))
    vec = lambda w: pl.BlockSpec((1, w), lambda j: (0, 0))
    wspec = pl.BlockSpec(wrg_bd.shape, lambda j: (0, 0, 0))
    return pl.pallas_call(
        functools.partial(_inproj_lru_kernel, ts=ts, width=width, tiles_per_seq=seq_len // ts,
                          col_chunk=MXU_DIM, slab=LANES // 2, cast_slabs=cast_slabs),
        grid=(n_tiles + 1,),
        in_specs=[
            pl.BlockSpec((ts, d), tile(0)),
            vec(d),
            pl.BlockSpec((d, cols), lambda j: (0, 0), pipeline_mode=pl.Buffered(1)),
            pl.BlockSpec((CONV_W, width), lambda j: (0, 0)),
            vec(width), wspec, vec(width), wspec, vec(width), vec(width), vec(width),
            *wspecs,
        ],
        out_specs=[
            pl.BlockSpec((ts, cols - 2 * width), tile(0)),
            pl.BlockSpec((ts, width), tile(1)),
            *wspecs,
        ],
        out_shape=[
            jax.ShapeDtypeStruct((n, cols - 2 * width), BF16),
            jax.ShapeDtypeStruct((n, width), BF16),
            *[jax.ShapeDtypeStruct(w.shape, BF16) for w in later_weights],
        ],
        scratch_shapes=[
            pltpu.VMEM((2, ts, 2 * width), F32),
            pltpu.VMEM((ts + SUBLANES, width), F32),
            pltpu.VMEM((SUBLANES, width), F32),
        ],
        compiler_params=pltpu.CompilerParams(
            dimension_semantics=("arbitrary",), vmem_limit_bytes=VMEM_LIMIT_BYTES),
        name="inproj_lru",
    )(x2, gain, w_in_bf16, conv_w, conv_b, wrg_bd, b_rg, wig_bd, b_ig, lam, gain_lru,
      *later_weights)


def _block_diag(w, per):
    nb, bw, _ = w.shape
    w = w.reshape(nb // per, per, bw, bw)
    eye = jnp.eye(per, dtype=w.dtype)
    return jnp.einsum('gpij,pq->gpiqj', w, eye).reshape(nb // per, per * bw, per * bw)


def _ffn_stages(x_ref, yl_ref, ya, ga_ref, wo_ref, gf_ref, wi_ref, wd_ref, gl_ref, out_ref, *,
                d_ff, ff_chunk, final_norm):
    ya = ((ya * _rms_scale(ya)) * ga_ref[...]).astype(BF16)
    wl = yl_ref.shape[1]
    x1 = x_ref[...] + jnp.dot(yl_ref[...], wo_ref[0:wl, :], preferred_element_type=F32)
    x1 = x1 + jnp.dot(ya, wo_ref[wl:, :], preferred_element_type=F32)
    h2 = ((x1 * _rms_scale(x1)) * gf_ref[...]).astype(BF16)
    yield
    acts = []
    for c in range(d_ff // ff_chunk):
        lo = c * ff_chunk
        gate = jnp.dot(h2, wi_ref[:, lo:lo + ff_chunk], preferred_element_type=F32)
        up = jnp.dot(h2, wi_ref[:, d_ff + lo:d_ff + lo + ff_chunk], preferred_element_type=F32)
        acts.append(((gate * _sigmoid(gate)) * up).astype(BF16))
        yield
    out = x1 + jnp.dot(jnp.concatenate(acts, axis=1), wd_ref[...], preferred_element_type=F32)
    yield
    if final_norm:
        out = (out * _rms_scale(out)) * gl_ref[...]
    out_ref[...] = out


def _attn_ffn_kernel(q_ref, kc_ref, kp_ref, vc_ref, vp_ref, qkv_hbm, x_ref, yl_ref, ga_ref,
                     wo_ref, gf_ref, wi_ref, wd_ref, gl_ref, out_ref,
                     yatt_ref, r_ref, acc_ref, kblk_ref, vblk_ref, sem,
                     *, tq, tk, nleft, scale2, tiles_per_seq, n_tiles, att_w, d_ff, ff_chunk,
                     final_norm):
    j = pl.program_id(0)
    ja = jnp.minimum(j, n_tiles - 1)
    bi = ja // tiles_per_seq
    ti = ja % tiles_per_seq
    nrb = tq // tk
    g0 = ti * nrb
    npair = att_w // LANES
    slot = j % 2

    @pl.when(j == 0)
    def _():
        yatt_ref[...] = jnp.zeros(yatt_ref.shape, F32)

    mixer = _ffn_stages(x_ref, yl_ref, yatt_ref[1 - slot], ga_ref, wo_ref, gf_ref, wi_ref, wd_ref,
                        gl_ref, out_ref, d_ff=d_ff, ff_chunk=ff_chunk, final_norm=final_norm)

    rr = lax.broadcasted_iota(jnp.int32, (2 * tk, 2 * tk), 0)
    cc = lax.broadcasted_iota(jnp.int32, (2 * tk, 2 * tk), 1)
    key_j = jnp.where(rr >= tk, rr - tk, rr)
    tri = jnp.where((cc >= tk) | (key_j > cc), 1.0, 0.0).astype(BF16)
    below_diag = (lax.broadcasted_iota(jnp.int32, (tk, tk), 1)
                  < lax.broadcasted_iota(jnp.int32, (tk, tk), 0))
    has_left = ti > 0

    def softplus_parts(z):
        lb = jnp.minimum(z, 0.0) - jnp.log(1.0 + jnp.exp2(-jnp.abs(z))) * LOG2_E
        return lb, lb - z

    def suffix_and_total(l):
        hi = l.astype(BF16)
        lo = (l - hi.astype(F32)).astype(BF16)
        cs = jnp.dot(jnp.concatenate([hi, lo], axis=1), tri, preferred_element_type=F32)
        return cs[:, :tk], cs[:, tk:]

    def pair_queries(g):
        return (q_ref[0, :, g * LANES:(g + 1) * LANES].astype(F32) * scale2).astype(BF16)

    def by_head(kv):
        lane = lax.broadcasted_iota(jnp.int32, kv.shape, 1)
        zero = jnp.zeros_like(kv)
        return jnp.where(lane < HEAD_DIM, kv, zero), jnp.where(lane >= HEAD_DIM, kv, zero)

    def pair_scores(q, k_a, k_b):
        return lax.dot_general(q, jnp.concatenate([k_a, k_b], axis=0), (((1,), (1,)), ((), ())),
                               preferred_element_type=F32)

    def pair_output(att_a, att_b, v_a, v_b):
        return jnp.dot(jnp.concatenate([att_a, att_b], axis=1),
                       jnp.concatenate([v_a, v_b], axis=0), preferred_element_type=F32)

    def attention_stages():
        nk = (nleft + 1) * tk
        for g in range(npair):
            cols = slice(g * LANES, (g + 1) * LANES)
            q = pair_queries(g)
            k_heads = by_head(jnp.concatenate([kp_ref[0, :, cols], kc_ref[0, :, cols]], axis=0))
            v_heads = by_head(jnp.concatenate([vp_ref[0, :, cols], vc_ref[0, :, cols]], axis=0))
            chains = []
            for rb in range(nrb):
                rows = slice(rb * tk, (rb + 1) * tk)
                win = slice(rb * tk, (rb + nleft + 1) * tk)
                z = pair_scores(q[rows, :], k_heads[0][win, :], k_heads[1][win, :])
                chains.append(dict(rb=rb, rows=rows, v=(v_heads[0][win, :], v_heads[1][win, :]),
                                   z=(z[:, :nk], z[:, nk:])))
            yield
            for c in chains:
                c["lb"], ls = ([], []), []
                for h in range(2):
                    for kb in range(nleft + 1):
                        zb = c["z"][h][:, kb * tk:(kb + 1) * tk]
                        if kb == nleft:
                            zb = jnp.where(below_diag, zb, -MASKED_SCORE)
                        elif c["rb"] + kb < nleft:
                            zb = jnp.where(has_left, zb, -MASKED_SCORE)
                        lb, l = softplus_parts(zb)
                        c["lb"][h].append(lb)
                        ls.append(l)
                suffix, total = suffix_and_total(jnp.concatenate(ls, axis=0))
                block = lambda a, i: a[i * tk:(i + 1) * tk, :]
                c["parts"] = tuple(
                    [(block(suffix, h * (nleft + 1) + kb), block(total, h * (nleft + 1) + kb))
                     for kb in range(nleft + 1)] for h in range(2))
            yield
            for c in chains:
                c["att"] = []
                for h in range(2):
                    r = jnp.zeros((tk, tk), F32)
                    atts = [None] * (nleft + 1)
                    for kb in reversed(range(nleft + 1)):
                        suffix, total = c["parts"][h][kb]
                        atts[kb] = jnp.exp2(c["lb"][h][kb] + suffix + r).astype(BF16)
                        r = r + total
                    c["att"].append(jnp.concatenate(atts, axis=1))
                    r_ref[2 * g + h, c["rows"], :] = r
            yield
            for c in chains:
                acc_ref[g, c["rows"], :] = pair_output(c["att"][0], c["att"][1], *c["v"])
            yield

    _interleave(attention_stages(), mixer)

    def cond(carry):
        s, alive = carry
        return jnp.logical_and(s <= g0 + (nrb - 1 - nleft - 1), alive)

    def body(carry):
        s, _ = carry
        for g in range(npair):
            q = pair_queries(g)
            for rb in range(nrb):
                jb = g0 + (rb - nleft - 1) - s
                rows = slice(rb * tk, (rb + 1) * tk)
                rows_alive = jnp.max(r_ref[2 * g:2 * g + 2, rows, :]) >= ATT_DEAD_LOG2

                @pl.when(jnp.logical_and(jb >= 0, rows_alive))
                def _():
                    ks = pl.multiple_of(jb * tk, tk)
                    copies = [
                        pltpu.make_async_copy(
                            qkv_hbm.at[bi, pl.ds(ks, tk), pl.ds((1 + p) * att_w + g * LANES, LANES)],
                            dst, sem.at[p])
                        for p, dst in enumerate((kblk_ref, vblk_ref))]
                    for cp in copies:
                        cp.start()
                    for cp in copies:
                        cp.wait()
                    z = pair_scores(q[rows, :], *by_head(kblk_ref[...]))
                    atts = []
                    for h in range(2):
                        lb, l = softplus_parts(z[:, h * tk:(h + 1) * tk])
                        suffix, total = suffix_and_total(l)
                        r_old = r_ref[2 * g + h, rows, :]
                        atts.append(jnp.exp2(lb + suffix + r_old).astype(BF16))
                        r_ref[2 * g + h, rows, :] = r_old + total
                    acc_ref[g, rows, :] += pair_output(*atts, *by_head(vblk_ref[...]))
        return s + 1, jnp.max(r_ref[...]) >= ATT_DEAD_LOG2

    lax.while_loop(cond, body, (0, jnp.max(r_ref[...]) >= ATT_DEAD_LOG2))

    for g in range(npair):
        yatt_ref[slot, :, g * LANES:(g + 1) * LANES] = acc_ref[g]


def _attn_ffn(qkv, x2, y_lru, g_att, w_out, g_ffn, w_ffn_in, w_ffn_out, g_last, tq, tk, nleft,
              ff_chunk, final_norm):
    bsz, s, three_w = qkv.shape
    att_w = three_w // 3
    n, d = x2.shape
    d_ff = w_ffn_out.shape[0]
    tiles_per_seq = s // tq
    n_tiles = bsz * tiles_per_seq
    left = nleft * tk
    assert s % tq == 0 and tq % tk == 0 and tq // tk >= nleft and tq % left == 0
    assert att_w % LANES == 0

    def tile(j):
        ja = jnp.minimum(j, n_tiles - 1)
        return ja // tiles_per_seq, ja % tiles_per_seq

    def cur(part):
        return pl.BlockSpec((1, tq, att_w), lambda j: (*tile(j), part))

    def prev(part):
        def index(j):
            b, t = tile(j)
            return b, jnp.maximum(t * (tq // left) - 1, 0), part
        return pl.BlockSpec((1, left, att_w), index)

    row = lambda w: pl.BlockSpec((tq, w), lambda j: (jnp.maximum(j - 1, 0), 0))
    vec = lambda w: pl.BlockSpec((1, w), lambda j: (0, 0))
    whole = lambda a: pl.BlockSpec(a.shape, lambda j: (0, 0), pipeline_mode=pl.Buffered(1))
    nheads = att_w // HEAD_DIM
    return pl.pallas_call(
        functools.partial(_attn_ffn_kernel, tq=tq, tk=tk, nleft=nleft,
                          scale2=LOG2_E / math.sqrt(HEAD_DIM), tiles_per_seq=tiles_per_seq,
                          n_tiles=n_tiles, att_w=att_w, d_ff=d_ff, ff_chunk=ff_chunk,
                          final_norm=final_norm),
        grid=(n_tiles + 1,),
        in_specs=[cur(0), cur(1), prev(1), cur(2), prev(2),
                  pl.BlockSpec(memory_space=pl.ANY),
                  row(d), row(y_lru.shape[1]), vec(att_w), whole(w_out), vec(d),
                  whole(w_ffn_in), whole(w_ffn_out), vec(d)],
        out_specs=row(d),
        out_shape=jax.ShapeDtypeStruct((n, d), F32),
        scratch_shapes=[
            pltpu.VMEM((2, tq, att_w), F32),
            pltpu.VMEM((nheads, tq, tk), F32),
            pltpu.VMEM((att_w // LANES, tq, LANES), F32),
            pltpu.VMEM((tk, LANES), BF16),
            pltpu.VMEM((tk, LANES), BF16),
            pltpu.SemaphoreType.DMA((2,)),
        ],
        compiler_params=pltpu.CompilerParams(
            dimension_semantics=("arbitrary",), vmem_limit_bytes=VMEM_LIMIT_BYTES),
        name="attn_ffn",
    )(qkv, qkv, qkv, qkv, qkv, qkv, x2, y_lru, g_att, w_out, g_ffn, w_ffn_in, w_ffn_out, g_last)


def _largest_chunk(total, limit):
    best = LANES
    for c in range(LANES, limit + 1, LANES):
        if total % c == 0:
            best = c
    return best


def kernel(x, norm_mix, w_in, conv_w, conv_b, w_rg, b_rg, w_ig, b_ig, lru_lambda, norm_lru_out,
           norm_att_out, w_out, norm_ffn, w_ffn_in, w_ffn_out, norm_final):
    bsz, s, d = x.shape
    depth = w_in.shape[0]
    lru_w = conv_w.shape[2]
    att_w = norm_att_out.shape[1]
    blk_w = w_rg.shape[2]
    assert w_in.shape[2] == 2 * lru_w + 3 * att_w and att_w % LANES == 0
    assert MXU_DIM % blk_w == 0 and lru_w % MXU_DIM == 0

    n = bsz * s
    ts = 512
    assert s % ts == 0
    tq, tk, nleft = 512, 128, 2
    per = MXU_DIM // blk_w
    row1 = lambda v: v.reshape(1, -1)

    x2 = x.reshape(n, d)
    for l in range(depth):
        qkv, y_lru, w_out_b, w_ffn_in_b, w_ffn_out_b = _inproj_lru(
            x2, row1(norm_mix[l]), w_in[l].astype(BF16), conv_w[l], row1(conv_b[l]),
            _block_diag(w_rg[l], per).astype(BF16), row1(b_rg[l]),
            _block_diag(w_ig[l], per).astype(BF16), row1(b_ig[l]), row1(lru_lambda[l]),
            row1(norm_lru_out[l]), (w_out[l], w_ffn_in[l], w_ffn_out[l]), s, ts)
        d_ff = w_ffn_out.shape[1]
        x2 = _attn_ffn(qkv.reshape(bsz, s, 3 * att_w), x2, y_lru, row1(norm_att_out[l]), w_out_b,
                       row1(norm_ffn[l]), w_ffn_in_b, w_ffn_out_b, row1(norm_final),
                       tq, tk, nleft, _largest_chunk(d_ff, 512), l == depth - 1)
    return x2.reshape(bsz, s, d)
```

```python
import functools
import math

import jax
import jax.numpy as jnp
from jax import lax
from jax.experimental import pallas as pl
from jax.experimental.pallas import tpu as pltpu

F32 = jnp.float32
BF16 = jnp.bfloat16

EPS = 1e-6
LRU_C = 8.0
CONV_W = 4
HEAD_DIM = 64

LANES = 128
SUBLANES = 8
MXU_DIM = 256
VMEM_LIMIT_BYTES = 56 * 1024 * 1024

LOG2_E = 1.4426950408889634
ATT_DEAD_LOG2 = -160.0
MASKED_SCORE = 1e30
CAST_STEPS = 8


def _rms_scale(x):
    return lax.rsqrt(jnp.mean(x * x, axis=-1, keepdims=True) + EPS)


def _sigmoid(x):
    return 1.0 / (1.0 + jnp.exp(-x))


def _softplus(x):
    return jnp.maximum(x, 0.0) + jnp.log(1.0 + jnp.exp(-jnp.abs(x)))


def _gelu_tanh(x):
    c = math.sqrt(2.0 / math.pi)
    return 0.5 * x * (1.0 + jnp.tanh(c * (x + 0.044715 * (x * x * x))))


def _inproj_stages(x_ref, g_ref, w_ref, u_slot_ref, qkv_ref, *, lru_cols, col_chunk):
    x = x_ref[...]
    h = ((x * _rms_scale(x)) * g_ref[...]).astype(BF16)
    yield
    for lo in range(0, w_ref.shape[1], col_chunk):
        u = jnp.dot(h, w_ref[:, lo:lo + col_chunk], preferred_element_type=F32)
        if lo < lru_cols:
            u_slot_ref[:, lo:lo + col_chunk] = u
        else:
            qkv_ref[:, lo - lru_cols:lo - lru_cols + col_chunk] = u.astype(BF16)
        yield


def _lru_stages(first, u_ref, cw_ref, cb_ref, wrg_ref, brg_ref, wig_ref, big_ref, lam_ref, gn_ref,
                out_ref, xs_ref, hc_ref, *, ts, width, slab):
    halo = SUBLANES
    xs_ref[0:halo, :] = jnp.where(first, 0.0, xs_ref[ts:ts + halo, :])
    xs_ref[halo:halo + ts, :] = u_ref[:, 0:width]
    row = lax.broadcasted_iota(jnp.int32, (SUBLANES, width), 0)
    hprev = jnp.where(first, 0.0, hc_ref[...])
    neg_sp = -LRU_C * _softplus(-lam_ref[...])

    for s0 in range(0, ts, slab):
        xc = jnp.broadcast_to(cb_ref[...], (slab, width))
        for i in range(CONV_W):
            off = s0 + halo - (CONV_W - 1) + i
            xc = xc + xs_ref[off:off + slab, :] * cw_ref[i:i + 1, :]

        xcb = xc.astype(BF16)
        rg, ig = [], []
        for c in range(width // MXU_DIM):
            sl = slice(c * MXU_DIM, (c + 1) * MXU_DIM)
            gates = jnp.dot(xcb[:, sl], jnp.concatenate([wrg_ref[c], wig_ref[c]], axis=1),
                            preferred_element_type=F32)
            rg.append(gates[:, :MXU_DIM])
            ig.append(gates[:, MXU_DIM:])
        r = _sigmoid(jnp.concatenate(rg, axis=1) + brg_ref[...])
        ig = _sigmoid(jnp.concatenate(ig, axis=1) + big_ref[...])

        log_a = r * neg_sp
        a = jnp.exp(log_a)
        th = jnp.tanh(log_a)
        b = jnp.sqrt((-2.0 * th) / (1.0 - th)) * (ig * xc)
        yield

        hs = []
        for g in range(slab // SUBLANES):
            rows = slice(g * SUBLANES, (g + 1) * SUBLANES)
            a8, b8 = a[rows, :], b[rows, :]
            for d in (1, 2, 4):
                keep = row >= d
                b8 = jnp.where(keep, b8 + a8 * pltpu.roll(b8, d, axis=0), b8)
                a8 = jnp.where(keep, a8 * pltpu.roll(a8, d, axis=0), a8)
            h8 = b8 + a8 * hprev
            hs.append(h8)
            hprev = jnp.broadcast_to(h8[SUBLANES - 1:SUBLANES, :], (SUBLANES, width))
        y = _gelu_tanh(u_ref[s0:s0 + slab, width:2 * width]) * jnp.concatenate(hs, axis=0)
        out_ref[s0:s0 + slab, :] = ((y * _rms_scale(y)) * gn_ref[...]).astype(out_ref.dtype)
        yield
    hc_ref[...] = hprev


def _interleave(*streams):
    streams = list(streams)
    while streams:
        for st in list(streams):
            if next(st, StopIteration) is StopIteration:
                streams.remove(st)


def _inproj_lru_kernel(x_ref, gm_ref, w_ref, cw_ref, cb_ref, wrg_ref, brg_ref, wig_ref, big_ref,
                       lam_ref, gn_ref, *rest, ts, width, tiles_per_seq, col_chunk, slab,
                       cast_slabs):
    nw = len(cast_slabs)
    wsrc_refs, (qkv_ref, ylru_ref) = rest[:nw], rest[nw:nw + 2]
    wdst_refs = rest[nw + 2:2 * nw + 2]
    u_ref, xs_ref, hc_ref = rest[2 * nw + 2:]
    j = pl.program_id(0)
    for src_ref, dst_ref, n_slabs in zip(wsrc_refs, wdst_refs, cast_slabs):
        @pl.when(j < n_slabs)
        def _():
            dst_ref[...] = src_ref[...].astype(dst_ref.dtype)

    slot = j % 2
    lt = jnp.maximum(j - 1, 0)

    @pl.when(j == 0)
    def _():
        for ref in (u_ref, xs_ref, hc_ref):
            ref[...] = jnp.zeros(ref.shape, ref.dtype)

    _interleave(
        _lru_stages(lt % tiles_per_seq == 0, u_ref.at[1 - slot], cw_ref, cb_ref, wrg_ref, brg_ref,
                    wig_ref, big_ref, lam_ref, gn_ref, ylru_ref, xs_ref, hc_ref,
                    ts=ts, width=width, slab=slab),
        _inproj_stages(x_ref, gm_ref, w_ref, u_ref.at[slot], qkv_ref, lru_cols=2 * width,
                       col_chunk=col_chunk))


def _cast_slab_rows(rows, max_slabs):
    step = 2 * SUBLANES
    for slab_rows in range(step, rows + 1, step):
        if rows % slab_rows == 0 and rows // slab_rows <= max_slabs:
            return slab_rows
    raise ValueError(f"no bf16 row slab for {rows} rows in {max_slabs} steps")


def _inproj_lru(x2, gain, w_in_bf16, conv_w, conv_b, wrg_bd, b_rg, wig_bd, b_ig, lam, gain_lru,
                later_weights, seq_len, ts):
    n, d = x2.shape
    cols = w_in_bf16.shape[1]
    width = conv_w.shape[1]
    n_tiles = n // ts
    slab_rows = [_cast_slab_rows(w.shape[0], min(n_tiles, CAST_STEPS)) for w in later_weights]
    cast_slabs = tuple(w.shape[0] // r for w, r in zip(later_weights, slab_rows))
    wslab = lambda w, r: pl.BlockSpec(
        (r, w.shape[1]), lambda j: (jnp.minimum(j, w.shape[0] // r - 1), 0))
    wspecs = [wslab(w, r) for w, r in zip(later_weights, slab_rows)]
    tile = lambda lag: (lambda j: (jnp.clip(j - lag, 0, n_tiles - 1), 0))
    vec = lambda w: pl.BlockSpec((1, w), lambda j: (0, 0))
    wspec = pl.BlockSpec(wrg_bd.shape, lambda j: (0, 0, 0))
    return pl.pallas_call(
        functools.partial(_inproj_lru_kernel, ts=ts, width=width, tiles_per_seq=seq_len // ts,
                          col_chunk=MXU_DIM, slab=LANES // 2, cast_slabs=cast_slabs),
        grid=(n_tiles + 1,),
        in_specs=[
            pl.BlockSpec((ts, d), tile(0)),
            vec(d),
            pl.BlockSpec((d, cols), lambda j: (0, 0), pipeline_mode=pl.Buffered(1)),
            pl.BlockSpec((CONV_W, width), lambda j: (0, 0)),
            vec(width), wspec, vec(width), wspec, vec(width), vec(width), vec(width),
            *wspecs,
        ],
        out_specs=[
            pl.BlockSpec((ts, cols - 2 * width), tile(0)),
            pl.BlockSpec((ts, width), tile(1)),
            *wspecs,
        ],
        out_shape=[
            jax.ShapeDtypeStruct((n, cols - 2 * width), BF16),
            jax.ShapeDtypeStruct((n, width), BF16),
            *[jax.ShapeDtypeStruct(w.shape, BF16) for w in later_weights],
        ],
        scratch_shapes=[
            pltpu.VMEM((2, ts, 2 * width), F32),
            pltpu.VMEM((ts + SUBLANES, width), F32),
            pltpu.VMEM((SUBLANES, width), F32),
        ],
        compiler_params=pltpu.CompilerParams(
            dimension_semantics=("arbitrary",), vmem_limit_bytes=VMEM_LIMIT_BYTES),
        name="inproj_lru",
    )(x2, gain, w_in_bf16, conv_w, conv_b, wrg_bd, b_rg, wig_bd, b_ig, lam, gain_lru,
      *later_weights)


def _block_diag(w, per):
    nb, bw, _ = w.shape
    w = w.reshape(nb // per, per, bw, bw)
    eye = jnp.eye(per, dtype=w.dtype)
    return jnp.einsum('gpij,pq->gpiqj', w, eye).reshape(nb // per, per * bw, per * bw)


def _ffn_stages(x_ref, yl_ref, ya, ga_ref, wo_ref, gf_ref, wi_ref, wd_ref, gl_ref, out_ref, *,
                d_ff, ff_chunk, final_norm):
    ya = ((ya * _rms_scale(ya)) * ga_ref[...]).astype(BF16)
    wl = yl_ref.shape[1]
    x1 = x_ref[...] + jnp.dot(yl_ref[...], wo_ref[0:wl, :], preferred_element_type=F32)
    x1 = x1 + jnp.dot(ya, wo_ref[wl:, :], preferred_element_type=F32)
    h2 = ((x1 * _rms_scale(x1)) * gf_ref[...]).astype(BF16)
    yield
    acts = []
    for c in range(d_ff // ff_chunk):
        lo = c * ff_chunk
        gate = jnp.dot(h2, wi_ref[:, lo:lo + ff_chunk], preferred_element_type=F32)
        up = jnp.dot(h2, wi_ref[:, d_ff + lo:d_ff + lo + ff_chunk], preferred_element_type=F32)
        acts.append(((gate * _sigmoid(gate)) * up).astype(BF16))
        yield
    out = x1 + jnp.dot(jnp.concatenate(acts, axis=1), wd_ref[...], preferred_element_type=F32)
    yield
    if final_norm:
        out = (out * _rms_scale(out)) * gl_ref[...]
    out_ref[...] = out


def _attn_ffn_kernel(q_ref, kc_ref, kp_ref, vc_ref, vp_ref, qkv_hbm, x_ref, yl_ref, ga_ref,
                     wo_ref, gf_ref, wi_ref, wd_ref, gl_ref, out_ref,
                     yatt_ref, r_ref, acc_ref, kblk_ref, vblk_ref, sem,
                     *, tq, tk, nleft, scale2, tiles_per_seq, n_tiles, att_w, d_ff, ff_chunk,
                     final_norm):
    j = pl.program_id(0)
    ja = jnp.minimum(j, n_tiles - 1)
    bi = ja // tiles_per_seq
    ti = ja % tiles_per_seq
    nrb = tq // tk
    g0 = ti * nrb
    npair = att_w // LANES
    slot = j % 2

    @pl.when(j == 0)
    def _():
        yatt_ref[...] = jnp.zeros(yatt_ref.shape, F32)

    mixer = _ffn_stages(x_ref, yl_ref, yatt_ref[1 - slot], ga_ref, wo_ref, gf_ref, wi_ref, wd_ref,
                        gl_ref, out_ref, d_ff=d_ff, ff_chunk=ff_chunk, final_norm=final_norm)

    rr = lax.broadcasted_iota(jnp.int32, (2 * tk, 2 * tk), 0)
    cc = lax.broadcasted_iota(jnp.int32, (2 * tk, 2 * tk), 1)
    key_j = jnp.where(rr >= tk, rr - tk, rr)
    tri = jnp.where((cc >= tk) | (key_j > cc), 1.0, 0.0).astype(BF16)
    below_diag = (lax.broadcasted_iota(jnp.int32, (tk, tk), 1)
                  < lax.broadcasted_iota(jnp.int32, (tk, tk), 0))
    has_left = ti > 0

    def softplus_parts(z):
        lb = jnp.minimum(z, 0.0) - jnp.log(1.0 + jnp.exp2(-jnp.abs(z))) * LOG2_E
        return lb, lb - z

    def suffix_and_total(l):
        hi = l.astype(BF16)
        lo = (l - hi.astype(F32)).astype(BF16)
        cs = jnp.dot(jnp.concatenate([hi, lo], axis=1), tri, preferred_element_type=F32)
        return cs[:, :tk], cs[:, tk:]

    def pair_queries(g):
        return (q_ref[0, :, g * LANES:(g + 1) * LANES].astype(F32) * scale2).astype(BF16)

    def by_head(kv):
        lane = lax.broadcasted_iota(jnp.int32, kv.shape, 1)
        zero = jnp.zeros_like(kv)
        return jnp.where(lane < HEAD_DIM, kv, zero), jnp.where(lane >= HEAD_DIM, kv, zero)

    def pair_scores(q, k_a, k_b):
        return lax.dot_general(q, jnp.concatenate([k_a, k_b], axis=0), (((1,), (1,)), ((), ())),
                               preferred_element_type=F32)

    def pair_output(att_a, att_b, v_a, v_b):
        return jnp.dot(jnp.concatenate([att_a, att_b], axis=1),
                       jnp.concatenate([v_a, v_b], axis=0), preferred_element_type=F32)

    def attention_stages():
        nk = (nleft + 1) * tk
        for g in range(npair):
            cols = slice(g * LANES, (g + 1) * LANES)
            q = pair_queries(g)
            k_heads = by_head(jnp.concatenate([kp_ref[0, :, cols], kc_ref[0, :, cols]], axis=0))
            v_heads = by_head(jnp.concatenate([vp_ref[0, :, cols], vc_ref[0, :, cols]], axis=0))
            chains = []
            for rb in range(nrb):
                rows = slice(rb * tk, (rb + 1) * tk)
                win = slice(rb * tk, (rb + nleft + 1) * tk)
                z = pair_scores(q[rows, :], k_heads[0][win, :], k_heads[1][win, :])
                chains.append(dict(rb=rb, rows=rows, v=(v_heads[0][win, :], v_heads[1][win, :]),
                                   z=(z[:, :nk], z[:, nk:])))
            yield
            for c in chains:
                c["lb"], c["parts"] = ([], []), ([], [])
                for h in range(2):
                    for kb in range(nleft + 1):
                        zb = c["z"][h][:, kb * tk:(kb + 1) * tk]
                        if kb == nleft:
                            zb = jnp.where(below_diag, zb, -MASKED_SCORE)
                        elif c["rb"] + kb < nleft:
                            zb = jnp.where(has_left, zb, -MASKED_SCORE)
                        lb, l = softplus_parts(zb)
                        c["lb"][h].append(lb)
                        c["parts"][h].append(suffix_and_total(l))
            yield
            for c in chains:
                c["att"] = []
                for h in range(2):
                    r = jnp.zeros((tk, tk), F32)
                    atts = [None] * (nleft + 1)
                    for kb in reversed(range(nleft + 1)):
                        suffix, total = c["parts"][h][kb]
                        atts[kb] = jnp.exp2(c["lb"][h][kb] + suffix + r).astype(BF16)
                        r = r + total
                    c["att"].append(jnp.concatenate(atts, axis=1))
                    r_ref[2 * g + h, c["rows"], :] = r
            yield
            for c in chains:
                acc_ref[g, c["rows"], :] = pair_output(c["att"][0], c["att"][1], *c["v"])
            yield

    _interleave(attention_stages(), mixer)

    def cond(carry):
        s, alive = carry
        return jnp.logical_and(s <= g0 + (nrb - 1 - nleft - 1), alive)

    def body(carry):
        s, _ = carry
        for g in range(npair):
            q = pair_queries(g)
            for rb in range(nrb):
                jb = g0 + (rb - nleft - 1) - s
                rows = slice(rb * tk, (rb + 1) * tk)
                rows_alive = jnp.max(r_ref[2 * g:2 * g + 2, rows, :]) >= ATT_DEAD_LOG2

                @pl.when(jnp.logical_and(jb >= 0, rows_alive))
                def _():
                    ks = pl.multiple_of(jb * tk, tk)
                    copies = [
                        pltpu.make_async_copy(
                            qkv_hbm.at[bi, pl.ds(ks, tk), pl.ds((1 + p) * att_w + g * LANES, LANES)],
                            dst, sem.at[p])
                        for p, dst in enumerate((kblk_ref, vblk_ref))]
                    for cp in copies:
                        cp.start()
                    for cp in copies:
                        cp.wait()
                    z = pair_scores(q[rows, :], *by_head(kblk_ref[...]))
                    atts = []
                    for h in range(2):
                        lb, l = softplus_parts(z[:, h * tk:(h + 1) * tk])
                        suffix, total = suffix_and_total(l)
                        r_old = r_ref[2 * g + h, rows, :]
                        atts.append(jnp.exp2(lb + suffix + r_old).astype(BF16))
                        r_ref[2 * g + h, rows, :] = r_old + total
                    acc_ref[g, rows, :] += pair_output(*atts, *by_head(vblk_ref[...]))
        return s + 1, jnp.max(r_ref[...]) >= ATT_DEAD_LOG2

    lax.while_loop(cond, body, (0, jnp.max(r_ref[...]) >= ATT_DEAD_LOG2))

    for g in range(npair):
        yatt_ref[slot, :, g * LANES:(g + 1) * LANES] = acc_ref[g]


def _attn_ffn(qkv, x2, y_lru, g_att, w_out, g_ffn, w_ffn_in, w_ffn_out, g_last, tq, tk, nleft,
              ff_chunk, final_norm):
    bsz, s, three_w = qkv.shape
    att_w = three_w // 3
    n, d = x2.shape
    d_ff = w_ffn_out.shape[0]
    tiles_per_seq = s // tq
    n_tiles = bsz * tiles_per_seq
    left = nleft * tk
    assert s % tq == 0 and tq % tk == 0 and tq // tk >= nleft and tq % left == 0
    assert att_w % LANES == 0

    def tile(j):
        ja = jnp.minimum(j, n_tiles - 1)
        return ja // tiles_per_seq, ja % tiles_per_seq

    def cur(part):
        return pl.BlockSpec((1, tq, att_w), lambda j: (*tile(j), part))

    def prev(part):
        def index(j):
            b, t = tile(j)
            return b, jnp.maximum(t * (tq // left) - 1, 0), part
        return pl.BlockSpec((1, left, att_w), index)

    row = lambda w: pl.BlockSpec((tq, w), lambda j: (jnp.maximum(j - 1, 0), 0))
    vec = lambda w: pl.BlockSpec((1, w), lambda j: (0, 0))
    whole = lambda a: pl.BlockSpec(a.shape, lambda j: (0, 0), pipeline_mode=pl.Buffered(1))
    nheads = att_w // HEAD_DIM
    return pl.pallas_call(
        functools.partial(_attn_ffn_kernel, tq=tq, tk=tk, nleft=nleft,
                          scale2=LOG2_E / math.sqrt(HEAD_DIM), tiles_per_seq=tiles_per_seq,
                          n_tiles=n_tiles, att_w=att_w, d_ff=d_ff, ff_chunk=ff_chunk,
                          final_norm=final_norm),
        grid=(n_tiles + 1,),
        in_specs=[cur(0), cur(1), prev(1), cur(2), prev(2),
                  pl.BlockSpec(memory_space=pl.ANY),
                  row(d), row(y_lru.shape[1]), vec(att_w), whole(w_out), vec(d),
                  whole(w_ffn_in), whole(w_ffn_out), vec(d)],
        out_specs=row(d),
        out_shape=jax.ShapeDtypeStruct((n, d), F32),
        scratch_shapes=[
            pltpu.VMEM((2, tq, att_w), F32),
            pltpu.VMEM((nheads, tq, tk), F32),
            pltpu.VMEM((att_w // LANES, tq, LANES), F32),
            pltpu.VMEM((tk, LANES), BF16),
            pltpu.VMEM((tk, LANES), BF16),
            pltpu.SemaphoreType.DMA((2,)),
        ],
        compiler_params=pltpu.CompilerParams(
            dimension_semantics=("arbitrary",), vmem_limit_bytes=VMEM_LIMIT_BYTES),
        name="attn_ffn",
    )(qkv, qkv, qkv, qkv, qkv, qkv, x2, y_lru, g_att, w_out, g_ffn, w_ffn_in, w_ffn_out, g_last)


def _largest_chunk(total, limit):
    best = LANES
    for c in range(LANES, limit + 1, LANES):
        if total % c == 0:
            best = c
    return best


def kernel(x, norm_mix, w_in, conv_w, conv_b, w_rg, b_rg, w_ig, b_ig, lru_lambda, norm_lru_out,
           norm_att_out, w_out, norm_ffn, w_ffn_in, w_ffn_out, norm_final):
    bsz, s, d = x.shape
    depth = w_in.shape[0]
    lru_w = conv_w.shape[2]
    att_w = norm_att_out.shape[1]
    blk_w = w_rg.shape[2]
    assert w_in.shape[2] == 2 * lru_w + 3 * att_w and att_w % LANES == 0
    assert MXU_DIM % blk_w == 0 and lru_w % MXU_DIM == 0

    n = bsz * s
    ts = 512
    assert s % ts == 0
    tq, tk, nleft = 512, 128, 2
    per = MXU_DIM // blk_w
    row1 = lambda v: v.reshape(1, -1)

    x2 = x.reshape(n, d)
    for l in range(depth):
        qkv, y_lru, w_out_b, w_ffn_in_b, w_ffn_out_b = _inproj_lru(
            x2, row1(norm_mix[l]), w_in[l].astype(BF16), conv_w[l], row1(conv_b[l]),
            _block_diag(w_rg[l], per).astype(BF16), row1(b_rg[l]),
            _block_diag(w_ig[l], per).astype(BF16), row1(b_ig[l]), row1(lru_lambda[l]),
            row1(norm_lru_out[l]), (w_out[l], w_ffn_in[l], w_ffn_out[l]), s, ts)
        d_ff = w_ffn_out.shape[1]
        x2 = _attn_ffn(qkv.reshape(bsz, s, 3 * att_w), x2, y_lru, row1(norm_att_out[l]), w_out_b,
                       row1(norm_ffn[l]), w_ffn_in_b, w_ffn_out_b, row1(norm_final),
                       tq, tk, nleft, _largest_chunk(d_ff, 512), l == depth - 1)
    return x2.reshape(bsz, s, d)
```

```python
import functools
import math

import jax
import jax.numpy as jnp
from jax import lax
from jax.experimental import pallas as pl
from jax.experimental.pallas import tpu as pltpu

F32 = jnp.float32
BF16 = jnp.bfloat16

EPS = 1e-6
LRU_C = 8.0
CONV_W = 4
HEAD_DIM = 64

LANES = 128
SUBLANES = 8
MXU_DIM = 256
VMEM_LIMIT_BYTES = 56 * 1024 * 1024

LOG2_E = 1.4426950408889634
ATT_DEAD_LOG2 = -160.0
MASKED_SCORE = 1e30
CAST_STEPS = 8


def _rms_scale(x):
    return lax.rsqrt(jnp.mean(x * x, axis=-1, keepdims=True) + EPS)


def _sigmoid(x):
    return 1.0 / (1.0 + jnp.exp(-x))


def _softplus(x):
    return jnp.maximum(x, 0.0) + jnp.log(1.0 + jnp.exp(-jnp.abs(x)))


def _gelu_tanh(x):
    c = math.sqrt(2.0 / math.pi)
    return 0.5 * x * (1.0 + jnp.tanh(c * (x + 0.044715 * (x * x * x))))


def _inproj_stages(x_ref, g_ref, w_ref, u_slot_ref, qkv_ref, *, lru_cols, col_chunk):
    x = x_ref[...]
    h = ((x * _rms_scale(x)) * g_ref[...]).astype(BF16)
    yield
    for lo in range(0, w_ref.shape[1], col_chunk):
        u = jnp.dot(h, w_ref[:, lo:lo + col_chunk], preferred_element_type=F32)
        if lo < lru_cols:
            u_slot_ref[:, lo:lo + col_chunk] = u
        else:
            qkv_ref[:, lo - lru_cols:lo - lru_cols + col_chunk] = u.astype(BF16)
        yield


def _lru_stages(first, u_ref, cw_ref, cb_ref, wrg_ref, brg_ref, wig_ref, big_ref, lam_ref, gn_ref,
                out_ref, xs_ref, hc_ref, *, ts, width, slab):
    halo = SUBLANES
    xs_ref[0:halo, :] = jnp.where(first, 0.0, xs_ref[ts:ts + halo, :])
    xs_ref[halo:halo + ts, :] = u_ref[:, 0:width]
    row = lax.broadcasted_iota(jnp.int32, (SUBLANES, width), 0)
    hprev = jnp.where(first, 0.0, hc_ref[...])
    neg_sp = -LRU_C * _softplus(-lam_ref[...])

    for s0 in range(0, ts, slab):
        xc = jnp.broadcast_to(cb_ref[...], (slab, width))
        for i in range(CONV_W):
            off = s0 + halo - (CONV_W - 1) + i
            xc = xc + xs_ref[off:off + slab, :] * cw_ref[i:i + 1, :]

        xcb = xc.astype(BF16)
        rg, ig = [], []
        for c in range(width // MXU_DIM):
            sl = slice(c * MXU_DIM, (c + 1) * MXU_DIM)
            gates = jnp.dot(xcb[:, sl], jnp.concatenate([wrg_ref[c], wig_ref[c]], axis=1),
                            preferred_element_type=F32)
            rg.append(gates[:, :MXU_DIM])
            ig.append(gates[:, MXU_DIM:])
        r = _sigmoid(jnp.concatenate(rg, axis=1) + brg_ref[...])
        ig = _sigmoid(jnp.concatenate(ig, axis=1) + big_ref[...])

        log_a = r * neg_sp
        a = jnp.exp(log_a)
        th = jnp.tanh(log_a)
        b = jnp.sqrt((-2.0 * th) / (1.0 - th)) * (ig * xc)
        yield

        hs = []
        for g in range(slab // SUBLANES):
            rows = slice(g * SUBLANES, (g + 1) * SUBLANES)
            a8, b8 = a[rows, :], b[rows, :]
            for d in (1, 2, 4):
                keep = row >= d
                b8 = jnp.where(keep, b8 + a8 * pltpu.roll(b8, d, axis=0), b8)
                a8 = jnp.where(keep, a8 * pltpu.roll(a8, d, axis=0), a8)
            h8 = b8 + a8 * hprev
            hs.append(h8)
            hprev = jnp.broadcast_to(h8[SUBLANES - 1:SUBLANES, :], (SUBLANES, width))
        y = _gelu_tanh(u_ref[s0:s0 + slab, width:2 * width]) * jnp.concatenate(hs, axis=0)
        out_ref[s0:s0 + slab, :] = ((y * _rms_scale(y)) * gn_ref[...]).astype(out_ref.dtype)
        yield
    hc_ref[...] = hprev


def _interleave(*streams):
    streams = list(streams)
    while streams:
        for st in list(streams):
            if next(st, StopIteration) is StopIteration:
                streams.remove(st)


def _inproj_lru_kernel(x_ref, gm_ref, w_ref, cw_ref, cb_ref, wrg_ref, brg_ref, wig_ref, big_ref,
                       lam_ref, gn_ref, *rest, ts, width, tiles_per_seq, col_chunk, slab,
                       cast_slabs):
    nw = len(cast_slabs)
    wsrc_refs, (qkv_ref, ylru_ref) = rest[:nw], rest[nw:nw + 2]
    wdst_refs = rest[nw + 2:2 * nw + 2]
    u_ref, xs_ref, hc_ref = rest[2 * nw + 2:]
    j = pl.program_id(0)
    for src_ref, dst_ref, n_slabs in zip(wsrc_refs, wdst_refs, cast_slabs):
        @pl.when(j < n_slabs)
        def _():
            dst_ref[...] = src_ref[...].astype(dst_ref.dtype)

    slot = j % 2
    lt = jnp.maximum(j - 1, 0)

    @pl.when(j == 0)
    def _():
        for ref in (u_ref, xs_ref, hc_ref):
            ref[...] = jnp.zeros(ref.shape, ref.dtype)

    _interleave(
        _lru_stages(lt % tiles_per_seq == 0, u_ref.at[1 - slot], cw_ref, cb_ref, wrg_ref, brg_ref,
                    wig_ref, big_ref, lam_ref, gn_ref, ylru_ref, xs_ref, hc_ref,
                    ts=ts, width=width, slab=slab),
        _inproj_stages(x_ref, gm_ref, w_ref, u_ref.at[slot], qkv_ref, lru_cols=2 * width,
                       col_chunk=col_chunk))


def _cast_slab_rows(rows, max_slabs):
    step = 2 * SUBLANES
    for slab_rows in range(step, rows + 1, step):
        if rows % slab_rows == 0 and rows // slab_rows <= max_slabs:
            return slab_rows
    raise ValueError(f"no bf16 row slab for {rows} rows in {max_slabs} steps")


def _inproj_lru(x2, gain, w_in_bf16, conv_w, conv_b, wrg_bd, b_rg, wig_bd, b_ig, lam, gain_lru,
                later_weights, seq_len, ts):
    n, d = x2.shape
    cols = w_in_bf16.shape[1]
    width = conv_w.shape[1]
    n_tiles = n // ts
    slab_rows = [_cast_slab_rows(w.shape[0], min(n_tiles, CAST_STEPS)) for w in later_weights]
    cast_slabs = tuple(w.shape[0] // r for w, r in zip(later_weights, slab_rows))
    wslab = lambda w, r: pl.BlockSpec(
        (r, w.shape[1]), lambda j: (jnp.minimum(j, w.shape[0] // r - 1), 0))
    wspecs = [wslab(w, r) for w, r in zip(later_weights, slab_rows)]
    tile = lambda lag: (lambda j: (jnp.clip(j - lag, 0, n_tiles - 1), 0))
    vec = lambda w: pl.BlockSpec((1, w), lambda j: (0, 0))
    wspec = pl.BlockSpec(wrg_bd.shape, lambda j: (0, 0, 0))
    return pl.pallas_call(
        functools.partial(_inproj_lru_kernel, ts=ts, width=width, tiles_per_seq=seq_len // ts,
                          col_chunk=MXU_DIM, slab=LANES, cast_slabs=cast_slabs),
        grid=(n_tiles + 1,),
        in_specs=[
            pl.BlockSpec((ts, d), tile(0)),
            vec(d),
            pl.BlockSpec((d, cols), lambda j: (0, 0), pipeline_mode=pl.Buffered(1)),
            pl.BlockSpec((CONV_W, width), lambda j: (0, 0)),
            vec(width), wspec, vec(width), wspec, vec(width), vec(width), vec(width),
            *wspecs,
        ],
        out_specs=[
            pl.BlockSpec((ts, cols - 2 * width), tile(0)),
            pl.BlockSpec((ts, width), tile(1)),
            *wspecs,
        ],
        out_shape=[
            jax.ShapeDtypeStruct((n, cols - 2 * width), BF16),
            jax.ShapeDtypeStruct((n, width), BF16),
            *[jax.ShapeDtypeStruct(w.shape, BF16) for w in later_weights],
        ],
        scratch_shapes=[
            pltpu.VMEM((2, ts, 2 * width), F32),
            pltpu.VMEM((ts + SUBLANES, width), F32),
            pltpu.VMEM((SUBLANES, width), F32),
        ],
        compiler_params=pltpu.CompilerParams(
            dimension_semantics=("arbitrary",), vmem_limit_bytes=VMEM_LIMIT_BYTES),
        name="inproj_lru",
    )(x2, gain, w_in_bf16, conv_w, conv_b, wrg_bd, b_rg, wig_bd, b_ig, lam, gain_lru,
      *later_weights)


def _block_diag(w, per):
    nb, bw, _ = w.shape
    w = w.reshape(nb // per, per, bw, bw)
    eye = jnp.eye(per, dtype=w.dtype)
    return jnp.einsum('gpij,pq->gpiqj', w, eye).reshape(nb // per, per * bw, per * bw)


def _ffn_stages(x_ref, yl_ref, ya, ga_ref, wo_ref, gf_ref, wi_ref, wd_ref, gl_ref, out_ref, *,
                d_ff, ff_chunk, final_norm):
    ya = ((ya * _rms_scale(ya)) * ga_ref[...]).astype(BF16)
    wl = yl_ref.shape[1]
    x1 = x_ref[...] + jnp.dot(yl_ref[...], wo_ref[0:wl, :], preferred_element_type=F32)
    x1 = x1 + jnp.dot(ya, wo_ref[wl:, :], preferred_element_type=F32)
    h2 = ((x1 * _rms_scale(x1)) * gf_ref[...]).astype(BF16)
    yield
    acts = []
    for c in range(d_ff // ff_chunk):
        lo = c * ff_chunk
        gate = jnp.dot(h2, wi_ref[:, lo:lo + ff_chunk], preferred_element_type=F32)
        up = jnp.dot(h2, wi_ref[:, d_ff + lo:d_ff + lo + ff_chunk], preferred_element_type=F32)
        acts.append(((gate * _sigmoid(gate)) * up).astype(BF16))
        yield
    out = x1 + jnp.dot(jnp.concatenate(acts, axis=1), wd_ref[...], preferred_element_type=F32)
    yield
    if final_norm:
        out = (out * _rms_scale(out)) * gl_ref[...]
    out_ref[...] = out


def _attn_ffn_kernel(q_ref, kc_ref, kp_ref, vc_ref, vp_ref, qkv_hbm, x_ref, yl_ref, ga_ref,
                     wo_ref, gf_ref, wi_ref, wd_ref, gl_ref, out_ref,
                     yatt_ref, r_ref, acc_ref, kblk_ref, vblk_ref, sem,
                     *, tq, tk, nleft, scale2, tiles_per_seq, n_tiles, att_w, d_ff, ff_chunk,
                     final_norm):
    j = pl.program_id(0)
    ja = jnp.minimum(j, n_tiles - 1)
    bi = ja // tiles_per_seq
    ti = ja % tiles_per_seq
    nrb = tq // tk
    g0 = ti * nrb
    npair = att_w // LANES
    slot = j % 2

    @pl.when(j == 0)
    def _():
        yatt_ref[...] = jnp.zeros(yatt_ref.shape, F32)

    mixer = _ffn_stages(x_ref, yl_ref, yatt_ref[1 - slot], ga_ref, wo_ref, gf_ref, wi_ref, wd_ref,
                        gl_ref, out_ref, d_ff=d_ff, ff_chunk=ff_chunk, final_norm=final_norm)

    rr = lax.broadcasted_iota(jnp.int32, (2 * tk, 2 * tk), 0)
    cc = lax.broadcasted_iota(jnp.int32, (2 * tk, 2 * tk), 1)
    key_j = jnp.where(rr >= tk, rr - tk, rr)
    tri = jnp.where((cc >= tk) | (key_j > cc), 1.0, 0.0).astype(BF16)
    below_diag = (lax.broadcasted_iota(jnp.int32, (tk, tk), 1)
                  < lax.broadcasted_iota(jnp.int32, (tk, tk), 0))
    has_left = ti > 0

    def softplus_parts(z):
        lb = jnp.minimum(z, 0.0) - jnp.log(1.0 + jnp.exp2(-jnp.abs(z))) * LOG2_E
        return lb, lb - z

    def suffix_and_total(l):
        hi = l.astype(BF16)
        lo = (l - hi.astype(F32)).astype(BF16)
        cs = jnp.dot(jnp.concatenate([hi, lo], axis=1), tri, preferred_element_type=F32)
        return cs[:, :tk], cs[:, tk:]

    def pair_queries(g):
        return (q_ref[0, :, g * LANES:(g + 1) * LANES].astype(F32) * scale2).astype(BF16)

    def by_head(kv):
        lane = lax.broadcasted_iota(jnp.int32, kv.shape, 1)
        zero = jnp.zeros_like(kv)
        return jnp.where(lane < HEAD_DIM, kv, zero), jnp.where(lane >= HEAD_DIM, kv, zero)

    def pair_scores(q, k_a, k_b):
        return lax.dot_general(q, jnp.concatenate([k_a, k_b], axis=0), (((1,), (1,)), ((), ())),
                               preferred_element_type=F32)

    def pair_output(att_a, att_b, v_a, v_b):
        return jnp.dot(jnp.concatenate([att_a, att_b], axis=1),
                       jnp.concatenate([v_a, v_b], axis=0), preferred_element_type=F32)

    def attention_stages():
        nk = (nleft + 1) * tk
        for g in range(npair):
            cols = slice(g * LANES, (g + 1) * LANES)
            q = pair_queries(g)
            k_heads = by_head(jnp.concatenate([kp_ref[0, :, cols], kc_ref[0, :, cols]], axis=0))
            v_heads = by_head(jnp.concatenate([vp_ref[0, :, cols], vc_ref[0, :, cols]], axis=0))
            chains = []
            for rb in range(nrb):
                rows = slice(rb * tk, (rb + 1) * tk)
                win = slice(rb * tk, (rb + nleft + 1) * tk)
                z = pair_scores(q[rows, :], k_heads[0][win, :], k_heads[1][win, :])
                chains.append(dict(rb=rb, rows=rows, v=(v_heads[0][win, :], v_heads[1][win, :]),
                                   z=(z[:, :nk], z[:, nk:])))
            yield
            for c in chains:
                c["lb"], c["parts"] = ([], []), ([], [])
                for h in range(2):
                    for kb in range(nleft + 1):
                        zb = c["z"][h][:, kb * tk:(kb + 1) * tk]
                        if kb == nleft:
                            zb = jnp.where(below_diag, zb, -MASKED_SCORE)
                        elif c["rb"] + kb < nleft:
                            zb = jnp.where(has_left, zb, -MASKED_SCORE)
                        lb, l = softplus_parts(zb)
                        c["lb"][h].append(lb)
                        c["parts"][h].append(suffix_and_total(l))
            yield
            for c in chains:
                c["att"] = []
                for h in range(2):
                    r = jnp.zeros((tk, tk), F32)
                    atts = [None] * (nleft + 1)
                    for kb in reversed(range(nleft + 1)):
                        suffix, total = c["parts"][h][kb]
                        atts[kb] = jnp.exp2(c["lb"][h][kb] + suffix + r).astype(BF16)
                        r = r + total
                    c["att"].append(jnp.concatenate(atts, axis=1))
                    r_ref[2 * g + h, c["rows"], :] = r
            yield
            for c in chains:
                acc_ref[g, c["rows"], :] = pair_output(c["att"][0], c["att"][1], *c["v"])
            yield

    _interleave(attention_stages(), mixer)

    def cond(carry):
        s, alive = carry
        return jnp.logical_and(s <= g0 + (nrb - 1 - nleft - 1), alive)

    def body(carry):
        s, _ = carry
        for g in range(npair):
            q = pair_queries(g)
            for rb in range(nrb):
                jb = g0 + (rb - nleft - 1) - s
                rows = slice(rb * tk, (rb + 1) * tk)
                rows_alive = jnp.max(r_ref[2 * g:2 * g + 2, rows, :]) >= ATT_DEAD_LOG2

                @pl.when(jnp.logical_and(jb >= 0, rows_alive))
                def _():
                    ks = pl.multiple_of(jb * tk, tk)
                    copies = [
                        pltpu.make_async_copy(
                            qkv_hbm.at[bi, pl.ds(ks, tk), pl.ds((1 + p) * att_w + g * LANES, LANES)],
                            dst, sem.at[p])
                        for p, dst in enumerate((kblk_ref, vblk_ref))]
                    for cp in copies:
                        cp.start()
                    for cp in copies:
                        cp.wait()
                    z = pair_scores(q[rows, :], *by_head(kblk_ref[...]))
                    atts = []
                    for h in range(2):
                        lb, l = softplus_parts(z[:, h * tk:(h + 1) * tk])
                        suffix, total = suffix_and_total(l)
                        r_old = r_ref[2 * g + h, rows, :]
                        atts.append(jnp.exp2(lb + suffix + r_old).astype(BF16))
                        r_ref[2 * g + h, rows, :] = r_old + total
                    acc_ref[g, rows, :] += pair_output(*atts, *by_head(vblk_ref[...]))
        return s + 1, jnp.max(r_ref[...]) >= ATT_DEAD_LOG2

    lax.while_loop(cond, body, (0, jnp.max(r_ref[...]) >= ATT_DEAD_LOG2))

    for g in range(npair):
        yatt_ref[slot, :, g * LANES:(g + 1) * LANES] = acc_ref[g]


def _attn_ffn(qkv, x2, y_lru, g_att, w_out, g_ffn, w_ffn_in, w_ffn_out, g_last, tq, tk, nleft,
              ff_chunk, final_norm):
    bsz, s, three_w = qkv.shape
    att_w = three_w // 3
    n, d = x2.shape
    d_ff = w_ffn_out.shape[0]
    tiles_per_seq = s // tq
    n_tiles = bsz * tiles_per_seq
    left = nleft * tk
    assert s % tq == 0 and tq % tk == 0 and tq // tk >= nleft and tq % left == 0
    assert att_w % LANES == 0

    def tile(j):
        ja = jnp.minimum(j, n_tiles - 1)
        return ja // tiles_per_seq, ja % tiles_per_seq

    def cur(part):
        return pl.BlockSpec((1, tq, att_w), lambda j: (*tile(j), part))

    def prev(part):
        def index(j):
            b, t = tile(j)
            return b, jnp.maximum(t * (tq // left) - 1, 0), part
        return pl.BlockSpec((1, left, att_w), index)

    row = lambda w: pl.BlockSpec((tq, w), lambda j: (jnp.maximum(j - 1, 0), 0))
    vec = lambda w: pl.BlockSpec((1, w), lambda j: (0, 0))
    whole = lambda a: pl.BlockSpec(a.shape, lambda j: (0, 0), pipeline_mode=pl.Buffered(1))
    nheads = att_w // HEAD_DIM
    return pl.pallas_call(
        functools.partial(_attn_ffn_kernel, tq=tq, tk=tk, nleft=nleft,
                          scale2=LOG2_E / math.sqrt(HEAD_DIM), tiles_per_seq=tiles_per_seq,
                          n_tiles=n_tiles, att_w=att_w, d_ff=d_ff, ff_chunk=ff_chunk,
                          final_norm=final_norm),
        grid=(n_tiles + 1,),
        in_specs=[cur(0), cur(1), prev(1), cur(2), prev(2),
                  pl.BlockSpec(memory_space=pl.ANY),
                  row(d), row(y_lru.shape[1]), vec(att_w), whole(w_out), vec(d),
                  whole(w_ffn_in), whole(w_ffn_out), vec(d)],
        out_specs=row(d),
        out_shape=jax.ShapeDtypeStruct((n, d), F32),
        scratch_shapes=[
            pltpu.VMEM((2, tq, att_w), F32),
            pltpu.VMEM((nheads, tq, tk), F32),
            pltpu.VMEM((att_w // LANES, tq, LANES), F32),
            pltpu.VMEM((tk, LANES), BF16),
            pltpu.VMEM((tk, LANES), BF16),
            pltpu.SemaphoreType.DMA((2,)),
        ],
        compiler_params=pltpu.CompilerParams(
            dimension_semantics=("arbitrary",), vmem_limit_bytes=VMEM_LIMIT_BYTES),
        name="attn_ffn",
    )(qkv, qkv, qkv, qkv, qkv, qkv, x2, y_lru, g_att, w_out, g_ffn, w_ffn_in, w_ffn_out, g_last)


def _largest_chunk(total, limit):
    best = LANES
    for c in range(LANES, limit + 1, LANES):
        if total % c == 0:
            best = c
    return best


def kernel(x, norm_mix, w_in, conv_w, conv_b, w_rg, b_rg, w_ig, b_ig, lru_lambda, norm_lru_out,
           norm_att_out, w_out, norm_ffn, w_ffn_in, w_ffn_out, norm_final):
    bsz, s, d = x.shape
    depth = w_in.shape[0]
    lru_w = conv_w.shape[2]
    att_w = norm_att_out.shape[1]
    blk_w = w_rg.shape[2]
    assert w_in.shape[2] == 2 * lru_w + 3 * att_w and att_w % LANES == 0
    assert MXU_DIM % blk_w == 0 and lru_w % MXU_DIM == 0

    n = bsz * s
    ts = 512
    assert s % ts == 0
    tq, tk, nleft = 512, 128, 2
    per = MXU_DIM // blk_w
    row1 = lambda v: v.reshape(1, -1)

    x2 = x.reshape(n, d)
    for l in range(depth):
        qkv, y_lru, w_out_b, w_ffn_in_b, w_ffn_out_b = _inproj_lru(
            x2, row1(norm_mix[l]), w_in[l].astype(BF16), conv_w[l], row1(conv_b[l]),
            _block_diag(w_rg[l], per).astype(BF16), row1(b_rg[l]),
            _block_diag(w_ig[l], per).astype(BF16), row1(b_ig[l]), row1(lru_lambda[l]),
            row1(norm_lru_out[l]), (w_out[l], w_ffn_in[l], w_ffn_out[l]), s, ts)
        d_ff = w_ffn_out.shape[1]
        x2 = _attn_ffn(qkv.reshape(bsz, s, 3 * att_w), x2, y_lru, row1(norm_att_out[l]), w_out_b,
                       row1(norm_ffn[l]), w_ffn_in_b, w_ffn_out_b, row1(norm_final),
                       tq, tk, nleft, _largest_chunk(d_ff, 512), l == depth - 1)
    return x2.reshape(bsz, s, d)
```

```python
import functools
import math

import jax
import jax.numpy as jnp
from jax import lax
from jax.experimental import pallas as pl
from jax.experimental.pallas import tpu as pltpu

F32 = jnp.float32
BF16 = jnp.bfloat16

EPS = 1e-6
LRU_C = 8.0
CONV_W = 4
HEAD_DIM = 64

LANES = 128
SUBLANES = 8
MXU_DIM = 256
VMEM_LIMIT_BYTES = 56 * 1024 * 1024

LOG2_E = 1.4426950408889634
ATT_DEAD_LOG2 = -160.0
MASKED_SCORE = 1e30
CAST_STEPS = 8


def _rms_scale(x):
    return lax.rsqrt(jnp.mean(x * x, axis=-1, keepdims=True) + EPS)


def _sigmoid(x):
    return 1.0 / (1.0 + jnp.exp(-x))


def _softplus(x):
    return jnp.maximum(x, 0.0) + jnp.log(1.0 + jnp.exp(-jnp.abs(x)))


def _gelu_tanh(x):
    c = math.sqrt(2.0 / math.pi)
    return 0.5 * x * (1.0 + jnp.tanh(c * (x + 0.044715 * (x * x * x))))


def _inproj_stages(x_ref, g_ref, w_ref, u_slot_ref, qkv_ref, *, lru_cols, col_chunk):
    x = x_ref[...]
    h = ((x * _rms_scale(x)) * g_ref[...]).astype(BF16)
    yield
    for lo in range(0, w_ref.shape[1], col_chunk):
        u = jnp.dot(h, w_ref[:, lo:lo + col_chunk], preferred_element_type=F32)
        if lo < lru_cols:
            u_slot_ref[:, lo:lo + col_chunk] = u
        else:
            qkv_ref[:, lo - lru_cols:lo - lru_cols + col_chunk] = u.astype(BF16)
        yield


def _lru_stages(first, u_ref, cw_ref, cb_ref, wrg_ref, brg_ref, wig_ref, big_ref, lam_ref, gn_ref,
                out_ref, xs_ref, hc_ref, *, ts, width, slab):
    halo = SUBLANES
    xs_ref[0:halo, :] = jnp.where(first, 0.0, xs_ref[ts:ts + halo, :])
    xs_ref[halo:halo + ts, :] = u_ref[:, 0:width]
    row = lax.broadcasted_iota(jnp.int32, (SUBLANES, width), 0)
    hprev = jnp.where(first, 0.0, hc_ref[...])
    neg_sp = -LRU_C * _softplus(-lam_ref[...])

    for s0 in range(0, ts, slab):
        xc = jnp.broadcast_to(cb_ref[...], (slab, width))
        for i in range(CONV_W):
            off = s0 + halo - (CONV_W - 1) + i
            xc = xc + xs_ref[off:off + slab, :] * cw_ref[i:i + 1, :]

        xcb = xc.astype(BF16)
        rg, ig = [], []
        for c in range(width // MXU_DIM):
            sl = slice(c * MXU_DIM, (c + 1) * MXU_DIM)
            gates = jnp.dot(xcb[:, sl], jnp.concatenate([wrg_ref[c], wig_ref[c]], axis=1),
                            preferred_element_type=F32)
            rg.append(gates[:, :MXU_DIM])
            ig.append(gates[:, MXU_DIM:])
        r = _sigmoid(jnp.concatenate(rg, axis=1) + brg_ref[...])
        ig = _sigmoid(jnp.concatenate(ig, axis=1) + big_ref[...])

        log_a = r * neg_sp
        a = jnp.exp(log_a)
        th = jnp.tanh(log_a)
        b = jnp.sqrt((-2.0 * th) / (1.0 - th)) * (ig * xc)
        yield

        hs = []
        for g in range(slab // SUBLANES):
            rows = slice(g * SUBLANES, (g + 1) * SUBLANES)
            a8, b8 = a[rows, :], b[rows, :]
            for d in (1, 2, 4):
                keep = row >= d
                b8 = jnp.where(keep, b8 + a8 * pltpu.roll(b8, d, axis=0), b8)
                a8 = jnp.where(keep, a8 * pltpu.roll(a8, d, axis=0), a8)
            h8 = b8 + a8 * hprev
            hs.append(h8)
            hprev = jnp.broadcast_to(h8[SUBLANES - 1:SUBLANES, :], (SUBLANES, width))
        y = _gelu_tanh(u_ref[s0:s0 + slab, width:2 * width]) * jnp.concatenate(hs, axis=0)
        out_ref[s0:s0 + slab, :] = ((y * _rms_scale(y)) * gn_ref[...]).astype(out_ref.dtype)
        yield
    hc_ref[...] = hprev


def _interleave(*streams):
    streams = list(streams)
    while streams:
        for st in list(streams):
            if next(st, StopIteration) is StopIteration:
                streams.remove(st)


def _inproj_lru_kernel(x_ref, gm_ref, w_ref, cw_ref, cb_ref, wrg_ref, brg_ref, wig_ref, big_ref,
                       lam_ref, gn_ref, *rest, ts, width, tiles_per_seq, col_chunk, slab,
                       cast_slabs):
    nw = len(cast_slabs)
    wsrc_refs, (qkv_ref, ylru_ref) = rest[:nw], rest[nw:nw + 2]
    wdst_refs = rest[nw + 2:2 * nw + 2]
    u_ref, xs_ref, hc_ref = rest[2 * nw + 2:]
    j = pl.program_id(0)
    for src_ref, dst_ref, n_slabs in zip(wsrc_refs, wdst_refs, cast_slabs):
        @pl.when(j < n_slabs)
        def _():
            dst_ref[...] = src_ref[...].astype(dst_ref.dtype)

    slot = j % 2
    lt = jnp.maximum(j - 1, 0)

    @pl.when(j == 0)
    def _():
        for ref in (u_ref, xs_ref, hc_ref):
            ref[...] = jnp.zeros(ref.shape, ref.dtype)

    _interleave(
        _lru_stages(lt % tiles_per_seq == 0, u_ref.at[1 - slot], cw_ref, cb_ref, wrg_ref, brg_ref,
                    wig_ref, big_ref, lam_ref, gn_ref, ylru_ref, xs_ref, hc_ref,
                    ts=ts, width=width, slab=slab),
        _inproj_stages(x_ref, gm_ref, w_ref, u_ref.at[slot], qkv_ref, lru_cols=2 * width,
                       col_chunk=col_chunk))


def _cast_slab_rows(rows, max_slabs):
    step = 2 * SUBLANES
    for slab_rows in range(step, rows + 1, step):
        if rows % slab_rows == 0 and rows // slab_rows <= max_slabs:
            return slab_rows
    raise ValueError(f"no bf16 row slab for {rows} rows in {max_slabs} steps")


def _inproj_lru(x2, gain, w_in_bf16, conv_w, conv_b, wrg_bd, b_rg, wig_bd, b_ig, lam, gain_lru,
                later_weights, seq_len, ts):
    n, d = x2.shape
    cols = w_in_bf16.shape[1]
    width = conv_w.shape[1]
    n_tiles = n // ts
    slab_rows = [_cast_slab_rows(w.shape[0], min(n_tiles, CAST_STEPS)) for w in later_weights]
    cast_slabs = tuple(w.shape[0] // r for w, r in zip(later_weights, slab_rows))
    wslab = lambda w, r: pl.BlockSpec(
        (r, w.shape[1]), lambda j: (jnp.minimum(j, w.shape[0] // r - 1), 0))
    wspecs = [wslab(w, r) for w, r in zip(later_weights, slab_rows)]
    tile = lambda lag: (lambda j: (jnp.clip(j - lag, 0, n_tiles - 1), 0))
    vec = lambda w: pl.BlockSpec((1, w), lambda j: (0, 0))
    wspec = pl.BlockSpec(wrg_bd.shape, lambda j: (0, 0, 0))
    return pl.pallas_call(
        functools.partial(_inproj_lru_kernel, ts=ts, width=width, tiles_per_seq=seq_len // ts,
                          col_chunk=2 * MXU_DIM, slab=LANES // 2, cast_slabs=cast_slabs),
        grid=(n_tiles + 1,),
        in_specs=[
            pl.BlockSpec((ts, d), tile(0)),
            vec(d),
            pl.BlockSpec((d, cols), lambda j: (0, 0), pipeline_mode=pl.Buffered(1)),
            pl.BlockSpec((CONV_W, width), lambda j: (0, 0)),
            vec(width), wspec, vec(width), wspec, vec(width), vec(width), vec(width),
            *wspecs,
        ],
        out_specs=[
            pl.BlockSpec((ts, cols - 2 * width), tile(0)),
            pl.BlockSpec((ts, width), tile(1)),
            *wspecs,
        ],
        out_shape=[
            jax.ShapeDtypeStruct((n, cols - 2 * width), BF16),
            jax.ShapeDtypeStruct((n, width), BF16),
            *[jax.ShapeDtypeStruct(w.shape, BF16) for w in later_weights],
        ],
        scratch_shapes=[
            pltpu.VMEM((2, ts, 2 * width), F32),
            pltpu.VMEM((ts + SUBLANES, width), F32),
            pltpu.VMEM((SUBLANES, width), F32),
        ],
        compiler_params=pltpu.CompilerParams(
            dimension_semantics=("arbitrary",), vmem_limit_bytes=VMEM_LIMIT_BYTES),
        name="inproj_lru",
    )(x2, gain, w_in_bf16, conv_w, conv_b, wrg_bd, b_rg, wig_bd, b_ig, lam, gain_lru,
      *later_weights)


def _block_diag(w, per):
    nb, bw, _ = w.shape
    w = w.reshape(nb // per, per, bw, bw)
    eye = jnp.eye(per, dtype=w.dtype)
    return jnp.einsum('gpij,pq->gpiqj', w, eye).reshape(nb // per, per * bw, per * bw)


def _ffn_stages(x_ref, yl_ref, ya, ga_ref, wo_ref, gf_ref, wi_ref, wd_ref, gl_ref, out_ref, *,
                d_ff, ff_chunk, final_norm):
    ya = ((ya * _rms_scale(ya)) * ga_ref[...]).astype(BF16)
    wl = yl_ref.shape[1]
    x1 = x_ref[...] + jnp.dot(yl_ref[...], wo_ref[0:wl, :], preferred_element_type=F32)
    x1 = x1 + jnp.dot(ya, wo_ref[wl:, :], preferred_element_type=F32)
    h2 = ((x1 * _rms_scale(x1)) * gf_ref[...]).astype(BF16)
    yield
    acts = []
    for c in range(d_ff // ff_chunk):
        lo = c * ff_chunk
        gate = jnp.dot(h2, wi_ref[:, lo:lo + ff_chunk], preferred_element_type=F32)
        up = jnp.dot(h2, wi_ref[:, d_ff + lo:d_ff + lo + ff_chunk], preferred_element_type=F32)
        acts.append(((gate * _sigmoid(gate)) * up).astype(BF16))
        yield
    out = x1 + jnp.dot(jnp.concatenate(acts, axis=1), wd_ref[...], preferred_element_type=F32)
    yield
    if final_norm:
        out = (out * _rms_scale(out)) * gl_ref[...]
    out_ref[...] = out


def _attn_ffn_kernel(q_ref, kc_ref, kp_ref, vc_ref, vp_ref, qkv_hbm, x_ref, yl_ref, ga_ref,
                     wo_ref, gf_ref, wi_ref, wd_ref, gl_ref, out_ref,
                     yatt_ref, r_ref, acc_ref, kblk_ref, vblk_ref, sem,
                     *, tq, tk, nleft, scale2, tiles_per_seq, n_tiles, att_w, d_ff, ff_chunk,
                     final_norm):
    j = pl.program_id(0)
    ja = jnp.minimum(j, n_tiles - 1)
    bi = ja // tiles_per_seq
    ti = ja % tiles_per_seq
    nrb = tq // tk
    g0 = ti * nrb
    npair = att_w // LANES
    slot = j % 2

    @pl.when(j == 0)
    def _():
        yatt_ref[...] = jnp.zeros(yatt_ref.shape, F32)

    mixer = _ffn_stages(x_ref, yl_ref, yatt_ref[1 - slot], ga_ref, wo_ref, gf_ref, wi_ref, wd_ref,
                        gl_ref, out_ref, d_ff=d_ff, ff_chunk=ff_chunk, final_norm=final_norm)

    rr = lax.broadcasted_iota(jnp.int32, (2 * tk, 2 * tk), 0)
    cc = lax.broadcasted_iota(jnp.int32, (2 * tk, 2 * tk), 1)
    key_j = jnp.where(rr >= tk, rr - tk, rr)
    tri = jnp.where((cc >= tk) | (key_j > cc), 1.0, 0.0).astype(BF16)
    below_diag = (lax.broadcasted_iota(jnp.int32, (tk, tk), 1)
                  < lax.broadcasted_iota(jnp.int32, (tk, tk), 0))
    has_left = ti > 0

    def softplus_parts(z):
        lb = jnp.minimum(z, 0.0) - jnp.log(1.0 + jnp.exp2(-jnp.abs(z))) * LOG2_E
        return lb, lb - z

    def suffix_and_total(l):
        hi = l.astype(BF16)
        lo = (l - hi.astype(F32)).astype(BF16)
        cs = jnp.dot(jnp.concatenate([hi, lo], axis=1), tri, preferred_element_type=F32)
        return cs[:, :tk], cs[:, tk:]

    def pair_queries(g):
        return (q_ref[0, :, g * LANES:(g + 1) * LANES].astype(F32) * scale2).astype(BF16)

    def by_head(kv):
        lane = lax.broadcasted_iota(jnp.int32, kv.shape, 1)
        zero = jnp.zeros_like(kv)
        return jnp.where(lane < HEAD_DIM, kv, zero), jnp.where(lane >= HEAD_DIM, kv, zero)

    def pair_scores(q, k_a, k_b):
        return lax.dot_general(q, jnp.concatenate([k_a, k_b], axis=0), (((1,), (1,)), ((), ())),
                               preferred_element_type=F32)

    def pair_output(att_a, att_b, v_a, v_b):
        return jnp.dot(jnp.concatenate([att_a, att_b], axis=1),
                       jnp.concatenate([v_a, v_b], axis=0), preferred_element_type=F32)

    def attention_stages():
        nk = (nleft + 1) * tk
        for g in range(npair):
            cols = slice(g * LANES, (g + 1) * LANES)
            q = pair_queries(g)
            k_heads = by_head(jnp.concatenate([kp_ref[0, :, cols], kc_ref[0, :, cols]], axis=0))
            v_heads = by_head(jnp.concatenate([vp_ref[0, :, cols], vc_ref[0, :, cols]], axis=0))
            chains = []
            for rb in range(nrb):
                rows = slice(rb * tk, (rb + 1) * tk)
                win = slice(rb * tk, (rb + nleft + 1) * tk)
                z = pair_scores(q[rows, :], k_heads[0][win, :], k_heads[1][win, :])
                chains.append(dict(rb=rb, rows=rows, v=(v_heads[0][win, :], v_heads[1][win, :]),
                                   z=(z[:, :nk], z[:, nk:])))
            yield
            for c in chains:
                c["lb"], c["parts"] = ([], []), ([], [])
                for h in range(2):
                    for kb in range(nleft + 1):
                        zb = c["z"][h][:, kb * tk:(kb + 1) * tk]
                        if kb == nleft:
                            zb = jnp.where(below_diag, zb, -MASKED_SCORE)
                        elif c["rb"] + kb < nleft:
                            zb = jnp.where(has_left, zb, -MASKED_SCORE)
                        lb, l = softplus_parts(zb)
                        c["lb"][h].append(lb)
                        c["parts"][h].append(suffix_and_total(l))
            yield
            for c in chains:
                c["att"] = []
                for h in range(2):
                    r = jnp.zeros((tk, tk), F32)
                    atts = [None] * (nleft + 1)
                    for kb in reversed(range(nleft + 1)):
                        suffix, total = c["parts"][h][kb]
                        atts[kb] = jnp.exp2(c["lb"][h][kb] + suffix + r).astype(BF16)
                        r = r + total
                    c["att"].append(jnp.concatenate(atts, axis=1))
                    r_ref[2 * g + h, c["rows"], :] = r
            yield
            for c in chains:
                acc_ref[g, c["rows"], :] = pair_output(c["att"][0], c["att"][1], *c["v"])
            yield

    _interleave(attention_stages(), mixer)

    def cond(carry):
        s, alive = carry
        return jnp.logical_and(s <= g0 + (nrb - 1 - nleft - 1), alive)

    def body(carry):
        s, _ = carry
        for g in range(npair):
            q = pair_queries(g)
            for rb in range(nrb):
                jb = g0 + (rb - nleft - 1) - s
                rows = slice(rb * tk, (rb + 1) * tk)
                rows_alive = jnp.max(r_ref[2 * g:2 * g + 2, rows, :]) >= ATT_DEAD_LOG2

                @pl.when(jnp.logical_and(jb >= 0, rows_alive))
                def _():
                    ks = pl.multiple_of(jb * tk, tk)
                    copies = [
                        pltpu.make_async_copy(
                            qkv_hbm.at[bi, pl.ds(ks, tk), pl.ds((1 + p) * att_w + g * LANES, LANES)],
                            dst, sem.at[p])
                        for p, dst in enumerate((kblk_ref, vblk_ref))]
                    for cp in copies:
                        cp.start()
                    for cp in copies:
                        cp.wait()
                    z = pair_scores(q[rows, :], *by_head(kblk_ref[...]))
                    atts = []
                    for h in range(2):
                        lb, l = softplus_parts(z[:, h * tk:(h + 1) * tk])
                        suffix, total = suffix_and_total(l)
                        r_old = r_ref[2 * g + h, rows, :]
                        atts.append(jnp.exp2(lb + suffix + r_old).astype(BF16))
                        r_ref[2 * g + h, rows, :] = r_old + total
                    acc_ref[g, rows, :] += pair_output(*atts, *by_head(vblk_ref[...]))
        return s + 1, jnp.max(r_ref[...]) >= ATT_DEAD_LOG2

    lax.while_loop(cond, body, (0, jnp.max(r_ref[...]) >= ATT_DEAD_LOG2))

    for g in range(npair):
        yatt_ref[slot, :, g * LANES:(g + 1) * LANES] = acc_ref[g]


def _attn_ffn(qkv, x2, y_lru, g_att, w_out, g_ffn, w_ffn_in, w_ffn_out, g_last, tq, tk, nleft,
              ff_chunk, final_norm):
    bsz, s, three_w = qkv.shape
    att_w = three_w // 3
    n, d = x2.shape
    d_ff = w_ffn_out.shape[0]
    tiles_per_seq = s // tq
    n_tiles = bsz * tiles_per_seq
    left = nleft * tk
    assert s % tq == 0 and tq % tk == 0 and tq // tk >= nleft and tq % left == 0
    assert att_w % LANES == 0

    def tile(j):
        ja = jnp.minimum(j, n_tiles - 1)
        return ja // tiles_per_seq, ja % tiles_per_seq

    def cur(part):
        return pl.BlockSpec((1, tq, att_w), lambda j: (*tile(j), part))

    def prev(part):
        def index(j):
            b, t = tile(j)
            return b, jnp.maximum(t * (tq // left) - 1, 0), part
        return pl.BlockSpec((1, left, att_w), index)

    row = lambda w: pl.BlockSpec((tq, w), lambda j: (jnp.maximum(j - 1, 0), 0))
    vec = lambda w: pl.BlockSpec((1, w), lambda j: (0, 0))
    whole = lambda a: pl.BlockSpec(a.shape, lambda j: (0, 0), pipeline_mode=pl.Buffered(1))
    nheads = att_w // HEAD_DIM
    return pl.pallas_call(
        functools.partial(_attn_ffn_kernel, tq=tq, tk=tk, nleft=nleft,
                          scale2=LOG2_E / math.sqrt(HEAD_DIM), tiles_per_seq=tiles_per_seq,
                          n_tiles=n_tiles, att_w=att_w, d_ff=d_ff, ff_chunk=ff_chunk,
                          final_norm=final_norm),
        grid=(n_tiles + 1,),
        in_specs=[cur(0), cur(1), prev(1), cur(2), prev(2),
                  pl.BlockSpec(memory_space=pl.ANY),
                  row(d), row(y_lru.shape[1]), vec(att_w), whole(w_out), vec(d),
                  whole(w_ffn_in), whole(w_ffn_out), vec(d)],
        out_specs=row(d),
        out_shape=jax.ShapeDtypeStruct((n, d), F32),
        scratch_shapes=[
            pltpu.VMEM((2, tq, att_w), F32),
            pltpu.VMEM((nheads, tq, tk), F32),
            pltpu.VMEM((att_w // LANES, tq, LANES), F32),
            pltpu.VMEM((tk, LANES), BF16),
            pltpu.VMEM((tk, LANES), BF16),
            pltpu.SemaphoreType.DMA((2,)),
        ],
        compiler_params=pltpu.CompilerParams(
            dimension_semantics=("arbitrary",), vmem_limit_bytes=VMEM_LIMIT_BYTES),
        name="attn_ffn",
    )(qkv, qkv, qkv, qkv, qkv, qkv, x2, y_lru, g_att, w_out, g_ffn, w_ffn_in, w_ffn_out, g_last)


def _largest_chunk(total, limit):
    best = LANES
    for c in range(LANES, limit + 1, LANES):
        if total % c == 0:
            best = c
    return best


def kernel(x, norm_mix, w_in, conv_w, conv_b, w_rg, b_rg, w_ig, b_ig, lru_lambda, norm_lru_out,
           norm_att_out, w_out, norm_ffn, w_ffn_in, w_ffn_out, norm_final):
    bsz, s, d = x.shape
    depth = w_in.shape[0]
    lru_w = conv_w.shape[2]
    att_w = norm_att_out.shape[1]
    blk_w = w_rg.shape[2]
    assert w_in.shape[2] == 2 * lru_w + 3 * att_w and att_w % LANES == 0
    assert MXU_DIM % blk_w == 0 and lru_w % MXU_DIM == 0

    n = bsz * s
    ts = 512
    assert s % ts == 0
    tq, tk, nleft = 512, 128, 2
    per = MXU_DIM // blk_w
    row1 = lambda v: v.reshape(1, -1)

    x2 = x.reshape(n, d)
    for l in range(depth):
        qkv, y_lru, w_out_b, w_ffn_in_b, w_ffn_out_b = _inproj_lru(
            x2, row1(norm_mix[l]), w_in[l].astype(BF16), conv_w[l], row1(conv_b[l]),
            _block_diag(w_rg[l], per).astype(BF16), row1(b_rg[l]),
            _block_diag(w_ig[l], per).astype(BF16), row1(b_ig[l]), row1(lru_lambda[l]),
            row1(norm_lru_out[l]), (w_out[l], w_ffn_in[l], w_ffn_out[l]), s, ts)
        d_ff = w_ffn_out.shape[1]
        x2 = _attn_ffn(qkv.reshape(bsz, s, 3 * att_w), x2, y_lru, row1(norm_att_out[l]), w_out_b,
                       row1(norm_ffn[l]), w_ffn_in_b, w_ffn_out_b, row1(norm_final),
                       tq, tk, nleft, _largest_chunk(d_ff, 512), l == depth - 1)
    return x2.reshape(bsz, s, d)
```

```python
import functools
import math

import jax
import jax.numpy as jnp
from jax import lax
from jax.experimental import pallas as pl
from jax.experimental.pallas import tpu as pltpu

F32 = jnp.float32
BF16 = jnp.bfloat16

EPS = 1e-6
LRU_C = 8.0
CONV_W = 4
HEAD_DIM = 64

LANES = 128
SUBLANES = 8
MXU_DIM = 256
VMEM_LIMIT_BYTES = 56 * 1024 * 1024

LOG2_E = 1.4426950408889634
ATT_DEAD_LOG2 = -160.0
MASKED_SCORE = 1e30
CAST_STEPS = 8


def _rms_scale(x):
    return lax.rsqrt(jnp.mean(x * x, axis=-1, keepdims=True) + EPS)


def _sigmoid(x):
    return 1.0 / (1.0 + jnp.exp(-x))


def _softplus(x):
    return jnp.maximum(x, 0.0) + jnp.log(1.0 + jnp.exp(-jnp.abs(x)))


def _gelu_tanh(x):
    c = math.sqrt(2.0 / math.pi)
    return 0.5 * x * (1.0 + jnp.tanh(c * (x + 0.044715 * (x * x * x))))


def _inproj_stages(x_ref, g_ref, w_ref, u_slot_ref, qkv_ref, *, lru_cols, col_chunk):
    x = x_ref[...]
    h = ((x * _rms_scale(x)) * g_ref[...]).astype(BF16)
    yield
    for lo in range(0, w_ref.shape[1], col_chunk):
        u = jnp.dot(h, w_ref[:, lo:lo + col_chunk], preferred_element_type=F32)
        if lo < lru_cols:
            u_slot_ref[:, lo:lo + col_chunk] = u
        else:
            qkv_ref[:, lo - lru_cols:lo - lru_cols + col_chunk] = u.astype(BF16)
        yield


def _lru_stages(first, u_ref, cw_ref, cb_ref, wrg_ref, brg_ref, wig_ref, big_ref, lam_ref, gn_ref,
                out_ref, xs_ref, hc_ref, *, ts, width, slab):
    halo = SUBLANES
    xs_ref[0:halo, :] = jnp.where(first, 0.0, xs_ref[ts:ts + halo, :])
    xs_ref[halo:halo + ts, :] = u_ref[:, 0:width]
    row = lax.broadcasted_iota(jnp.int32, (SUBLANES, width), 0)
    hprev = jnp.where(first, 0.0, hc_ref[...])
    neg_sp = -LRU_C * _softplus(-lam_ref[...])

    for s0 in range(0, ts, slab):
        xc = jnp.broadcast_to(cb_ref[...], (slab, width))
        for i in range(CONV_W):
            off = s0 + halo - (CONV_W - 1) + i
            xc = xc + xs_ref[off:off + slab, :] * cw_ref[i:i + 1, :]

        xcb = xc.astype(BF16)
        rg, ig = [], []
        for c in range(width // MXU_DIM):
            sl = slice(c * MXU_DIM, (c + 1) * MXU_DIM)
            gates = jnp.dot(xcb[:, sl], jnp.concatenate([wrg_ref[c], wig_ref[c]], axis=1),
                            preferred_element_type=F32)
            rg.append(gates[:, :MXU_DIM])
            ig.append(gates[:, MXU_DIM:])
        r = _sigmoid(jnp.concatenate(rg, axis=1) + brg_ref[...])
        ig = _sigmoid(jnp.concatenate(ig, axis=1) + big_ref[...])

        log_a = r * neg_sp
        a = jnp.exp(log_a)
        th = jnp.tanh(log_a)
        b = jnp.sqrt((-2.0 * th) / (1.0 - th)) * (ig * xc)
        yield

        hs = []
        for g in range(slab // SUBLANES):
            rows = slice(g * SUBLANES, (g + 1) * SUBLANES)
            a8, b8 = a[rows, :], b[rows, :]
            for d in (1, 2, 4):
                keep = row >= d
                b8 = jnp.where(keep, b8 + a8 * pltpu.roll(b8, d, axis=0), b8)
                a8 = jnp.where(keep, a8 * pltpu.roll(a8, d, axis=0), a8)
            h8 = b8 + a8 * hprev
            hs.append(h8)
            hprev = jnp.broadcast_to(h8[SUBLANES - 1:SUBLANES, :], (SUBLANES, width))
        y = _gelu_tanh(u_ref[s0:s0 + slab, width:2 * width]) * jnp.concatenate(hs, axis=0)
        out_ref[s0:s0 + slab, :] = ((y * _rms_scale(y)) * gn_ref[...]).astype(out_ref.dtype)
        yield
    hc_ref[...] = hprev


def _interleave(*streams):
    streams = list(streams)
    while streams:
        for st in list(streams):
            if next(st, StopIteration) is StopIteration:
                streams.remove(st)


def _inproj_lru_kernel(x_ref, gm_ref, w_ref, cw_ref, cb_ref, wrg_ref, brg_ref, wig_ref, big_ref,
                       lam_ref, gn_ref, *rest, ts, width, tiles_per_seq, col_chunk, slab,
                       cast_slabs):
    nw = len(cast_slabs)
    wsrc_refs, (qkv_ref, ylru_ref) = rest[:nw], rest[nw:nw + 2]
    wdst_refs = rest[nw + 2:2 * nw + 2]
    u_ref, xs_ref, hc_ref = rest[2 * nw + 2:]
    j = pl.program_id(0)
    for src_ref, dst_ref, n_slabs in zip(wsrc_refs, wdst_refs, cast_slabs):
        @pl.when(j < n_slabs)
        def _():
            dst_ref[...] = src_ref[...].astype(dst_ref.dtype)

    slot = j % 2
    lt = jnp.maximum(j - 1, 0)

    @pl.when(j == 0)
    def _():
        for ref in (u_ref, xs_ref, hc_ref):
            ref[...] = jnp.zeros(ref.shape, ref.dtype)

    _interleave(
        _lru_stages(lt % tiles_per_seq == 0, u_ref.at[1 - slot], cw_ref, cb_ref, wrg_ref, brg_ref,
                    wig_ref, big_ref, lam_ref, gn_ref, ylru_ref, xs_ref, hc_ref,
                    ts=ts, width=width, slab=slab),
        _inproj_stages(x_ref, gm_ref, w_ref, u_ref.at[slot], qkv_ref, lru_cols=2 * width,
                       col_chunk=col_chunk))


def _cast_slab_rows(rows, max_slabs):
    step = 2 * SUBLANES
    for slab_rows in range(step, rows + 1, step):
        if rows % slab_rows == 0 and rows // slab_rows <= max_slabs:
            return slab_rows
    raise ValueError(f"no bf16 row slab for {rows} rows in {max_slabs} steps")


def _inproj_lru(x2, gain, w_in_bf16, conv_w, conv_b, wrg_bd, b_rg, wig_bd, b_ig, lam, gain_lru,
                later_weights, seq_len, ts):
    n, d = x2.shape
    cols = w_in_bf16.shape[1]
    width = conv_w.shape[1]
    n_tiles = n // ts
    slab_rows = [_cast_slab_rows(w.shape[0], min(n_tiles, CAST_STEPS)) for w in later_weights]
    cast_slabs = tuple(w.shape[0] // r for w, r in zip(later_weights, slab_rows))
    wslab = lambda w, r: pl.BlockSpec(
        (r, w.shape[1]), lambda j: (jnp.minimum(j, w.shape[0] // r - 1), 0))
    wspecs = [wslab(w, r) for w, r in zip(later_weights, slab_rows)]
    tile = lambda lag: (lambda j: (jnp.clip(j - lag, 0, n_tiles - 1), 0))
    vec = lambda w: pl.BlockSpec((1, w), lambda j: (0, 0))
    wspec = pl.BlockSpec(wrg_bd.shape, lambda j: (0, 0, 0))
    return pl.pallas_call(
        functools.partial(_inproj_lru_kernel, ts=ts, width=width, tiles_per_seq=seq_len // ts,
                          col_chunk=MXU_DIM, slab=LANES // 2, cast_slabs=cast_slabs),
        grid=(n_tiles + 1,),
        in_specs=[
            pl.BlockSpec((ts, d), tile(0)),
            vec(d),
            pl.BlockSpec((d, cols), lambda j: (0, 0), pipeline_mode=pl.Buffered(1)),
            pl.BlockSpec((CONV_W, width), lambda j: (0, 0)),
            vec(width), wspec, vec(width), wspec, vec(width), vec(width), vec(width),
            *wspecs,
        ],
        out_specs=[
            pl.BlockSpec((ts, cols - 2 * width), tile(0)),
            pl.BlockSpec((ts, width), tile(1)),
            *wspecs,
        ],
        out_shape=[
            jax.ShapeDtypeStruct((n, cols - 2 * width), BF16),
            jax.ShapeDtypeStruct((n, width), BF16),
            *[jax.ShapeDtypeStruct(w.shape, BF16) for w in later_weights],
        ],
        scratch_shapes=[
            pltpu.VMEM((2, ts, 2 * width), F32),
            pltpu.VMEM((ts + SUBLANES, width), F32),
            pltpu.VMEM((SUBLANES, width), F32),
        ],
        compiler_params=pltpu.CompilerParams(
            dimension_semantics=("arbitrary",), vmem_limit_bytes=VMEM_LIMIT_BYTES),
        name="inproj_lru",
    )(x2, gain, w_in_bf16, conv_w, conv_b, wrg_bd, b_rg, wig_bd, b_ig, lam, gain_lru,
      *later_weights)


def _block_diag(w, per):
    nb, bw, _ = w.shape
    w = w.reshape(nb // per, per, bw, bw)
    eye = jnp.eye(per, dtype=w.dtype)
    return jnp.einsum('gpij,pq->gpiqj', w, eye).reshape(nb // per, per * bw, per * bw)


def _ffn_stages(x_ref, yl_ref, ya, ga_ref, wo_ref, gf_ref, wi_ref, wd_ref, gl_ref, out_ref, *,
                d_ff, ff_chunk, final_norm):
    ya = ((ya * _rms_scale(ya)) * ga_ref[...]).astype(BF16)
    wl = yl_ref.shape[1]
    x1 = x_ref[...] + jnp.dot(yl_ref[...], wo_ref[0:wl, :], preferred_element_type=F32)
    x1 = x1 + jnp.dot(ya, wo_ref[wl:, :], preferred_element_type=F32)
    h2 = ((x1 * _rms_scale(x1)) * gf_ref[...]).astype(BF16)
    yield
    acts = []
    for c in range(d_ff // ff_chunk):
        lo = c * ff_chunk
        gu = jnp.dot(h2, jnp.concatenate([wi_ref[:, lo:lo + ff_chunk],
                                          wi_ref[:, d_ff + lo:d_ff + lo + ff_chunk]], axis=1),
                     preferred_element_type=F32)
        gate, up = gu[:, :ff_chunk], gu[:, ff_chunk:]
        acts.append(((gate * _sigmoid(gate)) * up).astype(BF16))
        yield
    out = x1 + jnp.dot(jnp.concatenate(acts, axis=1), wd_ref[...], preferred_element_type=F32)
    yield
    if final_norm:
        out = (out * _rms_scale(out)) * gl_ref[...]
    out_ref[...] = out


def _attn_ffn_kernel(q_ref, kc_ref, kp_ref, vc_ref, vp_ref, qkv_hbm, x_ref, yl_ref, ga_ref,
                     wo_ref, gf_ref, wi_ref, wd_ref, gl_ref, out_ref,
                     yatt_ref, r_ref, acc_ref, kblk_ref, vblk_ref, sem,
                     *, tq, tk, nleft, scale2, tiles_per_seq, n_tiles, att_w, d_ff, ff_chunk,
                     final_norm):
    j = pl.program_id(0)
    ja = jnp.minimum(j, n_tiles - 1)
    bi = ja // tiles_per_seq
    ti = ja % tiles_per_seq
    nrb = tq // tk
    g0 = ti * nrb
    npair = att_w // LANES
    slot = j % 2

    @pl.when(j == 0)
    def _():
        yatt_ref[...] = jnp.zeros(yatt_ref.shape, F32)

    mixer = _ffn_stages(x_ref, yl_ref, yatt_ref[1 - slot], ga_ref, wo_ref, gf_ref, wi_ref, wd_ref,
                        gl_ref, out_ref, d_ff=d_ff, ff_chunk=ff_chunk, final_norm=final_norm)

    rr = lax.broadcasted_iota(jnp.int32, (2 * tk, 2 * tk), 0)
    cc = lax.broadcasted_iota(jnp.int32, (2 * tk, 2 * tk), 1)
    key_j = jnp.where(rr >= tk, rr - tk, rr)
    tri = jnp.where((cc >= tk) | (key_j > cc), 1.0, 0.0).astype(BF16)
    below_diag = (lax.broadcasted_iota(jnp.int32, (tk, tk), 1)
                  < lax.broadcasted_iota(jnp.int32, (tk, tk), 0))
    has_left = ti > 0

    def softplus_parts(z):
        lb = jnp.minimum(z, 0.0) - jnp.log(1.0 + jnp.exp2(-jnp.abs(z))) * LOG2_E
        return lb, lb - z

    def suffix_and_total(l):
        hi = l.astype(BF16)
        lo = (l - hi.astype(F32)).astype(BF16)
        cs = jnp.dot(jnp.concatenate([hi, lo], axis=1), tri, preferred_element_type=F32)
        return cs[:, :tk], cs[:, tk:]

    def pair_queries(g):
        return (q_ref[0, :, g * LANES:(g + 1) * LANES].astype(F32) * scale2).astype(BF16)

    def by_head(kv):
        lane = lax.broadcasted_iota(jnp.int32, kv.shape, 1)
        zero = jnp.zeros_like(kv)
        return jnp.where(lane < HEAD_DIM, kv, zero), jnp.where(lane >= HEAD_DIM, kv, zero)

    def pair_scores(q, k_a, k_b):
        return lax.dot_general(q, jnp.concatenate([k_a, k_b], axis=0), (((1,), (1,)), ((), ())),
                               preferred_element_type=F32)

    def pair_output(att_a, att_b, v_a, v_b):
        return jnp.dot(jnp.concatenate([att_a, att_b], axis=1),
                       jnp.concatenate([v_a, v_b], axis=0), preferred_element_type=F32)

    def attention_stages():
        nk = (nleft + 1) * tk
        for g in range(npair):
            cols = slice(g * LANES, (g + 1) * LANES)
            q = pair_queries(g)
            k_heads = by_head(jnp.concatenate([kp_ref[0, :, cols], kc_ref[0, :, cols]], axis=0))
            v_heads = by_head(jnp.concatenate([vp_ref[0, :, cols], vc_ref[0, :, cols]], axis=0))
            chains = []
            for rb in range(nrb):
                rows = slice(rb * tk, (rb + 1) * tk)
                win = slice(rb * tk, (rb + nleft + 1) * tk)
                z = pair_scores(q[rows, :], k_heads[0][win, :], k_heads[1][win, :])
                chains.append(dict(rb=rb, rows=rows, v=(v_heads[0][win, :], v_heads[1][win, :]),
                                   z=(z[:, :nk], z[:, nk:])))
            yield
            for c in chains:
                c["lb"], c["parts"] = ([], []), ([], [])
                for h in range(2):
                    for kb in range(nleft + 1):
                        zb = c["z"][h][:, kb * tk:(kb + 1) * tk]
                        if kb == nleft:
                            zb = jnp.where(below_diag, zb, -MASKED_SCORE)
                        elif c["rb"] + kb < nleft:
                            zb = jnp.where(has_left, zb, -MASKED_SCORE)
                        lb, l = softplus_parts(zb)
                        c["lb"][h].append(lb)
                        c["parts"][h].append(suffix_and_total(l))
            yield
            for c in chains:
                c["att"] = []
                for h in range(2):
                    r = jnp.zeros((tk, tk), F32)
                    atts = [None] * (nleft + 1)
                    for kb in reversed(range(nleft + 1)):
                        suffix, total = c["parts"][h][kb]
                        atts[kb] = jnp.exp2(c["lb"][h][kb] + suffix + r).astype(BF16)
                        r = r + total
                    c["att"].append(jnp.concatenate(atts, axis=1))
                    r_ref[2 * g + h, c["rows"], :] = r
            yield
            for c in chains:
                acc_ref[g, c["rows"], :] = pair_output(c["att"][0], c["att"][1], *c["v"])
            yield

    _interleave(attention_stages(), mixer)

    def cond(carry):
        s, alive = carry
        return jnp.logical_and(s <= g0 + (nrb - 1 - nleft - 1), alive)

    def body(carry):
        s, _ = carry
        for g in range(npair):
            q = pair_queries(g)
            for rb in range(nrb):
                jb = g0 + (rb - nleft - 1) - s
                rows = slice(rb * tk, (rb + 1) * tk)
                rows_alive = jnp.max(r_ref[2 * g:2 * g + 2, rows, :]) >= ATT_DEAD_LOG2

                @pl.when(jnp.logical_and(jb >= 0, rows_alive))
                def _():
                    ks = pl.multiple_of(jb * tk, tk)
                    copies = [
                        pltpu.make_async_copy(
                            qkv_hbm.at[bi, pl.ds(ks, tk), pl.ds((1 + p) * att_w + g * LANES, LANES)],
                            dst, sem.at[p])
                        for p, dst in enumerate((kblk_ref, vblk_ref))]
                    for cp in copies:
                        cp.start()
                    for cp in copies:
                        cp.wait()
                    z = pair_scores(q[rows, :], *by_head(kblk_ref[...]))
                    atts = []
                    for h in range(2):
                        lb, l = softplus_parts(z[:, h * tk:(h + 1) * tk])
                        suffix, total = suffix_and_total(l)
                        r_old = r_ref[2 * g + h, rows, :]
                        atts.append(jnp.exp2(lb + suffix + r_old).astype(BF16))
                        r_ref[2 * g + h, rows, :] = r_old + total
                    acc_ref[g, rows, :] += pair_output(*atts, *by_head(vblk_ref[...]))
        return s + 1, jnp.max(r_ref[...]) >= ATT_DEAD_LOG2

    lax.while_loop(cond, body, (0, jnp.max(r_ref[...]) >= ATT_DEAD_LOG2))

    for g in range(npair):
        yatt_ref[slot, :, g * LANES:(g + 1) * LANES] = acc_ref[g]


def _attn_ffn(qkv, x2, y_lru, g_att, w_out, g_ffn, w_ffn_in, w_ffn_out, g_last, tq, tk, nleft,
              ff_chunk, final_norm):
    bsz, s, three_w = qkv.shape
    att_w = three_w // 3
    n, d = x2.shape
    d_ff = w_ffn_out.shape[0]
    tiles_per_seq = s // tq
    n_tiles = bsz * tiles_per_seq
    left = nleft * tk
    assert s % tq == 0 and tq % tk == 0 and tq // tk >= nleft and tq % left == 0
    assert att_w % LANES == 0

    def tile(j):
        ja = jnp.minimum(j, n_tiles - 1)
        return ja // tiles_per_seq, ja % tiles_per_seq

    def cur(part):
        return pl.BlockSpec((1, tq, att_w), lambda j: (*tile(j), part))

    def prev(part):
        def index(j):
            b, t = tile(j)
            return b, jnp.maximum(t * (tq // left) - 1, 0), part
        return pl.BlockSpec((1, left, att_w), index)

    row = lambda w: pl.BlockSpec((tq, w), lambda j: (jnp.maximum(j - 1, 0), 0))
    vec = lambda w: pl.BlockSpec((1, w), lambda j: (0, 0))
    whole = lambda a: pl.BlockSpec(a.shape, lambda j: (0, 0), pipeline_mode=pl.Buffered(1))
    nheads = att_w // HEAD_DIM
    return pl.pallas_call(
        functools.partial(_attn_ffn_kernel, tq=tq, tk=tk, nleft=nleft,
                          scale2=LOG2_E / math.sqrt(HEAD_DIM), tiles_per_seq=tiles_per_seq,
                          n_tiles=n_tiles, att_w=att_w, d_ff=d_ff, ff_chunk=ff_chunk,
                          final_norm=final_norm),
        grid=(n_tiles + 1,),
        in_specs=[cur(0), cur(1), prev(1), cur(2), prev(2),
                  pl.BlockSpec(memory_space=pl.ANY),
                  row(d), row(y_lru.shape[1]), vec(att_w), whole(w_out), vec(d),
                  whole(w_ffn_in), whole(w_ffn_out), vec(d)],
        out_specs=row(d),
        out_shape=jax.ShapeDtypeStruct((n, d), F32),
        scratch_shapes=[
            pltpu.VMEM((2, tq, att_w), F32),
            pltpu.VMEM((nheads, tq, tk), F32),
            pltpu.VMEM((att_w // LANES, tq, LANES), F32),
            pltpu.VMEM((tk, LANES), BF16),
            pltpu.VMEM((tk, LANES), BF16),
            pltpu.SemaphoreType.DMA((2,)),
        ],
        compiler_params=pltpu.CompilerParams(
            dimension_semantics=("arbitrary",), vmem_limit_bytes=VMEM_LIMIT_BYTES),
        name="attn_ffn",
    )(qkv, qkv, qkv, qkv, qkv, qkv, x2, y_lru, g_att, w_out, g_ffn, w_ffn_in, w_ffn_out, g_last)


def _largest_chunk(total, limit):
    best = LANES
    for c in range(LANES, limit + 1, LANES):
        if total % c == 0:
            best = c
    return best


def kernel(x, norm_mix, w_in, conv_w, conv_b, w_rg, b_rg, w_ig, b_ig, lru_lambda, norm_lru_out,
           norm_att_out, w_out, norm_ffn, w_ffn_in, w_ffn_out, norm_final):
    bsz, s, d = x.shape
    depth = w_in.shape[0]
    lru_w = conv_w.shape[2]
    att_w = norm_att_out.shape[1]
    blk_w = w_rg.shape[2]
    assert w_in.shape[2] == 2 * lru_w + 3 * att_w and att_w % LANES == 0
    assert MXU_DIM % blk_w == 0 and lru_w % MXU_DIM == 0

    n = bsz * s
    ts = 512
    assert s % ts == 0
    tq, tk, nleft = 512, 128, 2
    per = MXU_DIM // blk_w
    row1 = lambda v: v.reshape(1, -1)

    x2 = x.reshape(n, d)
    for l in range(depth):
        qkv, y_lru, w_out_b, w_ffn_in_b, w_ffn_out_b = _inproj_lru(
            x2, row1(norm_mix[l]), w_in[l].astype(BF16), conv_w[l], row1(conv_b[l]),
            _block_diag(w_rg[l], per).astype(BF16), row1(b_rg[l]),
            _block_diag(w_ig[l], per).astype(BF16), row1(b_ig[l]), row1(lru_lambda[l]),
            row1(norm_lru_out[l]), (w_out[l], w_ffn_in[l], w_ffn_out[l]), s, ts)
        d_ff = w_ffn_out.shape[1]
        x2 = _attn_ffn(qkv.reshape(bsz, s, 3 * att_w), x2, y_lru, row1(norm_att_out[l]), w_out_b,
                       row1(norm_ffn[l]), w_ffn_in_b, w_ffn_out_b, row1(norm_final),
                       tq, tk, nleft, _largest_chunk(d_ff, 512), l == depth - 1)
    return x2.reshape(bsz, s, d)
```

```python
import functools
import math

import jax
import jax.numpy as jnp
from jax import lax
from jax.experimental import pallas as pl
from jax.experimental.pallas import tpu as pltpu

F32 = jnp.float32
BF16 = jnp.bfloat16

EPS = 1e-6
LRU_C = 8.0
CONV_W = 4
HEAD_DIM = 64

LANES = 128
SUBLANES = 8
MXU_DIM = 256
VMEM_LIMIT_BYTES = 56 * 1024 * 1024

LOG2_E = 1.4426950408889634
ATT_DEAD_LOG2 = -160.0
MASKED_SCORE = 1e30
CAST_STEPS = 8


def _rms_scale(x):
    return lax.rsqrt(jnp.mean(x * x, axis=-1, keepdims=True) + EPS)


def _sigmoid(x):
    return 1.0 / (1.0 + jnp.exp(-x))


def _softplus(x):
    return jnp.maximum(x, 0.0) + jnp.log(1.0 + jnp.exp(-jnp.abs(x)))


def _gelu_tanh(x):
    c = math.sqrt(2.0 / math.pi)
    return 0.5 * x * (1.0 + jnp.tanh(c * (x + 0.044715 * (x * x * x))))


def _inproj_stages(x_ref, g_ref, w_ref, u_slot_ref, qkv_ref, *, lru_cols, col_chunk):
    x = x_ref[...]
    h = ((x * _rms_scale(x)) * g_ref[...]).astype(BF16)
    yield
    for lo in range(0, w_ref.shape[1], col_chunk):
        u = jnp.dot(h, w_ref[:, lo:lo + col_chunk], preferred_element_type=F32)
        if lo < lru_cols:
            u_slot_ref[:, lo:lo + col_chunk] = u
        else:
            qkv_ref[:, lo - lru_cols:lo - lru_cols + col_chunk] = u.astype(BF16)
        yield


def _lru_stages(first, u_ref, cw_ref, cb_ref, wrg_ref, brg_ref, wig_ref, big_ref, lam_ref, gn_ref,
                out_ref, xs_ref, hc_ref, *, ts, width, slab):
    halo = SUBLANES
    xs_ref[0:halo, :] = jnp.where(first, 0.0, xs_ref[ts:ts + halo, :])
    xs_ref[halo:halo + ts, :] = u_ref[:, 0:width]
    row = lax.broadcasted_iota(jnp.int32, (SUBLANES, width), 0)
    hprev = jnp.where(first, 0.0, hc_ref[...])
    neg_sp = -LRU_C * _softplus(-lam_ref[...])

    for s0 in range(0, ts, slab):
        xc = jnp.broadcast_to(cb_ref[...], (slab, width))
        for i in range(CONV_W):
            off = s0 + halo - (CONV_W - 1) + i
            xc = xc + xs_ref[off:off + slab, :] * cw_ref[i:i + 1, :]

        xcb = xc.astype(BF16)
        rg, ig = [], []
        for c in range(width // MXU_DIM):
            sl = slice(c * MXU_DIM, (c + 1) * MXU_DIM)
            gates = jnp.dot(xcb[:, sl], jnp.concatenate([wrg_ref[c], wig_ref[c]], axis=1),
                            preferred_element_type=F32)
            rg.append(gates[:, :MXU_DIM])
            ig.append(gates[:, MXU_DIM:])
        r = _sigmoid(jnp.concatenate(rg, axis=1) + brg_ref[...])
        ig = _sigmoid(jnp.concatenate(ig, axis=1) + big_ref[...])

        log_a = r * neg_sp
        a = jnp.exp(log_a)
        th = jnp.tanh(log_a)
        b = jnp.sqrt((-2.0 * th) / (1.0 - th)) * (ig * xc)
        yield

        hs = []
        for g in range(slab // SUBLANES):
            rows = slice(g * SUBLANES, (g + 1) * SUBLANES)
            a8, b8 = a[rows, :], b[rows, :]
            for d in (1, 2, 4):
                keep = row >= d
                b8 = jnp.where(keep, b8 + a8 * pltpu.roll(b8, d, axis=0), b8)
                a8 = jnp.where(keep, a8 * pltpu.roll(a8, d, axis=0), a8)
            h8 = b8 + a8 * hprev
            hs.append(h8)
            hprev = jnp.broadcast_to(h8[SUBLANES - 1:SUBLANES, :], (SUBLANES, width))
        y = _gelu_tanh(u_ref[s0:s0 + slab, width:2 * width]) * jnp.concatenate(hs, axis=0)
        out_ref[s0:s0 + slab, :] = ((y * _rms_scale(y)) * gn_ref[...]).astype(out_ref.dtype)
        yield
    hc_ref[...] = hprev


def _interleave(*streams):
    streams = list(streams)
    while streams:
        for st in list(streams):
            if next(st, StopIteration) is StopIteration:
                streams.remove(st)


def _inproj_lru_kernel(x_ref, gm_ref, w_ref, cw_ref, cb_ref, wrg_ref, brg_ref, wig_ref, big_ref,
                       lam_ref, gn_ref, *rest, ts, width, tiles_per_seq, n_tiles, col_chunk, slab,
                       cast_slabs):
    nw = len(cast_slabs)
    wsrc_refs, (qkv_ref, ylru_ref) = rest[:nw], rest[nw:nw + 2]
    wdst_refs = rest[nw + 2:2 * nw + 2]
    u_ref, xs_ref, hc_ref = rest[2 * nw + 2:]
    j = pl.program_id(0)
    for src_ref, dst_ref, n_slabs in zip(wsrc_refs, wdst_refs, cast_slabs):
        @pl.when(j < n_slabs)
        def _():
            dst_ref[...] = src_ref[...].astype(dst_ref.dtype)

    slot = j % 2
    recurrent = lambda: _lru_stages(
        (j - 1) % tiles_per_seq == 0, u_ref.at[1 - slot], cw_ref, cb_ref, wrg_ref, brg_ref, wig_ref,
        big_ref, lam_ref, gn_ref, ylru_ref, xs_ref, hc_ref, ts=ts, width=width, slab=slab)
    projection = lambda: _inproj_stages(x_ref, gm_ref, w_ref, u_ref.at[slot], qkv_ref,
                                        lru_cols=2 * width, col_chunk=col_chunk)

    @pl.when(j == 0)
    def _():
        for ref in (xs_ref, hc_ref):
            ref[...] = jnp.zeros(ref.shape, ref.dtype)
        _interleave(projection())

    @pl.when(jnp.logical_and(j > 0, j < n_tiles))
    def _():
        _interleave(recurrent(), projection())

    @pl.when(j == n_tiles)
    def _():
        _interleave(recurrent())


def _cast_slab_rows(rows, max_slabs):
    step = 2 * SUBLANES
    for slab_rows in range(step, rows + 1, step):
        if rows % slab_rows == 0 and rows // slab_rows <= max_slabs:
            return slab_rows
    raise ValueError(f"no bf16 row slab for {rows} rows in {max_slabs} steps")


def _inproj_lru(x2, gain, w_in_bf16, conv_w, conv_b, wrg_bd, b_rg, wig_bd, b_ig, lam, gain_lru,
                later_weights, seq_len, ts):
    n, d = x2.shape
    cols = w_in_bf16.shape[1]
    width = conv_w.shape[1]
    n_tiles = n // ts
    slab_rows = [_cast_slab_rows(w.shape[0], min(n_tiles, CAST_STEPS)) for w in later_weights]
    cast_slabs = tuple(w.shape[0] // r for w, r in zip(later_weights, slab_rows))
    wslab = lambda w, r: pl.BlockSpec(
        (r, w.shape[1]), lambda j: (jnp.minimum(j, w.shape[0] // r - 1), 0))
    wspecs = [wslab(w, r) for w, r in zip(later_weights, slab_rows)]
    tile = lambda lag: (lambda j: (jnp.clip(j - lag, 0, n_tiles - 1), 0))
    vec = lambda w: pl.BlockSpec((1, w), lambda j: (0, 0))
    wspec = pl.BlockSpec(wrg_bd.shape, lambda j: (0, 0, 0))
    return pl.pallas_call(
        functools.partial(_inproj_lru_kernel, ts=ts, width=width, tiles_per_seq=seq_len // ts,
                          n_tiles=n_tiles, col_chunk=MXU_DIM, slab=LANES // 2,
                          cast_slabs=cast_slabs),
        grid=(n_tiles + 1,),
        in_specs=[
            pl.BlockSpec((ts, d), tile(0)),
            vec(d),
            pl.BlockSpec((d, cols), lambda j: (0, 0), pipeline_mode=pl.Buffered(1)),
            pl.BlockSpec((CONV_W, width), lambda j: (0, 0)),
            vec(width), wspec, vec(width), wspec, vec(width), vec(width), vec(width),
            *wspecs,
        ],
        out_specs=[
            pl.BlockSpec((ts, cols - 2 * width), tile(0)),
            pl.BlockSpec((ts, width), tile(1)),
            *wspecs,
        ],
        out_shape=[
            jax.ShapeDtypeStruct((n, cols - 2 * width), BF16),
            jax.ShapeDtypeStruct((n, width), BF16),
            *[jax.ShapeDtypeStruct(w.shape, BF16) for w in later_weights],
        ],
        scratch_shapes=[
            pltpu.VMEM((2, ts, 2 * width), F32),
            pltpu.VMEM((ts + SUBLANES, width), F32),
            pltpu.VMEM((SUBLANES, width), F32),
        ],
        compiler_params=pltpu.CompilerParams(
            dimension_semantics=("arbitrary",), vmem_limit_bytes=VMEM_LIMIT_BYTES),
        name="inproj_lru",
    )(x2, gain, w_in_bf16, conv_w, conv_b, wrg_bd, b_rg, wig_bd, b_ig, lam, gain_lru,
      *later_weights)


def _block_diag(w, per):
    nb, bw, _ = w.shape
    w = w.reshape(nb // per, per, bw, bw)
    eye = jnp.eye(per, dtype=w.dtype)
    return jnp.einsum('gpij,pq->gpiqj', w, eye).reshape(nb // per, per * bw, per * bw)


def _ffn_stages(x_ref, yl_ref, ya, ga_ref, wo_ref, gf_ref, wi_ref, wd_ref, gl_ref, out_ref, *,
                d_ff, ff_chunk, final_norm):
    ya = ((ya * _rms_scale(ya)) * ga_ref[...]).astype(BF16)
    wl = yl_ref.shape[1]
    x1 = x_ref[...] + jnp.dot(yl_ref[...], wo_ref[0:wl, :], preferred_element_type=F32)
    x1 = x1 + jnp.dot(ya, wo_ref[wl:, :], preferred_element_type=F32)
    h2 = ((x1 * _rms_scale(x1)) * gf_ref[...]).astype(BF16)
    yield
    acts = []
    for c in range(d_ff // ff_chunk):
        lo = c * ff_chunk
        gate = jnp.dot(h2, wi_ref[:, lo:lo + ff_chunk], preferred_element_type=F32)
        up = jnp.dot(h2, wi_ref[:, d_ff + lo:d_ff + lo + ff_chunk], preferred_element_type=F32)
        acts.append(((gate * _sigmoid(gate)) * up).astype(BF16))
        yield
    out = x1 + jnp.dot(jnp.concatenate(acts, axis=1), wd_ref[...], preferred_element_type=F32)
    yield
    if final_norm:
        out = (out * _rms_scale(out)) * gl_ref[...]
    out_ref[...] = out


def _attn_ffn_kernel(q_ref, kc_ref, kp_ref, vc_ref, vp_ref, qkv_hbm, x_ref, yl_ref, ga_ref,
                     wo_ref, gf_ref, wi_ref, wd_ref, gl_ref, out_ref,
                     yatt_ref, r_ref, acc_ref, kblk_ref, vblk_ref, sem,
                     *, tq, tk, nleft, scale2, tiles_per_seq, n_tiles, att_w, d_ff, ff_chunk,
                     final_norm):
    j = pl.program_id(0)
    ja = jnp.minimum(j, n_tiles - 1)
    bi = ja // tiles_per_seq
    ti = ja % tiles_per_seq
    nrb = tq // tk
    g0 = ti * nrb
    npair = att_w // LANES
    slot = j % 2

    mixer = lambda: _ffn_stages(x_ref, yl_ref, yatt_ref[1 - slot], ga_ref, wo_ref, gf_ref, wi_ref,
                                wd_ref, gl_ref, out_ref, d_ff=d_ff, ff_chunk=ff_chunk,
                                final_norm=final_norm)

    rr = lax.broadcasted_iota(jnp.int32, (2 * tk, 2 * tk), 0)
    cc = lax.broadcasted_iota(jnp.int32, (2 * tk, 2 * tk), 1)
    key_j = jnp.where(rr >= tk, rr - tk, rr)
    tri = jnp.where((cc >= tk) | (key_j > cc), 1.0, 0.0).astype(BF16)
    below_diag = (lax.broadcasted_iota(jnp.int32, (tk, tk), 1)
                  < lax.broadcasted_iota(jnp.int32, (tk, tk), 0))
    has_left = ti > 0

    def softplus_parts(z):
        lb = jnp.minimum(z, 0.0) - jnp.log(1.0 + jnp.exp2(-jnp.abs(z))) * LOG2_E
        return lb, lb - z

    def suffix_and_total(l):
        hi = l.astype(BF16)
        lo = (l - hi.astype(F32)).astype(BF16)
        cs = jnp.dot(jnp.concatenate([hi, lo], axis=1), tri, preferred_element_type=F32)
        return cs[:, :tk], cs[:, tk:]

    def pair_queries(g):
        return (q_ref[0, :, g * LANES:(g + 1) * LANES].astype(F32) * scale2).astype(BF16)

    def by_head(kv):
        lane = lax.broadcasted_iota(jnp.int32, kv.shape, 1)
        zero = jnp.zeros_like(kv)
        return jnp.where(lane < HEAD_DIM, kv, zero), jnp.where(lane >= HEAD_DIM, kv, zero)

    def pair_scores(q, k_a, k_b):
        return lax.dot_general(q, jnp.concatenate([k_a, k_b], axis=0), (((1,), (1,)), ((), ())),
                               preferred_element_type=F32)

    def pair_output(att_a, att_b, v_a, v_b):
        return jnp.dot(jnp.concatenate([att_a, att_b], axis=1),
                       jnp.concatenate([v_a, v_b], axis=0), preferred_element_type=F32)

    def attention_stages():
        nk = (nleft + 1) * tk
        for g in range(npair):
            cols = slice(g * LANES, (g + 1) * LANES)
            q = pair_queries(g)
            k_heads = by_head(jnp.concatenate([kp_ref[0, :, cols], kc_ref[0, :, cols]], axis=0))
            v_heads = by_head(jnp.concatenate([vp_ref[0, :, cols], vc_ref[0, :, cols]], axis=0))
            chains = []
            for rb in range(nrb):
                rows = slice(rb * tk, (rb + 1) * tk)
                win = slice(rb * tk, (rb + nleft + 1) * tk)
                z = pair_scores(q[rows, :], k_heads[0][win, :], k_heads[1][win, :])
                chains.append(dict(rb=rb, rows=rows, v=(v_heads[0][win, :], v_heads[1][win, :]),
                                   z=(z[:, :nk], z[:, nk:])))
            yield
            for c in chains:
                c["lb"], c["parts"] = ([], []), ([], [])
                for h in range(2):
                    for kb in range(nleft + 1):
                        zb = c["z"][h][:, kb * tk:(kb + 1) * tk]
                        if kb == nleft:
                            zb = jnp.where(below_diag, zb, -MASKED_SCORE)
                        elif c["rb"] + kb < nleft:
                            zb = jnp.where(has_left, zb, -MASKED_SCORE)
                        lb, l = softplus_parts(zb)
                        c["lb"][h].append(lb)
                        c["parts"][h].append(suffix_and_total(l))
            yield
            for c in chains:
                c["att"] = []
                for h in range(2):
                    r = jnp.zeros((tk, tk), F32)
                    atts = [None] * (nleft + 1)
                    for kb in reversed(range(nleft + 1)):
                        suffix, total = c["parts"][h][kb]
                        atts[kb] = jnp.exp2(c["lb"][h][kb] + suffix + r).astype(BF16)
                        r = r + total
                    c["att"].append(jnp.concatenate(atts, axis=1))
                    r_ref[2 * g + h, c["rows"], :] = r
            yield
            for c in chains:
                acc_ref[g, c["rows"], :] = pair_output(c["att"][0], c["att"][1], *c["v"])
            yield

    @pl.when(j == 0)
    def _():
        _interleave(attention_stages())

    @pl.when(jnp.logical_and(j > 0, j < n_tiles))
    def _():
        _interleave(attention_stages(), mixer())

    @pl.when(j == n_tiles)
    def _():
        _interleave(mixer())

    has_tile = j < n_tiles

    def cond(carry):
        s, alive = carry
        return jnp.logical_and(s <= g0 + (nrb - 1 - nleft - 1), jnp.logical_and(alive, has_tile))

    def body(carry):
        s, _ = carry
        for g in range(npair):
            q = pair_queries(g)
            for rb in range(nrb):
                jb = g0 + (rb - nleft - 1) - s
                rows = slice(rb * tk, (rb + 1) * tk)
                rows_alive = jnp.max(r_ref[2 * g:2 * g + 2, rows, :]) >= ATT_DEAD_LOG2

                @pl.when(jnp.logical_and(jb >= 0, rows_alive))
                def _():
                    ks = pl.multiple_of(jb * tk, tk)
                    copies = [
                        pltpu.make_async_copy(
                            qkv_hbm.at[bi, pl.ds(ks, tk), pl.ds((1 + p) * att_w + g * LANES, LANES)],
                            dst, sem.at[p])
                        for p, dst in enumerate((kblk_ref, vblk_ref))]
                    for cp in copies:
                        cp.start()
                    for cp in copies:
                        cp.wait()
                    z = pair_scores(q[rows, :], *by_head(kblk_ref[...]))
                    atts = []
                    for h in range(2):
                        lb, l = softplus_parts(z[:, h * tk:(h + 1) * tk])
                        suffix, total = suffix_and_total(l)
                        r_old = r_ref[2 * g + h, rows, :]
                        atts.append(jnp.exp2(lb + suffix + r_old).astype(BF16))
                        r_ref[2 * g + h, rows, :] = r_old + total
                    acc_ref[g, rows, :] += pair_output(*atts, *by_head(vblk_ref[...]))
        return s + 1, jnp.max(r_ref[...]) >= ATT_DEAD_LOG2

    lax.while_loop(cond, body, (0, jnp.max(r_ref[...]) >= ATT_DEAD_LOG2))

    for g in range(npair):
        yatt_ref[slot, :, g * LANES:(g + 1) * LANES] = acc_ref[g]


def _attn_ffn(qkv, x2, y_lru, g_att, w_out, g_ffn, w_ffn_in, w_ffn_out, g_last, tq, tk, nleft,
              ff_chunk, final_norm):
    bsz, s, three_w = qkv.shape
    att_w = three_w // 3
    n, d = x2.shape
    d_ff = w_ffn_out.shape[0]
    tiles_per_seq = s // tq
    n_tiles = bsz * tiles_per_seq
    left = nleft * tk
    assert s % tq == 0 and tq % tk == 0 and tq // tk >= nleft and tq % left == 0
    assert att_w % LANES == 0

    def tile(j):
        ja = jnp.minimum(j, n_tiles - 1)
        return ja // tiles_per_seq, ja % tiles_per_seq

    def cur(part):
        return pl.BlockSpec((1, tq, att_w), lambda j: (*tile(j), part))

    def prev(part):
        def index(j):
            b, t = tile(j)
            return b, jnp.maximum(t * (tq // left) - 1, 0), part
        return pl.BlockSpec((1, left, att_w), index)

    row = lambda w: pl.BlockSpec((tq, w), lambda j: (jnp.maximum(j - 1, 0), 0))
    vec = lambda w: pl.BlockSpec((1, w), lambda j: (0, 0))
    whole = lambda a: pl.BlockSpec(a.shape, lambda j: (0, 0), pipeline_mode=pl.Buffered(1))
    nheads = att_w // HEAD_DIM
    return pl.pallas_call(
        functools.partial(_attn_ffn_kernel, tq=tq, tk=tk, nleft=nleft,
                          scale2=LOG2_E / math.sqrt(HEAD_DIM), tiles_per_seq=tiles_per_seq,
                          n_tiles=n_tiles, att_w=att_w, d_ff=d_ff, ff_chunk=ff_chunk,
                          final_norm=final_norm),
        grid=(n_tiles + 1,),
        in_specs=[cur(0), cur(1), prev(1), cur(2), prev(2),
                  pl.BlockSpec(memory_space=pl.ANY),
                  row(d), row(y_lru.shape[1]), vec(att_w), whole(w_out), vec(d),
                  whole(w_ffn_in), whole(w_ffn_out), vec(d)],
        out_specs=row(d),
        out_shape=jax.ShapeDtypeStruct((n, d), F32),
        scratch_shapes=[
            pltpu.VMEM((2, tq, att_w), F32),
            pltpu.VMEM((nheads, tq, tk), F32),
            pltpu.VMEM((att_w // LANES, tq, LANES), F32),
            pltpu.VMEM((tk, LANES), BF16),
            pltpu.VMEM((tk, LANES), BF16),
            pltpu.SemaphoreType.DMA((2,)),
        ],
        compiler_params=pltpu.CompilerParams(
            dimension_semantics=("arbitrary",), vmem_limit_bytes=VMEM_LIMIT_BYTES),
        name="attn_ffn",
    )(qkv, qkv, qkv, qkv, qkv, qkv, x2, y_lru, g_att, w_out, g_ffn, w_ffn_in, w_ffn_out, g_last)


def _largest_chunk(total, limit):
    best = LANES
    for c in range(LANES, limit + 1, LANES):
        if total % c == 0:
            best = c
    return best


def kernel(x, norm_mix, w_in, conv_w, conv_b, w_rg, b_rg, w_ig, b_ig, lru_lambda, norm_lru_out,
           norm_att_out, w_out, norm_ffn, w_ffn_in, w_ffn_out, norm_final):
    bsz, s, d = x.shape
    depth = w_in.shape[0]
    lru_w = conv_w.shape[2]
    att_w = norm_att_out.shape[1]
    blk_w = w_rg.shape[2]
    assert w_in.shape[2] == 2 * lru_w + 3 * att_w and att_w % LANES == 0
    assert MXU_DIM % blk_w == 0 and lru_w % MXU_DIM == 0

    n = bsz * s
    ts = 512
    assert s % ts == 0
    tq, tk, nleft = 512, 128, 2
    per = MXU_DIM // blk_w
    row1 = lambda v: v.reshape(1, -1)

    x2 = x.reshape(n, d)
    for l in range(depth):
        qkv, y_lru, w_out_b, w_ffn_in_b, w_ffn_out_b = _inproj_lru(
            x2, row1(norm_mix[l]), w_in[l].astype(BF16), conv_w[l], row1(conv_b[l]),
            _block_diag(w_rg[l], per).astype(BF16), row1(b_rg[l]),
            _block_diag(w_ig[l], per).astype(BF16), row1(b_ig[l]), row1(lru_lambda[l]),
            row1(norm_lru_out[l]), (w_out[l], w_ffn_in[l], w_ffn_out[l]), s, ts)
        d_ff = w_ffn_out.shape[1]
        x2 = _attn_ffn(qkv.reshape(bsz, s, 3 * att_w), x2, y_lru, row1(norm_att_out[l]), w_out_b,
                       row1(norm_ffn[l]), w_ffn_in_b, w_ffn_out_b, row1(norm_final),
                       tq, tk, nleft, _largest_chunk(d_ff, 512), l == depth - 1)
    return x2.reshape(bsz, s, d)
```

```python
import functools
import math

import jax
import jax.numpy as jnp
from jax import lax
from jax.experimental import pallas as pl
from jax.experimental.pallas import tpu as pltpu

F32 = jnp.float32
BF16 = jnp.bfloat16

EPS = 1e-6
LRU_C = 8.0
CONV_W = 4
HEAD_DIM = 64

LANES = 128
SUBLANES = 8
MXU_DIM = 256
VMEM_LIMIT_BYTES = 56 * 1024 * 1024

LOG2_E = 1.4426950408889634
ATT_DEAD_LOG2 = -160.0
MASKED_SCORE = 1e30
CAST_STEPS = 8


def _rms_scale(x):
    return lax.rsqrt(jnp.mean(x * x, axis=-1, keepdims=True) + EPS)


def _sigmoid(x):
    return 1.0 / (1.0 + jnp.exp(-x))


def _softplus(x):
    return jnp.maximum(x, 0.0) + jnp.log(1.0 + jnp.exp(-jnp.abs(x)))


def _gelu_tanh(x):
    c = math.sqrt(2.0 / math.pi)
    return 0.5 * x * (1.0 + jnp.tanh(c * (x + 0.044715 * (x * x * x))))


def _inproj_stages(x_ref, g_ref, w_ref, u_slot_ref, qkv_ref, *, lru_cols, col_chunk):
    x = x_ref[...]
    h = ((x * _rms_scale(x)) * g_ref[...]).astype(BF16)
    yield
    for lo in range(0, w_ref.shape[1], col_chunk):
        u = jnp.dot(h, w_ref[:, lo:lo + col_chunk], preferred_element_type=F32)
        if lo < lru_cols:
            u_slot_ref[:, lo:lo + col_chunk] = u
        else:
            qkv_ref[:, lo - lru_cols:lo - lru_cols + col_chunk] = u.astype(BF16)
        yield


def _lru_stages(first, u_ref, cw_ref, cb_ref, wrg_ref, brg_ref, wig_ref, big_ref, lam_ref, gn_ref,
                out_ref, xs_ref, hc_ref, *, ts, width, slab):
    halo = SUBLANES
    xs_ref[0:halo, :] = jnp.where(first, 0.0, xs_ref[ts:ts + halo, :])
    xs_ref[halo:halo + ts, :] = u_ref[:, 0:width]
    row = lax.broadcasted_iota(jnp.int32, (SUBLANES, width), 0)
    hprev = jnp.where(first, 0.0, hc_ref[...])
    neg_sp = -LRU_C * _softplus(-lam_ref[...])

    for s0 in range(0, ts, slab):
        xc = jnp.broadcast_to(cb_ref[...], (slab, width))
        for i in range(CONV_W):
            off = s0 + halo - (CONV_W - 1) + i
            xc = xc + xs_ref[off:off + slab, :] * cw_ref[i:i + 1, :]

        xcb = xc.astype(BF16)
        rg, ig = [], []
        for c in range(width // MXU_DIM):
            sl = slice(c * MXU_DIM, (c + 1) * MXU_DIM)
            gates = jnp.dot(xcb[:, sl], jnp.concatenate([wrg_ref[c], wig_ref[c]], axis=1),
                            preferred_element_type=F32)
            rg.append(gates[:, :MXU_DIM])
            ig.append(gates[:, MXU_DIM:])
        r = _sigmoid(jnp.concatenate(rg, axis=1) + brg_ref[...])
        ig = _sigmoid(jnp.concatenate(ig, axis=1) + big_ref[...])

        log_a = r * neg_sp
        a = jnp.exp(log_a)
        th = jnp.tanh(log_a)
        b = jnp.sqrt((-2.0 * th) / (1.0 - th)) * (ig * xc)
        yield

        hs = []
        for g in range(slab // SUBLANES):
            rows = slice(g * SUBLANES, (g + 1) * SUBLANES)
            a8, b8 = a[rows, :], b[rows, :]
            for d in (1, 2, 4):
                keep = row >= d
                b8 = jnp.where(keep, b8 + a8 * pltpu.roll(b8, d, axis=0), b8)
                a8 = jnp.where(keep, a8 * pltpu.roll(a8, d, axis=0), a8)
            h8 = b8 + a8 * hprev
            hs.append(h8)
            hprev = jnp.broadcast_to(h8[SUBLANES - 1:SUBLANES, :], (SUBLANES, width))
        y = _gelu_tanh(u_ref[s0:s0 + slab, width:2 * width]) * jnp.concatenate(hs, axis=0)
        out_ref[s0:s0 + slab, :] = ((y * _rms_scale(y)) * gn_ref[...]).astype(out_ref.dtype)
        yield
    hc_ref[...] = hprev


def _interleave(*streams):
    streams = list(streams)
    while streams:
        for st in list(streams):
            if next(st, StopIteration) is StopIteration:
                streams.remove(st)


def _inproj_lru_kernel(x_ref, gm_ref, w_ref, cw_ref, cb_ref, wrg_ref, brg_ref, wig_ref, big_ref,
                       lam_ref, gn_ref, *rest, ts, width, tiles_per_seq, col_chunk, slab,
                       cast_slabs):
    nw = len(cast_slabs)
    wsrc_refs, (qkv_ref, ylru_ref) = rest[:nw], rest[nw:nw + 2]
    wdst_refs = rest[nw + 2:2 * nw + 2]
    u_ref, xs_ref, hc_ref = rest[2 * nw + 2:]
    j = pl.program_id(0)
    for src_ref, dst_ref, n_slabs in zip(wsrc_refs, wdst_refs, cast_slabs):
        @pl.when(j < n_slabs)
        def _():
            dst_ref[...] = src_ref[...].astype(dst_ref.dtype)

    slot = j % 2
    lt = jnp.maximum(j - 1, 0)

    @pl.when(j == 0)
    def _():
        for ref in (u_ref, xs_ref, hc_ref):
            ref[...] = jnp.zeros(ref.shape, ref.dtype)

    _interleave(
        _lru_stages(lt % tiles_per_seq == 0, u_ref.at[1 - slot], cw_ref, cb_ref, wrg_ref, brg_ref,
                    wig_ref, big_ref, lam_ref, gn_ref, ylru_ref, xs_ref, hc_ref,
                    ts=ts, width=width, slab=slab),
        _inproj_stages(x_ref, gm_ref, w_ref, u_ref.at[slot], qkv_ref, lru_cols=2 * width,
                       col_chunk=col_chunk))


def _cast_slab_rows(rows, max_slabs):
    step = 2 * SUBLANES
    for slab_rows in range(step, rows + 1, step):
        if rows % slab_rows == 0 and rows // slab_rows <= max_slabs:
            return slab_rows
    raise ValueError(f"no bf16 row slab for {rows} rows in {max_slabs} steps")


def _inproj_lru(x2, gain, w_in_bf16, conv_w, conv_b, wrg_bd, b_rg, wig_bd, b_ig, lam, gain_lru,
                later_weights, seq_len, ts):
    n, d = x2.shape
    cols = w_in_bf16.shape[1]
    width = conv_w.shape[1]
    n_tiles = n // ts
    slab_rows = [_cast_slab_rows(w.shape[0], min(n_tiles, CAST_STEPS)) for w in later_weights]
    cast_slabs = tuple(w.shape[0] // r for w, r in zip(later_weights, slab_rows))
    wslab = lambda w, r: pl.BlockSpec(
        (r, w.shape[1]), lambda j: (jnp.minimum(j, w.shape[0] // r - 1), 0))
    wspecs = [wslab(w, r) for w, r in zip(later_weights, slab_rows)]
    tile = lambda lag: (lambda j: (jnp.clip(j - lag, 0, n_tiles - 1), 0))
    vec = lambda w: pl.BlockSpec((1, w), lambda j: (0, 0))
    wspec = pl.BlockSpec(wrg_bd.shape, lambda j: (0, 0, 0))
    return pl.pallas_call(
        functools.partial(_inproj_lru_kernel, ts=ts, width=width, tiles_per_seq=seq_len // ts,
                          col_chunk=MXU_DIM, slab=LANES // 2, cast_slabs=cast_slabs),
        grid=(n_tiles + 1,),
        in_specs=[
            pl.BlockSpec((ts, d), tile(0)),
            vec(d),
            pl.BlockSpec((d, cols), lambda j: (0, 0), pipeline_mode=pl.Buffered(1)),
            pl.BlockSpec((CONV_W, width), lambda j: (0, 0)),
            vec(width), wspec, vec(width), wspec, vec(width), vec(width), vec(width),
            *wspecs,
        ],
        out_specs=[
            pl.BlockSpec((ts, cols - 2 * width), tile(0)),
            pl.BlockSpec((ts, width), tile(1)),
            *wspecs,
        ],
        out_shape=[
            jax.ShapeDtypeStruct((n, cols - 2 * width), BF16),
            jax.ShapeDtypeStruct((n, width), BF16),
            *[jax.ShapeDtypeStruct(w.shape, BF16) for w in later_weights],
        ],
        scratch_shapes=[
            pltpu.VMEM((2, ts, 2 * width), F32),
            pltpu.VMEM((ts + SUBLANES, width), F32),
            pltpu.VMEM((SUBLANES, width), F32),
        ],
        compiler_params=pltpu.CompilerParams(
            dimension_semantics=("arbitrary",), vmem_limit_bytes=VMEM_LIMIT_BYTES),
        name="inproj_lru",
    )(x2, gain, w_in_bf16, conv_w, conv_b, wrg_bd, b_rg, wig_bd, b_ig, lam, gain_lru,
      *later_weights)


def _block_diag(w, per):
    nb, bw, _ = w.shape
    w = w.reshape(nb // per, per, bw, bw)
    eye = jnp.eye(per, dtype=w.dtype)
    return jnp.einsum('gpij,pq->gpiqj', w, eye).reshape(nb // per, per * bw, per * bw)


def _ffn_stages(x_ref, yl_ref, ya, ga_ref, wo_ref, gf_ref, wi_ref, wd_ref, gl_ref, out_ref, *,
                d_ff, ff_chunk, final_norm):
    ya = ((ya * _rms_scale(ya)) * ga_ref[...]).astype(BF16)
    wl = yl_ref.shape[1]
    x1 = x_ref[...] + jnp.dot(yl_ref[...], wo_ref[0:wl, :], preferred_element_type=F32)
    x1 = x1 + jnp.dot(ya, wo_ref[wl:, :], preferred_element_type=F32)
    h2 = ((x1 * _rms_scale(x1)) * gf_ref[...]).astype(BF16)
    yield
    acts = []
    for c in range(d_ff // ff_chunk):
        lo = c * ff_chunk
        gate = jnp.dot(h2, wi_ref[:, lo:lo + ff_chunk], preferred_element_type=F32)
        up = jnp.dot(h2, wi_ref[:, d_ff + lo:d_ff + lo + ff_chunk], preferred_element_type=F32)
        acts.append(((gate * _sigmoid(gate)) * up).astype(BF16))
        yield
    out = x1 + jnp.dot(jnp.concatenate(acts, axis=1), wd_ref[...], preferred_element_type=F32)
    yield
    if final_norm:
        out = (out * _rms_scale(out)) * gl_ref[...]
    out_ref[...] = out


def _attn_ffn_kernel(cur_ref, prev_ref, qkv_hbm, x_ref, yl_ref, ga_ref,
                     wo_ref, gf_ref, wi_ref, wd_ref, gl_ref, out_ref,
                     yatt_ref, r_ref, acc_ref, kblk_ref, vblk_ref, sem,
                     *, tq, tk, nleft, scale2, tiles_per_seq, n_tiles, att_w, d_ff, ff_chunk,
                     final_norm):
    j = pl.program_id(0)
    ja = jnp.minimum(j, n_tiles - 1)
    bi = ja // tiles_per_seq
    ti = ja % tiles_per_seq
    nrb = tq // tk
    g0 = ti * nrb
    npair = att_w // LANES
    slot = j % 2

    @pl.when(j == 0)
    def _():
        yatt_ref[...] = jnp.zeros(yatt_ref.shape, F32)

    mixer = _ffn_stages(x_ref, yl_ref, yatt_ref[1 - slot], ga_ref, wo_ref, gf_ref, wi_ref, wd_ref,
                        gl_ref, out_ref, d_ff=d_ff, ff_chunk=ff_chunk, final_norm=final_norm)

    rr = lax.broadcasted_iota(jnp.int32, (2 * tk, 2 * tk), 0)
    cc = lax.broadcasted_iota(jnp.int32, (2 * tk, 2 * tk), 1)
    key_j = jnp.where(rr >= tk, rr - tk, rr)
    tri = jnp.where((cc >= tk) | (key_j > cc), 1.0, 0.0).astype(BF16)
    below_diag = (lax.broadcasted_iota(jnp.int32, (tk, tk), 1)
                  < lax.broadcasted_iota(jnp.int32, (tk, tk), 0))
    has_left = ti > 0

    def softplus_parts(z):
        lb = jnp.minimum(z, 0.0) - jnp.log(1.0 + jnp.exp2(-jnp.abs(z))) * LOG2_E
        return lb, lb - z

    def suffix_and_total(l):
        hi = l.astype(BF16)
        lo = (l - hi.astype(F32)).astype(BF16)
        cs = jnp.dot(jnp.concatenate([hi, lo], axis=1), tri, preferred_element_type=F32)
        return cs[:, :tk], cs[:, tk:]

    def pair_queries(g):
        return (cur_ref[0, :, g * LANES:(g + 1) * LANES].astype(F32) * scale2).astype(BF16)

    def by_head(kv):
        lane = lax.broadcasted_iota(jnp.int32, kv.shape, 1)
        zero = jnp.zeros_like(kv)
        return jnp.where(lane < HEAD_DIM, kv, zero), jnp.where(lane >= HEAD_DIM, kv, zero)

    def pair_scores(q, k_a, k_b):
        return lax.dot_general(q, jnp.concatenate([k_a, k_b], axis=0), (((1,), (1,)), ((), ())),
                               preferred_element_type=F32)

    def pair_output(att_a, att_b, v_a, v_b):
        return jnp.dot(jnp.concatenate([att_a, att_b], axis=1),
                       jnp.concatenate([v_a, v_b], axis=0), preferred_element_type=F32)

    def attention_stages():
        nk = (nleft + 1) * tk
        for g in range(npair):
            cols = slice(g * LANES, (g + 1) * LANES)
            q = pair_queries(g)
            window = lambda part: jnp.concatenate(
                [ref[0, :, part * att_w + g * LANES:part * att_w + (g + 1) * LANES]
                 for ref in (prev_ref, cur_ref)], axis=0)
            k_heads = by_head(window(1))
            v_heads = by_head(window(2))
            chains = []
            for rb in range(nrb):
                rows = slice(rb * tk, (rb + 1) * tk)
                win = slice(rb * tk, (rb + nleft + 1) * tk)
                z = pair_scores(q[rows, :], k_heads[0][win, :], k_heads[1][win, :])
                chains.append(dict(rb=rb, rows=rows, v=(v_heads[0][win, :], v_heads[1][win, :]),
                                   z=(z[:, :nk], z[:, nk:])))
            yield
            for c in chains:
                c["lb"], c["parts"] = ([], []), ([], [])
                for h in range(2):
                    for kb in range(nleft + 1):
                        zb = c["z"][h][:, kb * tk:(kb + 1) * tk]
                        if kb == nleft:
                            zb = jnp.where(below_diag, zb, -MASKED_SCORE)
                        elif c["rb"] + kb < nleft:
                            zb = jnp.where(has_left, zb, -MASKED_SCORE)
                        lb, l = softplus_parts(zb)
                        c["lb"][h].append(lb)
                        c["parts"][h].append(suffix_and_total(l))
            yield
            for c in chains:
                c["att"] = []
                for h in range(2):
                    r = jnp.zeros((tk, tk), F32)
                    atts = [None] * (nleft + 1)
                    for kb in reversed(range(nleft + 1)):
                        suffix, total = c["parts"][h][kb]
                        atts[kb] = jnp.exp2(c["lb"][h][kb] + suffix + r).astype(BF16)
                        r = r + total
                    c["att"].append(jnp.concatenate(atts, axis=1))
                    r_ref[2 * g + h, c["rows"], :] = r
            yield
            for c in chains:
                acc_ref[g, c["rows"], :] = pair_output(c["att"][0], c["att"][1], *c["v"])
            yield

    _interleave(attention_stages(), mixer)

    def cond(carry):
        s, alive = carry
        return jnp.logical_and(s <= g0 + (nrb - 1 - nleft - 1), alive)

    def body(carry):
        s, _ = carry
        for g in range(npair):
            q = pair_queries(g)
            for rb in range(nrb):
                jb = g0 + (rb - nleft - 1) - s
                rows = slice(rb * tk, (rb + 1) * tk)
                rows_alive = jnp.max(r_ref[2 * g:2 * g + 2, rows, :]) >= ATT_DEAD_LOG2

                @pl.when(jnp.logical_and(jb >= 0, rows_alive))
                def _():
                    ks = pl.multiple_of(jb * tk, tk)
                    copies = [
                        pltpu.make_async_copy(
                            qkv_hbm.at[bi, pl.ds(ks, tk), pl.ds((1 + p) * att_w + g * LANES, LANES)],
                            dst, sem.at[p])
                        for p, dst in enumerate((kblk_ref, vblk_ref))]
                    for cp in copies:
                        cp.start()
                    for cp in copies:
                        cp.wait()
                    z = pair_scores(q[rows, :], *by_head(kblk_ref[...]))
                    atts = []
                    for h in range(2):
                        lb, l = softplus_parts(z[:, h * tk:(h + 1) * tk])
                        suffix, total = suffix_and_total(l)
                        r_old = r_ref[2 * g + h, rows, :]
                        atts.append(jnp.exp2(lb + suffix + r_old).astype(BF16))
                        r_ref[2 * g + h, rows, :] = r_old + total
                    acc_ref[g, rows, :] += pair_output(*atts, *by_head(vblk_ref[...]))
        return s + 1, jnp.max(r_ref[...]) >= ATT_DEAD_LOG2

    lax.while_loop(cond, body, (0, jnp.max(r_ref[...]) >= ATT_DEAD_LOG2))

    for g in range(npair):
        yatt_ref[slot, :, g * LANES:(g + 1) * LANES] = acc_ref[g]


def _attn_ffn(qkv, x2, y_lru, g_att, w_out, g_ffn, w_ffn_in, w_ffn_out, g_last, tq, tk, nleft,
              ff_chunk, final_norm):
    bsz, s, three_w = qkv.shape
    att_w = three_w // 3
    n, d = x2.shape
    d_ff = w_ffn_out.shape[0]
    tiles_per_seq = s // tq
    n_tiles = bsz * tiles_per_seq
    left = nleft * tk
    assert s % tq == 0 and tq % tk == 0 and tq // tk >= nleft and tq % left == 0
    assert att_w % LANES == 0

    def tile(j):
        ja = jnp.minimum(j, n_tiles - 1)
        return ja // tiles_per_seq, ja % tiles_per_seq

    def prev_index(j):
        b, t = tile(j)
        return b, jnp.maximum(t * (tq // left) - 1, 0), 0

    cur = pl.BlockSpec((1, tq, three_w), lambda j: (*tile(j), 0))
    prev = pl.BlockSpec((1, left, three_w), prev_index)

    row = lambda w: pl.BlockSpec((tq, w), lambda j: (jnp.maximum(j - 1, 0), 0))
    vec = lambda w: pl.BlockSpec((1, w), lambda j: (0, 0))
    whole = lambda a: pl.BlockSpec(a.shape, lambda j: (0, 0), pipeline_mode=pl.Buffered(1))
    nheads = att_w // HEAD_DIM
    return pl.pallas_call(
        functools.partial(_attn_ffn_kernel, tq=tq, tk=tk, nleft=nleft,
                          scale2=LOG2_E / math.sqrt(HEAD_DIM), tiles_per_seq=tiles_per_seq,
                          n_tiles=n_tiles, att_w=att_w, d_ff=d_ff, ff_chunk=ff_chunk,
                          final_norm=final_norm),
        grid=(n_tiles + 1,),
        in_specs=[cur, prev, pl.BlockSpec(memory_space=pl.ANY),
                  row(d), row(y_lru.shape[1]), vec(att_w), whole(w_out), vec(d),
                  whole(w_ffn_in), whole(w_ffn_out), vec(d)],
        out_specs=row(d),
        out_shape=jax.ShapeDtypeStruct((n, d), F32),
        scratch_shapes=[
            pltpu.VMEM((2, tq, att_w), F32),
            pltpu.VMEM((nheads, tq, tk), F32),
            pltpu.VMEM((att_w // LANES, tq, LANES), F32),
            pltpu.VMEM((tk, LANES), BF16),
            pltpu.VMEM((tk, LANES), BF16),
            pltpu.SemaphoreType.DMA((2,)),
        ],
        compiler_params=pltpu.CompilerParams(
            dimension_semantics=("arbitrary",), vmem_limit_bytes=VMEM_LIMIT_BYTES),
        name="attn_ffn",
    )(qkv, qkv, qkv, x2, y_lru, g_att, w_out, g_ffn, w_ffn_in, w_ffn_out, g_last)


def _largest_chunk(total, limit):
    best = LANES
    for c in range(LANES, limit + 1, LANES):
        if total % c == 0:
            best = c
    return best


def kernel(x, norm_mix, w_in, conv_w, conv_b, w_rg, b_rg, w_ig, b_ig, lru_lambda, norm_lru_out,
           norm_att_out, w_out, norm_ffn, w_ffn_in, w_ffn_out, norm_final):
    bsz, s, d = x.shape
    depth = w_in.shape[0]
    lru_w = conv_w.shape[2]
    att_w = norm_att_out.shape[1]
    blk_w = w_rg.shape[2]
    assert w_in.shape[2] == 2 * lru_w + 3 * att_w and att_w % LANES == 0
    assert MXU_DIM % blk_w == 0 and lru_w % MXU_DIM == 0

    n = bsz * s
    ts = 512
    assert s % ts == 0
    tq, tk, nleft = 512, 128, 2
    per = MXU_DIM // blk_w
    row1 = lambda v: v.reshape(1, -1)

    x2 = x.reshape(n, d)
    for l in range(depth):
        qkv, y_lru, w_out_b, w_ffn_in_b, w_ffn_out_b = _inproj_lru(
            x2, row1(norm_mix[l]), w_in[l].astype(BF16), conv_w[l], row1(conv_b[l]),
            _block_diag(w_rg[l], per).astype(BF16), row1(b_rg[l]),
            _block_diag(w_ig[l], per).astype(BF16), row1(b_ig[l]), row1(lru_lambda[l]),
            row1(norm_lru_out[l]), (w_out[l], w_ffn_in[l], w_ffn_out[l]), s, ts)
        d_ff = w_ffn_out.shape[1]
        x2 = _attn_ffn(qkv.reshape(bsz, s, 3 * att_w), x2, y_lru, row1(norm_att_out[l]), w_out_b,
                       row1(norm_ffn[l]), w_ffn_in_b, w_ffn_out_b, row1(norm_final),
                       tq, tk, nleft, _largest_chunk(d_ff, 512), l == depth - 1)
    return x2.reshape(bsz, s, d)
```

```python
import functools
import math

import jax
import jax.numpy as jnp
from jax import lax
from jax.experimental import pallas as pl
from jax.experimental.pallas import tpu as pltpu

F32 = jnp.float32
BF16 = jnp.bfloat16

EPS = 1e-6
LRU_C = 8.0
CONV_W = 4
HEAD_DIM = 64

LANES = 128
SUBLANES = 8
MXU_DIM = 256
VMEM_LIMIT_BYTES = 56 * 1024 * 1024

LOG2_E = 1.4426950408889634
ATT_DEAD_LOG2 = -160.0
MASKED_SCORE = 1e30
CAST_STEPS = 8


def _rms_scale(x):
    return lax.rsqrt(jnp.mean(x * x, axis=-1, keepdims=True) + EPS)


def _sigmoid(x):
    return 1.0 / (1.0 + jnp.exp(-x))


def _softplus(x):
    return jnp.maximum(x, 0.0) + jnp.log(1.0 + jnp.exp(-jnp.abs(x)))


def _gelu_tanh(x):
    c = math.sqrt(2.0 / math.pi)
    return 0.5 * x * (1.0 + jnp.tanh(c * (x + 0.044715 * (x * x * x))))


def _inproj_stages(x_ref, g_ref, w_ref, u_slot_ref, qkv_ref, *, lru_cols, col_chunk):
    x = x_ref[...]
    h = ((x * _rms_scale(x)) * g_ref[...]).astype(BF16)
    yield
    for lo in range(0, w_ref.shape[1], col_chunk):
        u = jnp.dot(h, w_ref[:, lo:lo + col_chunk], preferred_element_type=F32)
        if lo < lru_cols:
            u_slot_ref[:, lo:lo + col_chunk] = u
        else:
            qkv_ref[:, lo - lru_cols:lo - lru_cols + col_chunk] = u.astype(BF16)
        yield


def _lru_stages(first, u_ref, cw_ref, cb_ref, wrg_ref, brg_ref, wig_ref, big_ref, lam_ref, gn_ref,
                out_ref, xs_ref, hc_ref, *, ts, width, slab):
    halo = SUBLANES
    xs_ref[0:halo, :] = jnp.where(first, 0.0, xs_ref[ts:ts + halo, :])
    xs_ref[halo:halo + ts, :] = u_ref[:, 0:width]
    row = lax.broadcasted_iota(jnp.int32, (SUBLANES, width), 0)
    hprev = jnp.where(first, 0.0, hc_ref[...])
    neg_sp = -LRU_C * _softplus(-lam_ref[...])

    for s0 in range(0, ts, slab):
        xc = jnp.broadcast_to(cb_ref[...], (slab, width))
        for i in range(CONV_W):
            off = s0 + halo - (CONV_W - 1) + i
            xc = xc + xs_ref[off:off + slab, :] * cw_ref[i:i + 1, :]

        xcb = xc.astype(BF16)
        rg, ig = [], []
        for c in range(width // MXU_DIM):
            sl = slice(c * MXU_DIM, (c + 1) * MXU_DIM)
            gates = jnp.dot(xcb[:, sl], jnp.concatenate([wrg_ref[c], wig_ref[c]], axis=1),
                            preferred_element_type=F32)
            rg.append(gates[:, :MXU_DIM])
            ig.append(gates[:, MXU_DIM:])
        r = _sigmoid(jnp.concatenate(rg, axis=1) + brg_ref[...])
        ig = _sigmoid(jnp.concatenate(ig, axis=1) + big_ref[...])

        log_a = r * neg_sp
        a = jnp.exp(log_a)
        th = jnp.tanh(log_a)
        b = jnp.sqrt((-2.0 * th) / (1.0 - th)) * (ig * xc)
        yield

        hs = []
        for g in range(slab // SUBLANES):
            rows = slice(g * SUBLANES, (g + 1) * SUBLANES)
            a8, b8 = a[rows, :], b[rows, :]
            for d in (1, 2, 4):
                keep = row >= d
                b8 = jnp.where(keep, b8 + a8 * pltpu.roll(b8, d, axis=0), b8)
                a8 = jnp.where(keep, a8 * pltpu.roll(a8, d, axis=0), a8)
            h8 = b8 + a8 * hprev
            hs.append(h8)
            hprev = jnp.broadcast_to(h8[SUBLANES - 1:SUBLANES, :], (SUBLANES, width))
        y = _gelu_tanh(u_ref[s0:s0 + slab, width:2 * width]) * jnp.concatenate(hs, axis=0)
        out_ref[s0:s0 + slab, :] = ((y * _rms_scale(y)) * gn_ref[...]).astype(out_ref.dtype)
        yield
    hc_ref[...] = hprev


def _interleave(*streams):
    streams = list(streams)
    while streams:
        for st in list(streams):
            if next(st, StopIteration) is StopIteration:
                streams.remove(st)


def _inproj_lru_kernel(x_ref, gm_ref, w_ref, cw_ref, cb_ref, wrg_ref, brg_ref, wig_ref, big_ref,
                       lam_ref, gn_ref, *rest, ts, width, tiles_per_seq, col_chunk, slab,
                       cast_slabs):
    nw = len(cast_slabs)
    wsrc_refs, (qkv_ref, ylru_ref) = rest[:nw], rest[nw:nw + 2]
    wdst_refs = rest[nw + 2:2 * nw + 2]
    u_ref, xs_ref, hc_ref = rest[2 * nw + 2:]
    j = pl.program_id(0)
    for src_ref, dst_ref, n_slabs in zip(wsrc_refs, wdst_refs, cast_slabs):
        @pl.when(j < n_slabs)
        def _():
            dst_ref[...] = src_ref[...].astype(dst_ref.dtype)

    slot = j % 2
    lt = jnp.maximum(j - 1, 0)

    @pl.when(j == 0)
    def _():
        for ref in (u_ref, xs_ref, hc_ref):
            ref[...] = jnp.zeros(ref.shape, ref.dtype)

    _interleave(
        _lru_stages(lt % tiles_per_seq == 0, u_ref.at[1 - slot], cw_ref, cb_ref, wrg_ref, brg_ref,
                    wig_ref, big_ref, lam_ref, gn_ref, ylru_ref, xs_ref, hc_ref,
                    ts=ts, width=width, slab=slab),
        _inproj_stages(x_ref, gm_ref, w_ref, u_ref.at[slot], qkv_ref, lru_cols=2 * width,
                       col_chunk=col_chunk))


def _cast_slab_rows(rows, max_slabs):
    step = 2 * SUBLANES
    for slab_rows in range(step, rows + 1, step):
        if rows % slab_rows == 0 and rows // slab_rows <= max_slabs:
            return slab_rows
    raise ValueError(f"no bf16 row slab for {rows} rows in {max_slabs} steps")


def _inproj_lru(x2, gain, w_in_bf16, conv_w, conv_b, wrg_bd, b_rg, wig_bd, b_ig, lam, gain_lru,
                later_weights, seq_len, ts):
    n, d = x2.shape
    cols = w_in_bf16.shape[1]
    width = conv_w.shape[1]
    n_tiles = n // ts
    slab_rows = [_cast_slab_rows(w.shape[0], min(n_tiles, CAST_STEPS)) for w in later_weights]
    cast_slabs = tuple(w.shape[0] // r for w, r in zip(later_weights, slab_rows))
    wslab = lambda w, r: pl.BlockSpec(
        (r, w.shape[1]), lambda j: (jnp.minimum(j, w.shape[0] // r - 1), 0))
    wspecs = [wslab(w, r) for w, r in zip(later_weights, slab_rows)]
    tile = lambda lag: (lambda j: (jnp.clip(j - lag, 0, n_tiles - 1), 0))
    vec = lambda w: pl.BlockSpec((1, w), lambda j: (0, 0))
    wspec = pl.BlockSpec(wrg_bd.shape, lambda j: (0, 0, 0))
    return pl.pallas_call(
        functools.partial(_inproj_lru_kernel, ts=ts, width=width, tiles_per_seq=seq_len // ts,
                          col_chunk=MXU_DIM, slab=LANES // 2, cast_slabs=cast_slabs),
        grid=(n_tiles + 1,),
        in_specs=[
            pl.BlockSpec((ts, d), tile(0)),
            vec(d),
            pl.BlockSpec((d, cols), lambda j: (0, 0), pipeline_mode=pl.Buffered(1)),
            pl.BlockSpec((CONV_W, width), lambda j: (0, 0)),
            vec(width), wspec, vec(width), wspec, vec(width), vec(width), vec(width),
            *wspecs,
        ],
        out_specs=[
            pl.BlockSpec((ts, cols - 2 * width), tile(0)),
            pl.BlockSpec((ts, width), tile(1)),
            *wspecs,
        ],
        out_shape=[
            jax.ShapeDtypeStruct((n, cols - 2 * width), BF16),
            jax.ShapeDtypeStruct((n, width), BF16),
            *[jax.ShapeDtypeStruct(w.shape, BF16) for w in later_weights],
        ],
        scratch_shapes=[
            pltpu.VMEM((2, ts, 2 * width), F32),
            pltpu.VMEM((ts + SUBLANES, width), F32),
            pltpu.VMEM((SUBLANES, width), F32),
        ],
        compiler_params=pltpu.CompilerParams(
            dimension_semantics=("arbitrary",), vmem_limit_bytes=VMEM_LIMIT_BYTES),
        name="inproj_lru",
    )(x2, gain, w_in_bf16, conv_w, conv_b, wrg_bd, b_rg, wig_bd, b_ig, lam, gain_lru,
      *later_weights)


def _block_diag(w, per):
    nb, bw, _ = w.shape
    w = w.reshape(nb // per, per, bw, bw)
    eye = jnp.eye(per, dtype=w.dtype)
    return jnp.einsum('gpij,pq->gpiqj', w, eye).reshape(nb // per, per * bw, per * bw)


def _ffn_stages(x_ref, yl_ref, ya, ga_ref, wo_ref, gf_ref, wi_ref, wd_ref, gl_ref, out_ref, *,
                d_ff, ff_chunk, final_norm):
    ya = ((ya * _rms_scale(ya)) * ga_ref[...]).astype(BF16)
    wl = yl_ref.shape[1]
    x1 = x_ref[...] + jnp.dot(yl_ref[...], wo_ref[0:wl, :], preferred_element_type=F32)
    x1 = x1 + jnp.dot(ya, wo_ref[wl:, :], preferred_element_type=F32)
    h2 = ((x1 * _rms_scale(x1)) * gf_ref[...]).astype(BF16)
    yield
    acts = []
    for c in range(d_ff // ff_chunk):
        lo = c * ff_chunk
        gate = jnp.dot(h2, wi_ref[:, lo:lo + ff_chunk], preferred_element_type=F32)
        up = jnp.dot(h2, wi_ref[:, d_ff + lo:d_ff + lo + ff_chunk], preferred_element_type=F32)
        acts.append(((gate * _sigmoid(gate)) * up).astype(BF16))
        yield
    out = x1 + jnp.dot(jnp.concatenate(acts, axis=1), wd_ref[...], preferred_element_type=F32)
    yield
    if final_norm:
        out = (out * _rms_scale(out)) * gl_ref[...]
    out_ref[...] = out


def _attn_ffn_kernel(cur_ref, prev_ref, qkv_hbm, x_ref, yl_ref, ga_ref,
                     wo_ref, gf_ref, wi_ref, wd_ref, gl_ref, out_ref,
                     yatt_ref, r_ref, acc_ref, kblk_ref, vblk_ref, sem,
                     *, tq, tk, nleft, scale2, tiles_per_seq, n_tiles, att_w, d_ff, ff_chunk,
                     final_norm):
    j = pl.program_id(0)
    ja = jnp.minimum(j, n_tiles - 1)
    bi = ja // tiles_per_seq
    ti = ja % tiles_per_seq
    nrb = tq // tk
    g0 = ti * nrb
    npair = att_w // LANES
    slot = j % 2

    @pl.when(j == 0)
    def _():
        yatt_ref[...] = jnp.zeros(yatt_ref.shape, F32)

    mixer = _ffn_stages(x_ref, yl_ref, yatt_ref[1 - slot], ga_ref, wo_ref, gf_ref, wi_ref, wd_ref,
                        gl_ref, out_ref, d_ff=d_ff, ff_chunk=ff_chunk, final_norm=final_norm)

    rr = lax.broadcasted_iota(jnp.int32, (2 * tk, 2 * tk), 0)
    cc = lax.broadcasted_iota(jnp.int32, (2 * tk, 2 * tk), 1)
    key_j = jnp.where(rr >= tk, rr - tk, rr)
    tri = jnp.where((cc >= tk) | (key_j > cc), 1.0, 0.0).astype(BF16)
    below_diag = (lax.broadcasted_iota(jnp.int32, (tk, tk), 1)
                  < lax.broadcasted_iota(jnp.int32, (tk, tk), 0))
    has_left = ti > 0

    def softplus_parts(z):
        lb = jnp.minimum(z, 0.0) - jnp.log(1.0 + jnp.exp2(-jnp.abs(z))) * LOG2_E
        return lb, lb - z

    def suffix_and_total(l):
        hi = l.astype(BF16)
        lo = (l - hi.astype(F32)).astype(BF16)
        cs = jnp.dot(jnp.concatenate([hi, lo], axis=1), tri, preferred_element_type=F32)
        return cs[:, :tk], cs[:, tk:]

    def pair_queries(g):
        return (cur_ref[0, :, g * LANES:(g + 1) * LANES].astype(F32) * scale2).astype(BF16)

    def by_head(kv):
        lane = lax.broadcasted_iota(jnp.int32, kv.shape, 1)
        zero = jnp.zeros_like(kv)
        return jnp.where(lane < HEAD_DIM, kv, zero), jnp.where(lane >= HEAD_DIM, kv, zero)

    def pair_scores(q, k_a, k_b):
        return lax.dot_general(q, jnp.concatenate([k_a, k_b], axis=0), (((1,), (1,)), ((), ())),
                               preferred_element_type=F32)

    def pair_output(att_a, att_b, v_a, v_b):
        return jnp.dot(jnp.concatenate([att_a, att_b], axis=1),
                       jnp.concatenate([v_a, v_b], axis=0), preferred_element_type=F32)

    def attention_stages():
        nk = (nleft + 1) * tk
        for g in range(npair):
            q = pair_queries(g)
            window = lambda part: jnp.concatenate(
                [ref[0, :, part * att_w + g * LANES:part * att_w + (g + 1) * LANES]
                 for ref in (prev_ref, cur_ref)], axis=0)
            k_heads = by_head(window(1))
            v_heads = by_head(window(2))
            chains = []
            for rb in range(nrb):
                rows = slice(rb * tk, (rb + 1) * tk)
                win = slice(rb * tk, (rb + nleft + 1) * tk)
                z = pair_scores(q[rows, :], k_heads[0][win, :], k_heads[1][win, :])
                chains.append(dict(rb=rb, rows=rows, v=(v_heads[0][win, :], v_heads[1][win, :]),
                                   z=(z[:, :nk], z[:, nk:])))
            yield
            for c in chains:
                c["lb"], c["parts"] = ([], []), ([], [])
                for h in range(2):
                    for kb in range(nleft + 1):
                        zb = c["z"][h][:, kb * tk:(kb + 1) * tk]
                        if kb == nleft:
                            zb = jnp.where(below_diag, zb, -MASKED_SCORE)
                        elif c["rb"] + kb < nleft:
                            zb = jnp.where(has_left, zb, -MASKED_SCORE)
                        lb, l = softplus_parts(zb)
                        c["lb"][h].append(lb)
                        c["parts"][h].append(suffix_and_total(l))
            yield
            for c in chains:
                c["att"] = []
                for h in range(2):
                    r = jnp.zeros((tk, tk), F32)
                    atts = [None] * (nleft + 1)
                    for kb in reversed(range(nleft + 1)):
                        suffix, total = c["parts"][h][kb]
                        atts[kb] = jnp.exp2(c["lb"][h][kb] + suffix + r).astype(BF16)
                        r = r + total
                    c["att"].append(jnp.concatenate(atts, axis=1))
                    r_ref[2 * g + h, c["rows"], :] = r
            yield
            for c in chains:
                acc_ref[g, c["rows"], :] = pair_output(c["att"][0], c["att"][1], *c["v"])
            yield

    _interleave(attention_stages(), mixer)

    def cond(carry):
        s, alive = carry
        return jnp.logical_and(s <= g0 + (nrb - 1 - nleft - 1), alive)

    def body(carry):
        s, _ = carry
        for g in range(npair):
            q = pair_queries(g)
            for rb in range(nrb):
                jb = g0 + (rb - nleft - 1) - s
                rows = slice(rb * tk, (rb + 1) * tk)
                rows_alive = jnp.max(r_ref[2 * g:2 * g + 2, rows, :]) >= ATT_DEAD_LOG2

                @pl.when(jnp.logical_and(jb >= 0, rows_alive))
                def _():
                    ks = pl.multiple_of(jb * tk, tk)
                    copies = [
                        pltpu.make_async_copy(
                            qkv_hbm.at[bi, pl.ds(ks, tk), pl.ds((1 + p) * att_w + g * LANES, LANES)],
                            dst, sem.at[p])
                        for p, dst in enumerate((kblk_ref, vblk_ref))]
                    for cp in copies:
                        cp.start()
                    for cp in copies:
                        cp.wait()
                    z = pair_scores(q[rows, :], *by_head(kblk_ref[...]))
                    atts = []
                    for h in range(2):
                        lb, l = softplus_parts(z[:, h * tk:(h + 1) * tk])
                        suffix, total = suffix_and_total(l)
                        r_old = r_ref[2 * g + h, rows, :]
                        atts.append(jnp.exp2(lb + suffix + r_old).astype(BF16))
                        r_ref[2 * g + h, rows, :] = r_old + total
                    acc_ref[g, rows, :] += pair_output(*atts, *by_head(vblk_ref[...]))
        return s + 1, jnp.max(r_ref[...]) >= ATT_DEAD_LOG2

    lax.while_loop(cond, body, (0, jnp.max(r_ref[...]) >= ATT_DEAD_LOG2))

    for g in range(npair):
        yatt_ref[slot, :, g * LANES:(g + 1) * LANES] = acc_ref[g]


def _attn_ffn(qkv, x2, y_lru, g_att, w_out, g_ffn, w_ffn_in, w_ffn_out, g_last, tq, tk, nleft,
              ff_chunk, final_norm):
    bsz, s, three_w = qkv.shape
    att_w = three_w // 3
    n, d = x2.shape
    d_ff = w_ffn_out.shape[0]
    tiles_per_seq = s // tq
    n_tiles = bsz * tiles_per_seq
    left = nleft * tk
    assert s % tq == 0 and tq % tk == 0 and tq // tk >= nleft and tq % left == 0
    assert att_w % LANES == 0

    def tile(j):
        ja = jnp.minimum(j, n_tiles - 1)
        return ja // tiles_per_seq, ja % tiles_per_seq

    def prev_index(j):
        b, t = tile(j)
        return b, jnp.maximum(t * (tq // left) - 1, 0), 0

    cur = pl.BlockSpec((1, tq, three_w), lambda j: (*tile(j), 0))
    prev = pl.BlockSpec((1, left, three_w), prev_index)

    row = lambda w: pl.BlockSpec((tq, w), lambda j: (jnp.maximum(j - 1, 0), 0))
    vec = lambda w: pl.BlockSpec((1, w), lambda j: (0, 0))
    whole = lambda a: pl.BlockSpec(a.shape, lambda j: (0, 0), pipeline_mode=pl.Buffered(1))
    nheads = att_w // HEAD_DIM
    return pl.pallas_call(
        functools.partial(_attn_ffn_kernel, tq=tq, tk=tk, nleft=nleft,
                          scale2=LOG2_E / math.sqrt(HEAD_DIM), tiles_per_seq=tiles_per_seq,
                          n_tiles=n_tiles, att_w=att_w, d_ff=d_ff, ff_chunk=ff_chunk,
                          final_norm=final_norm),
        grid=(n_tiles + 1,),
        in_specs=[cur, prev, pl.BlockSpec(memory_space=pl.ANY),
                  row(d), row(y_lru.shape[1]), vec(att_w), whole(w_out), vec(d),
                  whole(w_ffn_in), whole(w_ffn_out), vec(d)],
        out_specs=row(d),
        out_shape=jax.ShapeDtypeStruct((n, d), F32),
        scratch_shapes=[
            pltpu.VMEM((2, tq, att_w), F32),
            pltpu.VMEM((nheads, tq, tk), F32),
            pltpu.VMEM((att_w // LANES, tq, LANES), F32),
            pltpu.VMEM((tk, LANES), BF16),
            pltpu.VMEM((tk, LANES), BF16),
            pltpu.SemaphoreType.DMA((2,)),
        ],
        compiler_params=pltpu.CompilerParams(
            dimension_semantics=("arbitrary",), vmem_limit_bytes=VMEM_LIMIT_BYTES),
        name="attn_ffn",
    )(qkv, qkv, qkv, x2, y_lru, g_att, w_out, g_ffn, w_ffn_in, w_ffn_out, g_last)


def _largest_chunk(total, limit):
    best = LANES
    for c in range(LANES, limit + 1, LANES):
        if total % c == 0:
            best = c
    return best


def kernel(x, norm_mix, w_in, conv_w, conv_b, w_rg, b_rg, w_ig, b_ig, lru_lambda, norm_lru_out,
           norm_att_out, w_out, norm_ffn, w_ffn_in, w_ffn_out, norm_final):
    bsz, s, d = x.shape
    depth = w_in.shape[0]
    lru_w = conv_w.shape[2]
    att_w = norm_att_out.shape[1]
    blk_w = w_rg.shape[2]
    assert w_in.shape[2] == 2 * lru_w + 3 * att_w and att_w % LANES == 0
    assert MXU_DIM % blk_w == 0 and lru_w % MXU_DIM == 0

    n = bsz * s
    ts = 512
    assert s % ts == 0
    tq, tk, nleft = 512, 128, 2
    per = MXU_DIM // blk_w
    row1 = lambda v: v.reshape(1, -1)

    x2 = x.reshape(n, d)
    for l in range(depth):
        qkv, y_lru, w_out_b, w_ffn_in_b, w_ffn_out_b = _inproj_lru(
            x2, row1(norm_mix[l]), w_in[l].astype(BF16), conv_w[l], row1(conv_b[l]),
            _block_diag(w_rg[l], per).astype(BF16), row1(b_rg[l]),
            _block_diag(w_ig[l], per).astype(BF16), row1(b_ig[l]), row1(lru_lambda[l]),
            row1(norm_lru_out[l]), (w_out[l], w_ffn_in[l], w_ffn_out[l]), s, ts)
        d_ff = w_ffn_out.shape[1]
        x2 = _attn_ffn(qkv.reshape(bsz, s, 3 * att_w), x2, y_lru, row1(norm_att_out[l]), w_out_b,
                       row1(norm_ffn[l]), w_ffn_in_b, w_ffn_out_b, row1(norm_final),
                       tq, tk, nleft, _largest_chunk(d_ff, 512), l == depth - 1)
    return x2.reshape(bsz, s, d)
```
